```python
import jax
import jax.numpy as jnp
from jax import lax
import numpy as np

D_MODEL = 1024
BATCH = 4
SEQ = 4096
DEPTH = 2

CTX_LEN = 256
GRID_W = 64
F32 = jnp.float32

MLSTM_H = 4
MLSTM_DH = 64
MLSTM_W = MLSTM_H * MLSTM_DH
MLSTM_CHUNK = 64
CONV_W = 3
RWKV_H = 4
RWKV_N = 64
RWKV_W = RWKV_H * RWKV_N
DECAY_LORA = 64
AAA_LORA = 64
GATE_LORA = 128
RWKV_GN_EPS = 64e-5
MLA_H = 8
MLA_NOPE = 64
MLA_ROPE = 32
MLA_V = 64
MLA_W = MLA_H * MLA_V
Q_RANK = 256
KV_RANK = 128
ROPE_THETA = 10000.0
Q_BLOCK = 128
D_MIX = MLSTM_W + RWKV_W + MLA_W
MLSTM_COLS = (2 * MLSTM_W, MLSTM_W, MLSTM_W, 4 * MLSTM_H)
RWKV_COLS = (RWKV_W, RWKV_W, RWKV_W, 2 * DECAY_LORA, 2 * AAA_LORA, GATE_LORA)
MLA_COLS = (Q_RANK, KV_RANK, MLA_ROPE)
N_MLSTM_IN = 4 * MLSTM_W + 4 * MLSTM_H
N_RWKV_IN = 3 * RWKV_W + 2 * DECAY_LORA + 2 * AAA_LORA + GATE_LORA
N_MLA_IN = Q_RANK + KV_RANK + MLA_ROPE
N_IN = N_MLSTM_IN + N_RWKV_IN + N_MLA_IN
N_EXPERTS = 64
TOP_K = 8
N_GROUPS = 8
TOPK_GROUPS = 4
D_EXPERT = 256
ROUTED_SCALE = 2.5
MOE_BLOCK = 256
DEEPNORM_ALPHA = (2 * DEPTH) ** 0.25
DEEPNORM_BETA = (8 * DEPTH) ** -0.25
LN_EPS = 1e-6

kernel_name = 'hybrid_mlstm_rwkv7_mla_moe_dit_block'


def _split(z, sizes):
    return jnp.split(z, np.cumsum(sizes)[:-1].tolist(), axis=-1)


def _ln(x):
    xf = x.astype(F32)
    xc = xf - xf.mean(-1, keepdims=True)
    return xc * lax.rsqrt(jnp.mean(xc * xc, -1, keepdims=True) + LN_EPS)


def _modulate(x, shift, scale):
    return (_ln(x) * (1.0 + scale) + shift).astype(x.dtype)


def _post_norm(x_res, y, gate, w, b):
    return (_ln(DEEPNORM_ALPHA * x_res + gate * y) * w + b).astype(x_res.dtype)


def _rms(x, w):
    xf = x.astype(F32)
    return (xf * lax.rsqrt(jnp.mean(xf * xf, -1, keepdims=True) + 1e-6) * w).astype(x.dtype)


def _head_normalize(x, eps):
    xf = x.astype(F32)
    xc = xf - xf.mean(-1, keepdims=True)
    return xc * lax.rsqrt(jnp.mean(xc * xc, -1, keepdims=True) + eps)


def _centred_conv(z, w):
    t = z.shape[1]
    zp = jnp.pad(z, ((0, 0), (CONV_W // 2, CONV_W // 2), (0, 0)))
    return sum(w[j] * zp[:, j:j + t] for j in range(CONV_W))


def _bidir_token_shift(z, mu):
    zp = jnp.pad(z, ((0, 0), (1, 1), (0, 0)))
    return z + mu * (0.5 * (zp[:, :-2] + zp[:, 2:]) - z)


def _rope_tables(n):
    ROWS = n // GRID_W
    row = jnp.repeat(jnp.arange(ROWS), GRID_W).astype(F32)
    col = jnp.tile(jnp.arange(GRID_W), ROWS).astype(F32)
    n_freq = MLA_ROPE // 4
    freq = ROPE_THETA ** (-jnp.arange(n_freq, dtype=F32) / n_freq)
    ang = jnp.concatenate([row[:, None] * freq, col[:, None] * freq], -1)
    return jnp.cos(ang), jnp.sin(ang)


def _rope(x, cos, sin):
    xf = x.astype(F32).reshape(x.shape[:-1] + (-1, 2))
    x0, x1 = xf[..., 0], xf[..., 1]
    out = jnp.stack([x0 * cos - x1 * sin, x0 * sin + x1 * cos], -1)
    return out.reshape(x.shape).astype(x.dtype)


def _mlstm_chunked(q, k, v, log_i, log_f, state, emit):
    b2, h, t, dh = q.shape
    cl = MLSTM_CHUNK
    nc = t // cl
    q, k, v = (a.reshape(b2, h, nc, cl, dh) for a in (q, k, v))
    li = log_i.reshape(b2, h, nc, cl)
    bcum = jnp.cumsum(log_f.reshape(b2, h, nc, cl), axis=-1)
    b_end = bcum[..., -1]
    w_end = b_end[..., None] - bcum + li
    m_loc = w_end.max(-1)
    e_end = jnp.exp(w_end - m_loc[..., None])
    c_loc = jnp.einsum('bhcl,bhclv,bhclk->bhcvk', e_end, v, k)
    n_loc = jnp.einsum('bhcl,bhclk->bhck', e_end, k)

    def step(carry, xs):
        c_st, n_st, m_st = carry
        c_l, n_l, m_l, b_l = xs
        m_new = jnp.maximum(b_l + m_st, m_l)
        a_old = jnp.exp(b_l + m_st - m_new)
        a_loc = jnp.exp(m_l - m_new)
        c_new = a_old[..., None, None] * c_st + a_loc[..., None, None] * c_l
        n_new = a_old[..., None] * n_st + a_loc[..., None] * n_l
        return (c_new, n_new, m_new), (c_st, n_st, m_st)

    xs = tuple(jnp.moveaxis(a, 2, 0) for a in (c_loc, n_loc, m_loc, b_end))
    final, prev = lax.scan(step, state, xs)
    if not emit:
        return None, final
    c_prev, n_prev, m_prev = (jnp.moveaxis(a, 0, 2) for a in prev)
    lower = jnp.tril(jnp.ones((cl, cl), dtype=bool))
    d_log = jnp.where(lower, bcum[..., :, None] - bcum[..., None, :] + li[..., None, :], -jnp.inf)
    g_log = bcum + m_prev[..., None]
    m_row = jnp.maximum(g_log, d_log.max(-1))
    w_intra = jnp.exp(d_log - m_row[..., None]) * jnp.einsum('bhcjd,bhcsd->bhcjs', q, k)
    e_inter = jnp.exp(g_log - m_row)
    num = (jnp.einsum('bhcjs,bhcsv->bhcjv', w_intra, v)
           + e_inter[..., None] * jnp.einsum('bhcvk,bhcjk->bhcjv', c_prev, q))
    den = w_intra.sum(-1) + e_inter * jnp.einsum('bhck,bhcjk->bhcj', n_prev, q)
    out = num / jnp.maximum(jnp.abs(den), jnp.exp(-m_row))[..., None]
    return out.reshape(b2, h, t, dh), final


def _mlstm_prepare(z, conv_w, gate_bias):
    b, t, _ = z.shape
    zqk, zv, zo, zg = _split(z, MLSTM_COLS)
    zq, zk = jnp.split(jax.nn.silu(_centred_conv(zqk, conv_w)), 2, axis=-1)
    heads = lambda a: a.astype(F32).reshape(b, t, MLSTM_H, MLSTM_DH).transpose(0, 2, 1, 3)
    q = heads(zq) * MLSTM_DH ** -0.5
    k = heads(zk)
    v = heads(zv)
    g = (zg.astype(F32).reshape(b, t, 4, MLSTM_H) + gate_bias).transpose(2, 0, 3, 1)
    log_i = jnp.concatenate([g[0], jnp.flip(g[2], -1)], 0)
    log_f = jax.nn.log_sigmoid(jnp.concatenate([g[1], jnp.flip(g[3], -1)], 0))
    both = lambda a: jnp.concatenate([a, jnp.flip(a, 2)], 0)
    return (both(q), both(k), both(v), log_i, log_f), zo


def _mlstm_mixer(z_lat, z_ctx, p, emit_ctx):
    b = z_lat.shape[0]
    ctx_in, o_ctx = _mlstm_prepare(z_ctx, p['mlstm_conv'], p['mlstm_gate_bias'])
    lat_in, o_lat = _mlstm_prepare(z_lat, p['mlstm_conv'], p['mlstm_gate_bias'])
    state0 = (jnp.zeros((2 * b, MLSTM_H, MLSTM_DH, MLSTM_DH), F32),
              jnp.zeros((2 * b, MLSTM_H, MLSTM_DH), F32),
              jnp.zeros((2 * b, MLSTM_H), F32))
    h_ctx, state_ctx = _mlstm_chunked(*ctx_in, state0, emit_ctx)
    h_lat, _ = _mlstm_chunked(*lat_in, state_ctx, True)

    def readout(h2, zo):
        hs = (h2[:b] + jnp.flip(h2[b:], 2)).transpose(0, 2, 1, 3)
        hs = _head_normalize(hs, 1e-6) * p['mlstm_norm_w'].reshape(MLSTM_H, MLSTM_DH)
        return (jax.nn.sigmoid(zo.astype(F32)) * hs.reshape(zo.shape)).astype(zo.dtype)

    y_ctx = readout(h_ctx, o_ctx) if emit_ctx else None
    return readout(h_lat, o_lat), y_ctx


def _rwkv_prepare(z, p):
    b, t, _ = z.shape
    z = _bidir_token_shift(z.astype(F32), p['rwkv_mu'])
    zr, zk, zv, zw, za, zg = _split(z, RWKV_COLS)
    heads = lambda a: a.reshape(b, t, RWKV_H, RWKV_N)
    w_raw = p['rwkv_w0'] + jnp.einsum('btdr,drc->btdc', jnp.tanh(zw.reshape(b, t, 2, DECAY_LORA)), p['rwkv_w_up'])
    decay = jnp.exp(-jnp.exp(-jax.nn.softplus(-w_raw) - 0.5))
    a = jax.nn.sigmoid(p['rwkv_a0'] + jnp.einsum('btdr,drc->btdc', za.reshape(b, t, 2, AAA_LORA), p['rwkv_a_up']))
    g = jax.nn.sigmoid(zg) @ p['rwkv_g_up']
    kk = heads(zk * p['rwkv_k_k'])
    kk = kk * lax.rsqrt(jnp.maximum(jnp.sum(kk * kk, -1, keepdims=True), 1e-12))
    k_dir = (zk[:, :, None, :] * (1.0 + (a - 1.0) * p['rwkv_k_a'])).reshape(b, t, 2, RWKV_H, RWKV_N)
    r = heads(zr)
    v = heads(zv)
    bonus = jnp.einsum('bthn,btdhn->bth', r * p['rwkv_r_k'].reshape(RWKV_H, RWKV_N), k_dir)[..., None] * v
    both = lambda f, bw: jnp.concatenate([f, jnp.flip(bw, 1)], 0)
    seqs = (both(r, r),
            both(heads(decay[:, :, 0]), heads(decay[:, :, 1])),
            both(k_dir[:, :, 0], k_dir[:, :, 1]),
            both(v, v),
            both(kk, kk),
            both(heads(a[:, :, 0]), heads(a[:, :, 1])))
    return seqs, g, bonus


def _rwkv_scan(s0, seqs, emit):
    xs = tuple(jnp.moveaxis(a, 1, 0) for a in seqs)

    def step(s, inp):
        r, w, k, v, kk, a = inp
        s = (s * w[:, :, None, :]
             + jnp.einsum('bhvk,bhk->bhv', s, -kk)[..., None] * (kk * a)[:, :, None, :]
             + v[..., None] * k[:, :, None, :])
        return s, (jnp.einsum('bhvk,bhk->bhv', s, r) if emit else None)

    s, out = lax.scan(step, s0, xs)
    return (jnp.moveaxis(out, 0, 1) if emit else None), s


def _rwkv_mixer(z_lat, z_ctx, p, emit_ctx):
    b = z_lat.shape[0]
    seq_c, g_c, bonus_c = _rwkv_prepare(z_ctx, p)
    seq_l, g_l, bonus_l = _rwkv_prepare(z_lat, p)
    s0 = jnp.zeros((2 * b, RWKV_H, RWKV_N, RWKV_N), F32)
    o_c, s_ctx = _rwkv_scan(s0, seq_c, emit_ctx)
    o_l, _ = _rwkv_scan(s_ctx, seq_l, True)

    def readout(o2, g, bonus, dtype):
        o = o2[:b] + jnp.flip(o2[b:], 1)
        o = (_head_normalize(o, RWKV_GN_EPS) * p['rwkv_ln_w'].reshape(RWKV_H, RWKV_N)
             + p['rwkv_ln_b'].reshape(RWKV_H, RWKV_N) + bonus)
        return (o.reshape(g.shape) * g).astype(dtype)

    y_ctx = readout(o_c, g_c, bonus_c, z_ctx.dtype) if emit_ctx else None
    return readout(o_l, g_l, bonus_l, z_lat.dtype), y_ctx


def _mla_project(z, p, cos=None, sin=None):
    b, t, _ = z.shape
    zq, zkv, k_rope = _split(z, MLA_COLS)
    q = (_rms(zq, p['mla_q_norm']) @ p['mla_w_uq']).reshape(b, t, MLA_H, MLA_NOPE + MLA_ROPE)
    kv = (_rms(zkv, p['mla_kv_norm']) @ p['mla_w_ukv']).reshape(b, t, MLA_H, MLA_NOPE + MLA_V)
    q_nope, q_rope = q[..., :MLA_NOPE], q[..., MLA_NOPE:]
    k_nope, v = kv[..., :MLA_NOPE], kv[..., MLA_NOPE:]
    if cos is not None:
        q_rope = _rope(q_rope, cos[:, None], sin[:, None])
        k_rope = _rope(k_rope, cos, sin)
    q = jnp.concatenate([q_nope, q_rope], -1)
    k = jnp.concatenate([k_nope, jnp.broadcast_to(k_rope[:, :, None, :], (b, t, MLA_H, MLA_ROPE))], -1)
    return q.transpose(0, 2, 1, 3), k.transpose(0, 2, 1, 3), v.transpose(0, 2, 1, 3)


def _softmax_attend(q, k, v):
    s = jnp.einsum('bhqd,bhkd->bhqk', q.astype(F32), k.astype(F32)) * (q.shape[-1] ** -0.5)
    return jnp.einsum('bhqk,bhkd->bhqd', jax.nn.softmax(s, axis=-1).astype(v.dtype), v)


def _mla_mixer(z_lat, z_ctx, p, cos, sin, emit_ctx):
    q_c, k_c, v_c = _mla_project(z_ctx, p)
    q_l, k_l, v_l = _mla_project(z_lat, p, cos, sin)
    b, h, t, dk = q_l.shape
    k_all = jnp.concatenate([k_l, k_c], 2)
    v_all = jnp.concatenate([v_l, v_c], 2)
    q_blocks = jnp.moveaxis(q_l.reshape(b, h, t // Q_BLOCK, Q_BLOCK, dk), 2, 0)
    o = lax.map(lambda qb: _softmax_attend(qb, k_all, v_all), q_blocks)
    y_lat = jnp.moveaxis(o, 0, 2).reshape(b, h, t, MLA_V).transpose(0, 2, 1, 3).reshape(b, t, MLA_W)
    y_ctx = None
    if emit_ctx:
        y_ctx = _softmax_attend(q_c, k_c, v_c).transpose(0, 2, 1, 3).reshape(b, z_ctx.shape[1], MLA_W)
    return y_lat, y_ctx


def _token_mixers(h_lat, h_ctx, p, cos, sin, emit_ctx):
    z_lat = h_lat @ p['w_in']
    z_ctx = h_ctx @ p['w_in']
    zl_m, zl_r, zl_a = _split(z_lat, (N_MLSTM_IN, N_RWKV_IN, N_MLA_IN))
    zc_m, zc_r, zc_a = _split(z_ctx, (N_MLSTM_IN, N_RWKV_IN, N_MLA_IN))
    m_l, m_c = _mlstm_mixer(zl_m, zc_m, p, emit_ctx)
    r_l, r_c = _rwkv_mixer(zl_r, zc_r, p, emit_ctx)
    a_l, a_c = _mla_mixer(zl_a, zc_a, p, cos, sin, emit_ctx)
    y_lat = jnp.concatenate([m_l, r_l, a_l], -1) @ p['w_out']
    y_ctx = jnp.concatenate([m_c, r_c, a_c], -1) @ p['w_out'] if emit_ctx else None
    return y_lat, y_ctx


def _moe(h, p):
    t, d = h.shape
    e_n = N_EXPERTS
    scores = jax.nn.sigmoid(h.astype(F32) @ p['router_w'].astype(F32))
    biased = scores + p['router_bias'].astype(F32)
    group_score = lax.top_k(biased.reshape(t, N_GROUPS, e_n // N_GROUPS), 2)[0].sum(-1)
    _, top_groups = lax.top_k(group_score, TOPK_GROUPS)
    group_mask = jax.nn.one_hot(top_groups, N_GROUPS, dtype=F32).sum(1)
    allowed = jnp.repeat(group_mask, e_n // N_GROUPS, axis=1) > 0
    _, eidx = lax.top_k(jnp.where(allowed, biased, -jnp.inf), TOP_K)
    sel = jnp.take_along_axis(scores, eidx, 1)
    gates = ROUTED_SCALE * sel / sel.sum(-1, keepdims=True)
    n_assign = t * TOP_K
    flat_e = eidx.reshape(n_assign)
    flat_t = jnp.repeat(jnp.arange(t, dtype=jnp.int32), TOP_K)
    flat_g = gates.reshape(n_assign)
    order = jnp.argsort(flat_e)
    sorted_e = flat_e[order]
    counts = jnp.bincount(flat_e, length=e_n)
    padded = (counts + MOE_BLOCK - 1) // MOE_BLOCK * MOE_BLOCK
    ends = jnp.cumsum(padded)
    dest = (ends - padded)[sorted_e] + jnp.arange(n_assign) - (jnp.cumsum(counts) - counts)[sorted_e]
    n_blocks = -(-(n_assign + e_n * (MOE_BLOCK - 1)) // MOE_BLOCK)
    n_rows = n_blocks * MOE_BLOCK
    tok = jnp.full((n_rows,), t, jnp.int32).at[dest].set(flat_t[order])
    gate = jnp.zeros((n_rows,), h.dtype).at[dest].set(flat_g[order].astype(h.dtype))
    block_e = jnp.minimum(jnp.searchsorted(ends, jnp.arange(n_blocks) * MOE_BLOCK, side='right'), e_n - 1)
    h_pad = jnp.concatenate([h, jnp.zeros((1, d), h.dtype)], 0)

    def expert_block(args):
        tb, gb, e = args
        xb = h_pad[tb]
        y = (jax.nn.silu(xb @ p['exp_w_gate'][e]) * (xb @ p['exp_w_up'][e])) @ p['exp_w_down'][e]
        return y * gb[:, None]

    ys = lax.map(expert_block, (tok.reshape(n_blocks, MOE_BLOCK), gate.reshape(n_blocks, MOE_BLOCK), block_e))
    routed = jnp.zeros((t + 1, d), h.dtype).at[tok].add(ys.reshape(n_rows, d))[:t]
    shared = (jax.nn.silu(h @ p['sh_w_gate']) * (h @ p['sh_w_up'])) @ p['sh_w_down']
    return routed + shared


def setup_inputs(seed: int = 0) -> dict:
    key = jax.random.key(seed)
    ks = iter(jax.random.split(key, 48))
    nrm = lambda shape, scale: jax.random.normal(next(ks), shape, F32) * scale
    L, D = DEPTH, D_MODEL
    gate_i = nrm((L, 2, MLSTM_H), 0.1)
    gate_f = jax.random.uniform(next(ks), (L, 2, MLSTM_H), F32, 3.0, 6.0)
    mlstm_gate_bias = jnp.stack([gate_i[:, 0], gate_f[:, 0], gate_i[:, 1], gate_f[:, 1]], 1)
    return {
        'x': nrm((BATCH, SEQ, D), 1.0),
        'c': nrm((BATCH, D), 1.0),
        'ctx': nrm((BATCH, CTX_LEN, D), 1.0),
        'c_ctx': nrm((D,), 1.0),
        'w_mod': nrm((L, D, 6 * D), 0.5 * D ** -0.5),
        'b_mod': nrm((L, 6 * D), 0.01),
        'w_in': nrm((L, D, N_IN), D ** -0.5),
        'mlstm_conv': nrm((L, CONV_W, 2 * MLSTM_W), CONV_W ** -0.5),
        'mlstm_gate_bias': mlstm_gate_bias,
        'mlstm_norm_w': 1.0 + nrm((L, MLSTM_W), 0.1),
        'rwkv_mu': jax.random.uniform(next(ks), (L, N_RWKV_IN), F32),
        'rwkv_w0': jax.random.uniform(next(ks), (L, 2, RWKV_W), F32, -3.0, 1.0),
        'rwkv_w_up': nrm((L, 2, DECAY_LORA, RWKV_W), 0.5 * DECAY_LORA ** -0.5),
        'rwkv_a0': nrm((L, 2, RWKV_W), 0.5),
        'rwkv_a_up': nrm((L, 2, AAA_LORA, RWKV_W), 0.5 * AAA_LORA ** -0.5),
        'rwkv_g_up': nrm((L, GATE_LORA, RWKV_W), GATE_LORA ** -0.5),
        'rwkv_k_k': 0.85 + nrm((L, RWKV_W), 0.1),
        'rwkv_k_a': 1.0 + nrm((L, RWKV_W), 0.1),
        'rwkv_r_k': nrm((L, RWKV_W), 0.1),
        'rwkv_ln_w': 1.0 + nrm((L, RWKV_W), 0.1),
        'rwkv_ln_b': nrm((L, RWKV_W), 0.01),
        'mla_q_norm': 1.0 + nrm((L, Q_RANK), 0.1),
        'mla_kv_norm': 1.0 + nrm((L, KV_RANK), 0.1),
        'mla_w_uq': nrm((L, Q_RANK, MLA_H * (MLA_NOPE + MLA_ROPE)), Q_RANK ** -0.5),
        'mla_w_ukv': nrm((L, KV_RANK, MLA_H * (MLA_NOPE + MLA_V)), KV_RANK ** -0.5),
        'w_out': nrm((L, D_MIX, D), DEEPNORM_BETA * D_MIX ** -0.5),
        'ln1_w': 1.0 + nrm((L, D), 0.1),
        'ln1_b': nrm((L, D), 0.01),
        'router_w': nrm((L, D, N_EXPERTS), D ** -0.5),
        'router_bias': nrm((L, N_EXPERTS), 0.01),
        'exp_w_gate': nrm((L, N_EXPERTS, D, D_EXPERT), D ** -0.5),
        'exp_w_up': nrm((L, N_EXPERTS, D, D_EXPERT), D ** -0.5),
        'exp_w_down': nrm((L, N_EXPERTS, D_EXPERT, D), DEEPNORM_BETA * D_EXPERT ** -0.5),
        'sh_w_gate': nrm((L, D, D_EXPERT), D ** -0.5),
        'sh_w_up': nrm((L, D, D_EXPERT), D ** -0.5),
        'sh_w_down': nrm((L, D_EXPERT, D), DEEPNORM_BETA * D_EXPERT ** -0.5),
        'ln2_w': 1.0 + nrm((L, D), 0.1),
        'ln2_b': nrm((L, D), 0.01),
    }


def reference(x, c, ctx, c_ctx, w_mod, b_mod, w_in, mlstm_conv, mlstm_gate_bias, mlstm_norm_w,
              rwkv_mu, rwkv_w0, rwkv_w_up, rwkv_a0, rwkv_a_up, rwkv_g_up, rwkv_k_k, rwkv_k_a, rwkv_r_k,
              rwkv_ln_w, rwkv_ln_b, mla_q_norm, mla_kv_norm, mla_w_uq, mla_w_ukv, w_out, ln1_w, ln1_b,
              router_w, router_bias, exp_w_gate, exp_w_up, exp_w_down, sh_w_gate, sh_w_up, sh_w_down,
              ln2_w, ln2_b):
    b, n_lat, d = x.shape
    n_ctx = ctx.shape[1]
    cos, sin = _rope_tables(n_lat)
    xc = ctx
    for l in range(DEPTH):
        last = l == DEPTH - 1
        p = {
            'w_in': w_in[l], 'w_out': w_out[l],
            'mlstm_conv': mlstm_conv[l], 'mlstm_gate_bias': mlstm_gate_bias[l], 'mlstm_norm_w': mlstm_norm_w[l],
            'rwkv_mu': rwkv_mu[l], 'rwkv_w0': rwkv_w0[l], 'rwkv_w_up': rwkv_w_up[l], 'rwkv_a0': rwkv_a0[l],
            'rwkv_a_up': rwkv_a_up[l], 'rwkv_g_up': rwkv_g_up[l], 'rwkv_k_k': rwkv_k_k[l], 'rwkv_k_a': rwkv_k_a[l],
            'rwkv_r_k': rwkv_r_k[l], 'rwkv_ln_w': rwkv_ln_w[l], 'rwkv_ln_b': rwkv_ln_b[l],
            'mla_q_norm': mla_q_norm[l], 'mla_kv_norm': mla_kv_norm[l], 'mla_w_uq': mla_w_uq[l], 'mla_w_ukv': mla_w_ukv[l],
        }
        moe_p = {
            'router_w': router_w[l], 'router_bias': router_bias[l], 'exp_w_gate': exp_w_gate[l],
            'exp_w_up': exp_w_up[l], 'exp_w_down': exp_w_down[l], 'sh_w_gate': sh_w_gate[l],
            'sh_w_up': sh_w_up[l], 'sh_w_down': sh_w_down[l],
        }
        sh1, sc1, g1, sh2, sc2, g2 = [m[:, None, :] for m in jnp.split(jax.nn.silu(c) @ w_mod[l] + b_mod[l], 6, axis=-1)]
        csh1, csc1, cg1, csh2, csc2, cg2 = jnp.split(jax.nn.silu(c_ctx) @ w_mod[l] + b_mod[l], 6, axis=-1)
        y_lat, y_ctx = _token_mixers(_modulate(x, sh1, sc1), _modulate(xc, csh1, csc1), p, cos, sin, not last)
        x = _post_norm(x, y_lat, g1, ln1_w[l], ln1_b[l])
        h2_lat = _modulate(x, sh2, sc2).reshape(b * n_lat, d)
        if last:
            f_lat = _moe(h2_lat, moe_p)
        else:
            xc = _post_norm(xc, y_ctx, cg1, ln1_w[l], ln1_b[l])
            h2_ctx = _modulate(xc, csh2, csc2).reshape(b * n_ctx, d)
            f_all = _moe(jnp.concatenate([h2_lat, h2_ctx], 0), moe_p)
            f_lat = f_all[:b * n_lat]
            xc = _post_norm(xc, f_all[b * n_lat:].reshape(b, n_ctx, d), cg2, ln2_w[l], ln2_b[l])
        x = _post_norm(x, f_lat.reshape(b, n_lat, d), g2, ln2_w[l], ln2_b[l])
    return x
```

```python
import functools

import jax
import jax.numpy as jnp
import numpy as np
from jax import lax
from jax.experimental import pallas as pl
from jax.experimental.pallas import tpu as pltpu

F32 = jnp.float32
BF16 = jnp.bfloat16
HIGHEST = lax.Precision.HIGHEST

GRID_W = 64
MLSTM_H, MLSTM_DH = 4, 64
MLSTM_W = MLSTM_H * MLSTM_DH
RWKV_H, RWKV_N = 4, 64
RWKV_W = RWKV_H * RWKV_N
DECAY_LORA, AAA_LORA, GATE_LORA = 64, 64, 128
RWKV_GN_EPS = 64e-5
MLA_H, MLA_NOPE, MLA_ROPE, MLA_V = 8, 64, 32, 64
Q_RANK, KV_RANK = 256, 128
ROPE_THETA = 10000.0
N_MLSTM_IN = 4 * MLSTM_W + 4 * MLSTM_H
N_RWKV_IN = 3 * RWKV_W + 2 * DECAY_LORA + 2 * AAA_LORA + GATE_LORA
N_EXPERTS, TOP_K, N_GROUPS, TOPK_GROUPS = 64, 8, 8, 4
ROUTED_SCALE = 2.5
LN_EPS = 1e-6
NEG = -1e30

C_QK, C_R, C_V, C_O, C_G, C_A = 0, 512, 1664, 1920, 2176, 2304
N_SHIFT = 1664
N_A = 640
N_EXT = C_A + N_A
HALO = 8
VMEM_LIMIT = 56 * 1024 * 1024


def _cp(sem):
    return pltpu.CompilerParams(dimension_semantics=sem, vmem_limit_bytes=VMEM_LIMIT)


def _ln(x):
    mu = jnp.mean(x, axis=-1, keepdims=True)
    xc = x - mu
    return xc * lax.rsqrt(jnp.mean(xc * xc, axis=-1, keepdims=True) + LN_EPS)


def _sigmoid(x):
    return 1.0 / (1.0 + jnp.exp(-x))


def _silu(x):
    return x * _sigmoid(x)


def _log_sigmoid(x):
    return jnp.minimum(x, 0.0) - jnp.log(1.0 + jnp.exp(-jnp.abs(x)))


def _softplus(x):
    return jnp.maximum(x, 0.0) + jnp.log(1.0 + jnp.exp(-jnp.abs(x)))


def _nt_dot(a, b, **kw):
    return lax.dot_general(a, b, (((1,), (1,)), ((), ())), preferred_element_type=F32, **kw)


def _tn_dot(a, b, **kw):
    return lax.dot_general(a, b, (((0,), (0,)), ((), ())), preferred_element_type=F32, **kw)


def _dot(a, b, **kw):
    return jnp.dot(a, b, preferred_element_type=F32, **kw)


def _mod_kernel(c_ref, w_ref, b_ref, o_ref):
    o_ref[0] = _dot(_silu(c_ref[...]), w_ref[0], precision=HIGHEST) + b_ref[0]


def _modulation(cc, w_mod, b_mod):
    n_layers, d, n = w_mod.shape
    tn = 1536
    return pl.pallas_call(
        _mod_kernel,
        grid=(n_layers, n // tn),
        in_specs=[pl.BlockSpec((8, d), lambda l, j: (0, 0)),
                  pl.BlockSpec((1, d, tn), lambda l, j: (l, 0, j)),
                  pl.BlockSpec((1, 1, tn), lambda l, j: (l, 0, j))],
        out_specs=pl.BlockSpec((1, 8, tn), lambda l, j: (l, 0, j)),
        out_shape=jax.ShapeDtypeStruct((n_layers, 8, n), F32),
        compiler_params=_cp(("parallel", "parallel")),
        name="modulation",
    )(cc, w_mod, b_mod.reshape(n_layers, 1, n))


def _in_kernel(xp_ref, x_ref, xn_ref, mod_ref, w_ref, conv_ref, mu_ref, zm_ref, zr_ref, za_ref, *, tm, nt_lat, nt):
    i = pl.program_id(1)
    has_prev = jnp.logical_and(i != 0, i != nt_lat)
    has_next = jnp.logical_and(i != nt_lat - 1, i != nt - 1)
    xt = jnp.concatenate([xp_ref[0], x_ref[0], xn_ref[0]], axis=0)
    sh = mod_ref[0, 0, 0:1, :]
    sc = mod_ref[0, 0, 1:2, :]
    h = (_ln(xt) * (1.0 + sc) + sh).astype(BF16)
    z = _dot(h, w_ref[...])
    rows = lax.broadcasted_iota(jnp.int32, (tm + 2 * HALO, 1), 0)
    lo = jnp.where(has_prev, 0, HALO)
    hi = jnp.where(has_next, tm + 2 * HALO, tm + HALO)
    keep = jnp.logical_and(rows >= lo, rows < hi)
    zs = jnp.where(keep, z[:, :N_SHIFT], 0.0)
    zc = zs[HALO:HALO + tm]
    zprev = pltpu.roll(zs, 1, axis=0)[HALO:HALO + tm]
    znext = pltpu.roll(zs, tm + 2 * HALO - 1, axis=0)[HALO:HALO + tm]
    cw = conv_ref[...]
    qk = (cw[0:1] * zprev[:, :C_R] + cw[1:2] * zc[:, :C_R] + cw[2:3] * znext[:, :C_R])
    qk = _silu(qk)
    lane = lax.broadcasted_iota(jnp.int32, (1, C_R), 1)
    qk = qk * jnp.where(lane < MLSTM_W, MLSTM_DH ** -0.5, 1.0)
    zm_ref[0, :, 0:512] = qk
    zm_ref[0, :, 512:1152] = z[HALO:HALO + tm, C_V:C_A]
    zr = zc[:, C_R:]
    zr_ref[0] = zr + mu_ref[...] * (0.5 * (zprev[:, C_R:] + znext[:, C_R:]) - zr)
    za_ref[0] = z[HALO:HALO + tm, C_A:]


def _in_proj(xa, modv, w_ext, conv, mu, *, t_lat, tm):
    b, ta, d = xa.shape
    nt, nt_lat = ta // tm, t_lat // tm
    nh = tm // HALO
    last = ta // HALO - 1
    kern = functools.partial(_in_kernel, tm=tm, nt_lat=nt_lat, nt=nt)
    return pl.pallas_call(
        kern,
        grid=(b, nt),
        in_specs=[pl.BlockSpec((1, HALO, d), lambda bi, i: (bi, jnp.maximum(i * nh - 1, 0), 0)),
                  pl.BlockSpec((1, tm, d), lambda bi, i: (bi, i, 0)),
                  pl.BlockSpec((1, HALO, d), lambda bi, i: (bi, jnp.minimum((i + 1) * nh, last), 0)),
                  pl.BlockSpec((1, 1, 6, d), lambda bi, i: (bi, i // nt_lat, 0, 0)),
                  pl.BlockSpec((d, N_EXT), lambda bi, i: (0, 0)),
                  pl.BlockSpec((3, C_R), lambda bi, i: (0, 0)),
                  pl.BlockSpec((1, N_RWKV_IN), lambda bi, i: (0, 0))],
        out_specs=[pl.BlockSpec((1, tm, 1152), lambda bi, i: (bi, i, 0)),
                   pl.BlockSpec((1, tm, N_RWKV_IN), lambda bi, i: (bi, i, 0)),
                   pl.BlockSpec((1, tm, N_A), lambda bi, i: (bi, i, 0))],
        out_shape=[jax.ShapeDtypeStruct((b, ta, 1152), F32),
                   jax.ShapeDtypeStruct((b, ta, N_RWKV_IN), F32),
                   jax.ShapeDtypeStruct((b, ta, N_A), F32)],
        compiler_params=_cp(("parallel", "parallel")),
        name="in_proj",
    )(xa, xa, xa, modv, w_ext, conv, mu)


def _mlstm_kernel(zm_ref, gc_ref, gr_ref, bc_ref, br_ref, o_ref, ct_ref, n_ref, m_ref, *, cl):
    d = pl.program_id(1)

    @pl.when(pl.program_id(2) == 0)
    def _():
        ct_ref[...] = jnp.zeros_like(ct_ref)
        n_ref[...] = jnp.zeros_like(n_ref)
        m_ref[...] = jnp.zeros_like(m_ref)

    sgn = 1 - 2 * d
    gc = gc_ref[0, 0] + bc_ref[0]
    gr = gr_ref[0, 0] + br_ref[0]
    lf_c = _log_sigmoid(gc)
    lf_r = _log_sigmoid(gr)
    ii = lax.broadcasted_iota(jnp.int32, (cl, cl), 0)
    jj = lax.broadcasted_iota(jnp.int32, (cl, cl), 1)
    mask = (jj - ii) * sgn <= 0
    mask_t = (ii - jj) * sgn <= 0
    cum_c = _dot(mask.astype(F32), lf_c, precision=HIGHEST)
    cum_r = _dot(lf_r, mask_t.astype(F32), precision=HIGHEST)
    outs = []
    for h in range(MLSTM_H):
        sl = slice(h * MLSTM_DH, (h + 1) * MLSTM_DH)
        q = zm_ref[0, :, sl]
        k = zm_ref[0, :, MLSTM_W + h * MLSTM_DH:MLSTM_W + (h + 1) * MLSTM_DH]
        v = zm_ref[0, :, 2 * MLSTM_W + h * MLSTM_DH:2 * MLSTM_W + (h + 1) * MLSTM_DH]
        qb, kb = q.astype(BF16), k.astype(BF16)
        li_c, li_r = gc[:, h:h + 1], gr[h:h + 1, :]
        cu_c, cu_r = cum_c[:, 4 + h:5 + h], cum_r[4 + h:5 + h, :]
        b_end = jnp.sum(lf_r[4 + h:5 + h, :], axis=1, keepdims=True)
        m_prev = m_ref[h:h + 1, 0:1]
        ct = ct_ref[h]
        nv = n_ref[h:h + 1, 0:MLSTM_DH]
        d_log = jnp.where(mask, cu_c - cu_r + li_r, NEG)
        g_log = cu_c + m_prev
        m_row = jnp.maximum(g_log, jnp.max(d_log, axis=1, keepdims=True))
        w_intra = jnp.exp(d_log - m_row) * _nt_dot(qb, kb)
        e_inter = jnp.exp(g_log - m_row)
        num = _dot(w_intra.astype(BF16), v.astype(BF16)) + e_inter * _dot(qb, ct.astype(BF16))
        den = jnp.sum(w_intra, axis=1, keepdims=True) + e_inter * jnp.sum(q * nv, axis=1, keepdims=True)
        outs.append(num / jnp.maximum(jnp.abs(den), jnp.exp(-m_row)))
        w_end = b_end - cu_c + li_c
        m_loc = jnp.max(w_end, axis=0, keepdims=True)
        e_end = jnp.exp(w_end - m_loc)
        m_new = jnp.maximum(b_end + m_prev, m_loc)
        a_old = jnp.exp(b_end + m_prev - m_new)
        a_loc = jnp.exp(m_loc - m_new)
        ct_ref[h] = a_old * ct + a_loc * _tn_dot(kb, (v * e_end).astype(BF16))
        n_ref[h:h + 1, 0:MLSTM_DH] = a_old * nv + a_loc * jnp.sum(k * e_end, axis=0, keepdims=True)
        m_ref[h:h + 1, :] = jnp.broadcast_to(m_new, (1, 128))
    o_ref[0, 0] = jnp.concatenate(outs, axis=1)


def _mlstm(zm, g_col, g_row, b_col, b_row, *, t_lat, cl):
    b, ta, _ = zm.shape
    nc, nc_lat = ta // cl, t_lat // cl
    nc_ctx = nc - nc_lat

    def chunk(di, c):
        fwd_idx = jnp.where(c < nc_ctx, nc_lat + c, c - nc_ctx)
        return jnp.where(di == 0, fwd_idx, nc - 1 - c)

    return pl.pallas_call(
        functools.partial(_mlstm_kernel, cl=cl),
        grid=(b, 2, nc),
        in_specs=[pl.BlockSpec((1, cl, 1152), lambda bi, di, c: (bi, chunk(di, c), 0)),
                  pl.BlockSpec((1, 1, cl, 8), lambda bi, di, c: (bi, di, chunk(di, c), 0)),
                  pl.BlockSpec((1, 1, 8, cl), lambda bi, di, c: (bi, di, 0, chunk(di, c))),
                  pl.BlockSpec((1, 1, 8), lambda bi, di, c: (di, 0, 0)),
                  pl.BlockSpec((1, 8, 1), lambda bi, di, c: (di, 0, 0))],
        out_specs=pl.BlockSpec((1, 1, cl, MLSTM_W), lambda bi, di, c: (bi, di, chunk(di, c), 0)),
        out_shape=jax.ShapeDtypeStruct((b, 2, ta, MLSTM_W), F32),
        scratch_shapes=[pltpu.VMEM((MLSTM_H, MLSTM_DH, MLSTM_DH), F32),
                        pltpu.VMEM((8, 128), F32),
                        pltpu.VMEM((8, 128), F32)],
        compiler_params=_cp(("parallel", "parallel", "arbitrary")),
        name="mlstm",
    )(zm, g_col, g_row, b_col, b_row)


def _rwkv_prep_kernel(zr_ref, wup_ref, aup_ref, gup_ref, vec_ref, hsum_ref, a_ref, v_ref, g_ref, bonus_ref):
    z = zr_ref[0]
    zr_, zk, zv = z[:, 0:256], z[:, 256:512], z[:, 512:768]
    zw, za, zg = z[:, 768:896], z[:, 896:1024], z[:, 1024:1152]
    vec = vec_ref[...]
    hsum = hsum_ref[...]
    w_raw = _dot(jnp.tanh(zw).astype(BF16), wup_ref[...])
    a_raw = _dot(za.astype(BF16), aup_ref[...])
    g_ref[0] = _dot(_sigmoid(zg).astype(BF16), gup_ref[...])
    kk = zk * vec[4:5]
    kk_ss = _dot(kk * kk, hsum, precision=HIGHEST)
    kk = kk * lax.rsqrt(jnp.maximum(kk_ss, 1e-12))
    k_sum = jnp.zeros_like(zk)
    for di in range(2):
        sl = slice(di * RWKV_W, (di + 1) * RWKV_W)
        decay = jnp.exp(-jnp.exp(-_softplus(-(vec[di:di + 1] + w_raw[:, sl])) - 0.5))
        a = _sigmoid(vec[2 + di:3 + di] + a_raw[:, sl])
        k_dir = zk * (1.0 + (a - 1.0) * vec[5:6])
        k_sum = k_sum + k_dir
        a_ref[0, di, 0] = kk
        a_ref[0, di, 1] = decay
        a_ref[0, di, 2] = kk * a
        a_ref[0, di, 3] = k_dir
        a_ref[0, di, 4] = zr_
    v_ref[0] = zv
    bonus_ref[0] = _dot(zr_ * vec[6:7] * k_sum, hsum, precision=HIGHEST) * zv


def _rwkv_prep(zr, wup, aup, gup, vec, hsum, *, tm):
    b, ta, _ = zr.shape
    tok = pl.BlockSpec((1, tm, RWKV_W), lambda bi, i: (bi, i, 0))
    full = lambda a: pl.BlockSpec(a.shape, lambda bi, i: (0,) * a.ndim)
    return pl.pallas_call(
        _rwkv_prep_kernel,
        grid=(b, ta // tm),
        in_specs=[pl.BlockSpec((1, tm, N_RWKV_IN), lambda bi, i: (bi, i, 0)),
                  full(wup), full(aup), full(gup), full(vec), full(hsum)],
        out_specs=[pl.BlockSpec((1, 2, 5, tm, RWKV_W), lambda bi, i: (bi, 0, 0, i, 0)), tok, tok, tok],
        out_shape=[jax.ShapeDtypeStruct((b, 2, 5, ta, RWKV_W), F32)] + [jax.ShapeDtypeStruct((b, ta, RWKV_W), F32)] * 3,
        compiler_params=_cp(("parallel", "parallel")),
        name="rwkv_prep",
    )(zr, wup, aup, gup, vec, hsum)


def _rwkv_scan_kernel(x_ref, v_ref, o_ref, s_ref, *, ts):
    @pl.when(pl.program_id(0) == 0)
    def _():
        s_ref[...] = jnp.zeros_like(s_ref)

    def step(t, carry):
        vt = v_ref[t]
        outs = []
        for vh in range(RWKV_N // 4):
            sv = s_ref[vh]
            sa = jnp.sum(sv * x_ref[t, 0], axis=0, keepdims=True)
            sv = sv * x_ref[t, 1] - sa * x_ref[t, 2] + vt[vh:vh + 1, :] * x_ref[t, 3]
            s_ref[vh] = sv
            outs.append(jnp.sum(sv * x_ref[t, 4], axis=0, keepdims=True))
        o_ref[t] = jnp.concatenate(outs, axis=0)
        return carry

    lax.fori_loop(0, ts, step, 0)


def _rwkv_scan(xs, vs, *, ts):
    s, nl = xs.shape[0], xs.shape[3]
    return pl.pallas_call(
        functools.partial(_rwkv_scan_kernel, ts=ts),
        grid=(s // ts,),
        in_specs=[pl.BlockSpec((ts, 5, RWKV_N, nl), lambda i: (i, 0, 0, 0)),
                  pl.BlockSpec((ts, RWKV_N // 4, nl), lambda i: (i, 0, 0))],
        out_specs=pl.BlockSpec((ts, RWKV_N // 4, nl), lambda i: (i, 0, 0)),
        out_shape=jax.ShapeDtypeStruct((s, RWKV_N // 4, nl), F32),
        scratch_shapes=[pltpu.VMEM((RWKV_N // 4, RWKV_N, nl), F32)],
        compiler_params=_cp(("arbitrary",)),
        name="rwkv_scan",
    )(xs, vs)


def _mla_proj_kernel(za_ref, cos_ref, sin_ref, nq_ref, nkv_ref, wq_ref, wqr_ref, wk_ref, wv_ref, q_ref, k_ref, v_ref):
    za = za_ref[0]
    cq, ckv = za[:, 0:Q_RANK], za[:, Q_RANK:Q_RANK + KV_RANK]
    kr, krr = za[:, 384:512], za[:, 512:640]
    cos, sin = cos_ref[...], sin_ref[...]
    cqn = (cq * lax.rsqrt(jnp.mean(cq * cq, axis=-1, keepdims=True) + 1e-6) * nq_ref[...]).astype(BF16)
    ckvn = (ckv * lax.rsqrt(jnp.mean(ckv * ckv, axis=-1, keepdims=True) + 1e-6) * nkv_ref[...]).astype(BF16)
    q = _dot(cqn, wq_ref[...])
    qr = _dot(cqn, wqr_ref[...])
    kn = _dot(ckvn, wk_ref[...])
    v_ref[0] = _dot(ckvn, wv_ref[...]).astype(BF16)
    k_rope = kr * cos + krr * sin
    scale = (MLA_NOPE + MLA_ROPE) ** -0.5
    for h in range(MLA_H):
        sl = slice(h * 128, (h + 1) * 128)
        q_ref[0, :, sl] = ((q[:, sl] * cos + qr[:, sl] * sin) * scale).astype(BF16)
        k_ref[0, :, sl] = (kn[:, sl] + k_rope).astype(BF16)


def _mla_proj(za, cos, sin, nq, nkv, wq, wqr, wk, wv, *, tm):
    b, ta, _ = za.shape
    full = lambda a: pl.BlockSpec(a.shape, lambda bi, i: (0,) * a.ndim)
    return pl.pallas_call(
        _mla_proj_kernel,
        grid=(b, ta // tm),
        in_specs=[pl.BlockSpec((1, tm, N_A), lambda bi, i: (bi, i, 0)),
                  pl.BlockSpec((tm, 128), lambda bi, i: (i, 0)),
                  pl.BlockSpec((tm, 128), lambda bi, i: (i, 0)),
                  full(nq), full(nkv), full(wq), full(wqr), full(wk), full(wv)],
        out_specs=[pl.BlockSpec((1, tm, 1024), lambda bi, i: (bi, i, 0)),
                   pl.BlockSpec((1, tm, 1024), lambda bi, i: (bi, i, 0)),
                   pl.BlockSpec((1, tm, 512), lambda bi, i: (bi, i, 0))],
        out_shape=[jax.ShapeDtypeStruct((b, ta, 1024), BF16),
                   jax.ShapeDtypeStruct((b, ta, 1024), BF16),
                   jax.ShapeDtypeStruct((b, ta, 512), BF16)],
        compiler_params=_cp(("parallel", "parallel")),
        name="mla_proj",
    )(za, cos, sin, nq, nkv, wq, wqr, wk, wv)


def _flash_kernel(q_ref, k_ref, v_ref, o_ref, acc_ref, m_ref, l_ref, *, tk, n_keys):
    tq = q_ref.shape[1]
    chunks = [(c * tk, tk) for c in range(n_keys // tk)]
    if n_keys % tk:
        chunks.append((n_keys - n_keys % tk, n_keys % tk))
    res = []
    for hh in range(2):
        q = q_ref[0, :, hh * 128:(hh + 1) * 128]
        m_ref[...] = jnp.full(m_ref.shape, NEG, F32)
        l_ref[...] = jnp.zeros_like(l_ref)
        acc_ref[...] = jnp.zeros_like(acc_ref)
        for start, size in chunks:
            k = k_ref[0, start:start + size, hh * 128:(hh + 1) * 128]
            s = _nt_dot(q, k)
            m_old = m_ref[...]
            m_new = jnp.maximum(m_old, jnp.max(s, axis=1, keepdims=True))
            p = jnp.exp(s - m_new)
            alpha = jnp.exp(m_old - m_new)
            l_ref[...] = alpha * l_ref[...] + jnp.sum(p, axis=1, keepdims=True)
            acc_ref[...] = alpha * acc_ref[...] + _dot(p.astype(BF16), v_ref[0, start:start + size, :])
            m_ref[...] = m_new
        res.append(acc_ref[...] / l_ref[...])
    lane = lax.broadcasted_iota(jnp.int32, (tq, 128), 1)
    o_ref[0] = jnp.where(lane < MLA_V, res[0], res[1])


def _flash(q, k, v, *, q_off, n_q, k_off, n_keys, tq, tk):
    b = q.shape[0]
    nq = n_q // tq
    qo, ko = q_off // tq, k_off // n_keys
    return pl.pallas_call(
        functools.partial(_flash_kernel, tk=tk, n_keys=n_keys),
        grid=(b, MLA_H // 2, nq),
        in_specs=[pl.BlockSpec((1, tq, 256), lambda bi, hp, i: (bi, qo + i, hp)),
                  pl.BlockSpec((1, n_keys, 256), lambda bi, hp, i: (bi, ko, hp)),
                  pl.BlockSpec((1, n_keys, 128), lambda bi, hp, i: (bi, ko, hp))],
        out_specs=pl.BlockSpec((1, tq, 128), lambda bi, hp, i: (bi, i, hp)),
        out_shape=jax.ShapeDtypeStruct((b, n_q, 512), F32),
        scratch_shapes=[pltpu.VMEM((tq, 128), F32), pltpu.VMEM((tq, 1), F32), pltpu.VMEM((tq, 1), F32)],
        compiler_params=_cp(("parallel", "parallel", "parallel")),
        name="mla_attention",
    )(q, k, v)


def _out_kernel(x_ref, hf_ref, hb_ref, zm_ref, of_ref, ob_ref, g_ref, bonus_ref, att_ref, mod_ref, vec_ref,
                hmean_ref, wo_ref, ln_ref, o_ref, *, alpha):
    hmean = hmean_ref[...]

    def head_norm(y, eps):
        mu = _dot(y, hmean, precision=HIGHEST)
        yc = y - mu
        return yc * lax.rsqrt(_dot(yc * yc, hmean, precision=HIGHEST) + eps)

    vec = vec_ref[...]
    zo = zm_ref[0, :, 768:1024]
    m_mix = _sigmoid(zo) * (head_norm(hf_ref[0, 0] + hb_ref[0, 0], 1e-6) * vec[0:1])
    r_o = head_norm(of_ref[0] + ob_ref[0], RWKV_GN_EPS) * vec[1:2] + vec[2:3] + bonus_ref[0]
    r_mix = r_o * g_ref[0]
    mix = jnp.concatenate([m_mix, r_mix, att_ref[0]], axis=1).astype(BF16)
    y = _dot(mix, wo_ref[...])
    gate = mod_ref[0, 0, 2:3, :]
    o_ref[0] = _ln(alpha * x_ref[0] + gate * y) * ln_ref[0:1] + ln_ref[1:2]


def _out_proj(xa, h_m, zm, o_f, o_b, g, bonus, att, modv, vec, hmean, wo, ln, *, t_lat, tm, alpha):
    b, ta, d = xa.shape
    nt_lat = t_lat // tm
    tok = lambda w: pl.BlockSpec((1, tm, w), lambda bi, i: (bi, i, 0))
    full = lambda a: pl.BlockSpec(a.shape, lambda bi, i: (0,) * a.ndim)
    return pl.pallas_call(
        functools.partial(_out_kernel, alpha=alpha),
        grid=(b, ta // tm),
        in_specs=[tok(d),
                  pl.BlockSpec((1, 1, tm, MLSTM_W), lambda bi, i: (bi, 0, i, 0)),
                  pl.BlockSpec((1, 1, tm, MLSTM_W), lambda bi, i: (bi, 1, i, 0)),
                  tok(1152), tok(RWKV_W), tok(RWKV_W), tok(RWKV_W), tok(RWKV_W), tok(512),
                  pl.BlockSpec((1, 1, 6, d), lambda bi, i: (bi, i // nt_lat, 0, 0)),
                  full(vec), full(hmean), full(wo), full(ln)],
        out_specs=tok(d),
        out_shape=jax.ShapeDtypeStruct((b, ta, d), F32),
        compiler_params=_cp(("parallel", "parallel")),
        name="out_proj",
    )(xa, h_m, h_m, zm, o_f, o_b, g, bonus, att, modv, vec, hmean, wo, ln)


def _first_max(x, idx, axis):
    mx = jnp.max(x, axis=axis, keepdims=True)
    first = jnp.min(jnp.where(x == mx, idx, 1 << 20), axis=axis, keepdims=True)
    return mx, idx == first


def _router_kernel(x_ref, mod_ref, w_ref, bias_ref, o_ref):
    tm = x_ref.shape[1]
    sh, sc = mod_ref[0, 0, 3:4, :], mod_ref[0, 0, 4:5, :]
    h = _ln(x_ref[0]) * (1.0 + sc) + sh
    scores = _sigmoid(_nt_dot(w_ref[...], h, precision=HIGHEST))
    biased = scores + bias_ref[...]
    gsz = N_EXPERTS // N_GROUPS
    b3 = biased.reshape(N_GROUPS, gsz, tm)
    s3 = scores.reshape(N_GROUPS, gsz, tm)
    e_idx = lax.broadcasted_iota(jnp.int32, (N_GROUPS, gsz, tm), 1)
    m1, hit = _first_max(b3, e_idx, 1)
    m2 = jnp.max(jnp.where(hit, -jnp.inf, b3), axis=1, keepdims=True)
    gscore = m1 + m2
    g_idx = lax.broadcasted_iota(jnp.int32, (N_GROUPS, 1, tm), 0)
    gsel = jnp.zeros((N_GROUPS, 1, tm), F32)
    for _ in range(TOPK_GROUPS):
        _, hit = _first_max(jnp.where(gsel > 0, -jnp.inf, gscore), g_idx, 0)
        gsel = jnp.where(hit, 1.0, gsel)
    cand = jnp.where(jnp.broadcast_to(gsel, b3.shape) > 0, b3, -jnp.inf)
    x_idx = g_idx * gsz + e_idx
    sel = jnp.zeros(b3.shape, F32)
    for _ in range(TOP_K):
        _, hit = _first_max(jnp.where(sel > 0, -jnp.inf, cand), x_idx, (0, 1))
        sel = jnp.where(hit, 1.0, sel)
    picked = s3 * sel
    gates = ROUTED_SCALE * picked / jnp.sum(picked, axis=(0, 1), keepdims=True)
    o_ref[0] = gates.reshape(N_EXPERTS, tm)


def _router(xa, modv, w_t, bias, *, t_lat, tm):
    b, ta, d = xa.shape
    nt_lat = t_lat // tm
    return pl.pallas_call(
        _router_kernel,
        grid=(b, ta // tm),
        in_specs=[pl.BlockSpec((1, tm, d), lambda bi, i: (bi, i, 0)),
                  pl.BlockSpec((1, 1, 6, d), lambda bi, i: (bi, i // nt_lat, 0, 0)),
                  pl.BlockSpec((N_EXPERTS, d), lambda bi, i: (0, 0)),
                  pl.BlockSpec((N_EXPERTS, 1), lambda bi, i: (0, 0))],
        out_specs=pl.BlockSpec((1, N_EXPERTS, tm), lambda bi, i: (bi, 0, i)),
        out_shape=jax.ShapeDtypeStruct((b, N_EXPERTS, ta), F32),
        compiler_params=_cp(("parallel", "parallel")),
        name="router",
    )(xa, modv, w_t, bias)


def _moe_kernel(x_ref, gates_ref, mod_ref, wg_ref, wu_ref, wd_ref, sg_ref, su_ref, sd_ref, ln_ref, o_ref,
                h_ref, acc_ref, *, t_lat, alpha):
    e = pl.program_id(2)
    tm = x_ref.shape[1]
    row = pl.program_id(1) * tm + lax.broadcasted_iota(jnp.int32, (tm, 1), 0)
    is_ctx = row >= t_lat

    def mod(j):
        return jnp.where(is_ctx, mod_ref[0, 1, j:j + 1, :], mod_ref[0, 0, j:j + 1, :])

    def ffn(h, wg, wu, wd, gate):
        a = _silu(_dot(h, wg)) * _dot(h, wu)
        if gate is not None:
            a = a * gate
        return _dot(a.astype(BF16), wd)

    @pl.when(e == 0)
    def _():
        h = (_ln(x_ref[0]) * (1.0 + mod(4)) + mod(3)).astype(BF16)
        h_ref[...] = h
        acc_ref[...] = ffn(h, sg_ref[...], su_ref[...], sd_ref[...], None)

    lane = lax.broadcasted_iota(jnp.int32, (1, N_EXPERTS), 1)
    gate = jnp.sum(jnp.where(lane == e, gates_ref[0], 0.0), axis=1, keepdims=True)
    acc_ref[...] += ffn(h_ref[...], wg_ref[0], wu_ref[0], wd_ref[0], gate)

    @pl.when(e == N_EXPERTS - 1)
    def _():
        o_ref[0] = _ln(alpha * x_ref[0] + mod(5) * acc_ref[...]) * ln_ref[0:1] + ln_ref[1:2]


def _moe(xa, gates, modv, wg, wu, wd, sg, su, sd, ln, *, t_lat, tm, alpha):
    b, ta, d = xa.shape
    de = wg.shape[2]
    full = lambda a: pl.BlockSpec(a.shape, lambda bi, i, e: (0,) * a.ndim)
    return pl.pallas_call(
        functools.partial(_moe_kernel, t_lat=t_lat, alpha=alpha),
        grid=(b, ta // tm, N_EXPERTS),
        in_specs=[pl.BlockSpec((1, tm, d), lambda bi, i, e: (bi, i, 0)),
                  pl.BlockSpec((1, tm, N_EXPERTS), lambda bi, i, e: (bi, i, 0)),
                  pl.BlockSpec((1, 2, 6, d), lambda bi, i, e: (bi, 0, 0, 0)),
                  pl.BlockSpec((1, d, de), lambda bi, i, e: (e, 0, 0)),
                  pl.BlockSpec((1, d, de), lambda bi, i, e: (e, 0, 0)),
                  pl.BlockSpec((1, de, d), lambda bi, i, e: (e, 0, 0)),
                  full(sg), full(su), full(sd), full(ln)],
        out_specs=pl.BlockSpec((1, tm, d), lambda bi, i, e: (bi, i, 0)),
        out_shape=jax.ShapeDtypeStruct((b, ta, d), F32),
        scratch_shapes=[pltpu.VMEM((tm, d), BF16), pltpu.VMEM((tm, d), F32)],
        compiler_params=_cp(("parallel", "parallel", "arbitrary")),
        name="moe",
    )(xa, gates, modv, wg, wu, wd, sg, su, sd, ln)


def _rope_blocks(w_rope):
    k = w_rope.shape[0]
    ev, od = w_rope[:, 0::2], w_rope[:, 1::2]
    z64, z32 = jnp.zeros((k, 64), w_rope.dtype), jnp.zeros((k, 32), w_rope.dtype)
    return (jnp.concatenate([z64, ev, od, z32], 1), jnp.concatenate([z64, -od, ev, z32], 1))


def _layer_weights(l, w_in, mla_w_uq, mla_w_ukv):
    w = w_in[l]
    d = w.shape[0]
    wm, wr, wa = w[:, :N_MLSTM_IN], w[:, N_MLSTM_IN:N_MLSTM_IN + N_RWKV_IN], w[:, N_MLSTM_IN + N_RWKV_IN:]
    kr, krr = _rope_blocks(wa[:, Q_RANK + KV_RANK:])
    w_ext = jnp.concatenate([
        wm[:, :2 * MLSTM_W], wr, wm[:, 2 * MLSTM_W:4 * MLSTM_W],
        wm[:, 4 * MLSTM_W:], jnp.zeros((d, 128 - 4 * MLSTM_H), w.dtype),
        wa[:, :Q_RANK + KV_RANK], kr, krr], axis=1).astype(BF16)
    uq = mla_w_uq[l].reshape(Q_RANK, MLA_H, MLA_NOPE + MLA_ROPE)
    wq, wqr = [], []
    for h in range(MLA_H):
        r0, r1 = _rope_blocks(uq[:, h, MLA_NOPE:])
        wq.append(r0.at[:, :MLA_NOPE].set(uq[:, h, :MLA_NOPE]))
        wqr.append(r1)
    ukv = mla_w_ukv[l].reshape(KV_RANK, MLA_H, MLA_NOPE + MLA_V)
    wk = jnp.concatenate([ukv[:, :, :MLA_NOPE], jnp.zeros((KV_RANK, MLA_H, 64), F32)], axis=2).reshape(KV_RANK, MLA_H * 128)
    wv = ukv[:, :, MLA_NOPE:].reshape(KV_RANK, MLA_H * MLA_V)
    return w_ext, jnp.concatenate(wq, 1).astype(BF16), jnp.concatenate(wqr, 1).astype(BF16), wk.astype(BF16), wv.astype(BF16)


def _rope_tables(t_lat, n_ctx):
    rows = t_lat // GRID_W
    row = jnp.repeat(jnp.arange(rows), GRID_W).astype(F32)
    col = jnp.tile(jnp.arange(GRID_W), rows).astype(F32)
    n_freq = MLA_ROPE // 4
    freq = ROPE_THETA ** (-jnp.arange(n_freq, dtype=F32) / n_freq)
    ang = jnp.concatenate([row[:, None] * freq, col[:, None] * freq], -1)
    cos, sin = jnp.cos(ang), jnp.sin(ang)
    one, zero = jnp.ones((t_lat, 64), F32), jnp.zeros((t_lat, 32), F32)
    cos_l = jnp.concatenate([one, cos, cos, zero], 1)
    sin_l = jnp.concatenate([0 * one, sin, sin, zero], 1)
    cos_c = jnp.concatenate([jnp.ones((n_ctx, 96), F32), jnp.zeros((n_ctx, 32), F32)], 1)
    return jnp.concatenate([cos_l, cos_c], 0), jnp.concatenate([sin_l, jnp.zeros((n_ctx, 128), F32)], 0)


def _scan_order(a, t_lat, axis):
    lat = lax.slice_in_dim(a, 0, t_lat, axis=axis)
    ctx = lax.slice_in_dim(a, t_lat, a.shape[axis], axis=axis)
    fwd = jnp.concatenate([ctx, lat], axis=axis)
    bwd = jnp.concatenate([jnp.flip(ctx, axis), jnp.flip(lat, axis)], axis=axis)
    return fwd, bwd


def _token_order(o, t_lat, n_ctx, backward):
    ctx, lat = o[:, :n_ctx], o[:, n_ctx:]
    if backward:
        ctx, lat = jnp.flip(ctx, 1), jnp.flip(lat, 1)
    return jnp.concatenate([lat, ctx], axis=1)


def kernel(x, c, ctx, c_ctx, w_mod, b_mod, w_in, mlstm_conv, mlstm_gate_bias, mlstm_norm_w, rwkv_mu, rwkv_w0, rwkv_w_up, rwkv_a0, rwkv_a_up, rwkv_g_up, rwkv_k_k, rwkv_k_a, rwkv_r_k, rwkv_ln_w, rwkv_ln_b, mla_q_norm, mla_kv_norm, mla_w_uq, mla_w_ukv, w_out, ln1_w, ln1_b, router_w, router_bias, exp_w_gate, exp_w_up, exp_w_down, sh_w_gate, sh_w_up, sh_w_down, ln2_w, ln2_b):
    b, t_lat, d = x.shape
    n_ctx = ctx.shape[1]
    ta = t_lat + n_ctx
    depth = w_in.shape[0]
    alpha = (2 * depth) ** 0.25
    tm = 256
    cl = 128
    tm_moe = ta // 4
    assert b + 1 <= 8 and n_ctx % tm == 0 and t_lat % tm == 0 and tm_moe % 8 == 0

    cc = jnp.concatenate([c, c_ctx[None], jnp.zeros((8 - b - 1, d), F32)], 0)
    mods = _modulation(cc, w_mod, b_mod)
    cos, sin = _rope_tables(t_lat, n_ctx)
    hsum = jnp.kron(jnp.eye(RWKV_H, dtype=F32), jnp.ones((RWKV_N, RWKV_N), F32))
    hmean = hsum / RWKV_N

    xa = jnp.concatenate([x, ctx], axis=1)
    for l in range(depth):
        m_lat = mods[l, :b].reshape(b, 1, 6, d)
        m_ctx = jnp.broadcast_to(mods[l, b].reshape(1, 1, 6, d), (b, 1, 6, d))
        modv = jnp.concatenate([m_lat, m_ctx], axis=1)
        w_ext, wq, wqr, wk, wv = _layer_weights(l, w_in, mla_w_uq, mla_w_ukv)
        mu = rwkv_mu[l].reshape(1, N_RWKV_IN)
        zm, zr, za = _in_proj(xa, modv, w_ext, mlstm_conv[l], mu, t_lat=t_lat, tm=tm)

        gates = zm[:, :, 1024:1024 + 4 * MLSTM_H].reshape(b, ta, 2, 8)
        g_col = gates.transpose(0, 2, 1, 3)
        g_row = gates.transpose(0, 2, 3, 1)
        gb = mlstm_gate_bias[l].reshape(2, 8)
        h_m = _mlstm(zm, g_col, g_row, gb.reshape(2, 1, 8), gb.reshape(2, 8, 1), t_lat=t_lat, cl=cl)

        zeros = jnp.zeros((DECAY_LORA, RWKV_W), F32)
        wup = jnp.concatenate([jnp.concatenate([rwkv_w_up[l, 0], zeros], 1),
                               jnp.concatenate([zeros, rwkv_w_up[l, 1]], 1)], 0).astype(BF16)
        aup = jnp.concatenate([jnp.concatenate([rwkv_a_up[l, 0], zeros], 1),
                               jnp.concatenate([zeros, rwkv_a_up[l, 1]], 1)], 0).astype(BF16)
        vec = jnp.concatenate([rwkv_w0[l], rwkv_a0[l], rwkv_k_k[l][None], rwkv_k_a[l][None], rwkv_r_k[l][None],
                               jnp.zeros((1, RWKV_W), F32)], 0)
        a_all, v_tok, g_tok, bonus = _rwkv_prep(zr, wup, aup, rwkv_g_up[l].astype(BF16), vec, hsum, tm=tm)
        a_f, _ = _scan_order(a_all[:, 0], t_lat, 2)
        _, a_b = _scan_order(a_all[:, 1], t_lat, 2)
        a_s = jnp.stack([a_f, a_b], 0).reshape(2, b, 5, ta, RWKV_H, RWKV_N)
        xs = jnp.broadcast_to(a_s.transpose(3, 2, 5, 0, 1, 4)[..., None], (ta, 5, RWKV_N, 2, b, RWKV_H, 4))
        xs = xs.reshape(ta, 5, RWKV_N, 2 * b * RWKV_H * 4)
        v_s = jnp.stack(_scan_order(v_tok, t_lat, 1), 0).reshape(2, b, ta, RWKV_H, RWKV_N // 4, 4)
        vs = v_s.transpose(2, 4, 0, 1, 3, 5).reshape(ta, RWKV_N // 4, 2 * b * RWKV_H * 4)
        o_s = _rwkv_scan(xs, vs, ts=16)
        o_s = o_s.reshape(ta, RWKV_N // 4, 2, b, RWKV_H, 4).transpose(2, 3, 0, 4, 1, 5).reshape(2, b, ta, RWKV_W)
        o_f = _token_order(o_s[0], t_lat, n_ctx, False)
        o_b = _token_order(o_s[1], t_lat, n_ctx, True)

        q, k, v = _mla_proj(za, cos, sin, mla_q_norm[l][None], mla_kv_norm[l][None], wq, wqr, wk, wv, tm=tm)
        att_l = _flash(q, k, v, q_off=0, n_q=t_lat, k_off=0, n_keys=ta, tq=min(512, t_lat), tk=512)
        att_c = _flash(q, k, v, q_off=t_lat, n_q=n_ctx, k_off=t_lat, n_keys=n_ctx, tq=n_ctx, tk=512)
        att = jnp.concatenate([att_l, att_c], axis=1)

        vec_o = jnp.concatenate([mlstm_norm_w[l][None], rwkv_ln_w[l][None], rwkv_ln_b[l][None],
                                 jnp.zeros((5, RWKV_W), F32)], 0)
        ln1 = jnp.stack([ln1_w[l], ln1_b[l]], 0)
        xa = _out_proj(xa, h_m, zm, o_f, o_b, g_tok, bonus, att, modv, vec_o, hmean, w_out[l].astype(BF16), ln1,
                       t_lat=t_lat, tm=tm, alpha=alpha)

        gates_t = _router(xa, modv, router_w[l].T, router_bias[l].reshape(N_EXPERTS, 1), t_lat=t_lat, tm=tm)
        ln2 = jnp.stack([ln2_w[l], ln2_b[l]], 0)
        xa = _moe(xa, gates_t.transpose(0, 2, 1), modv,
                  exp_w_gate[l].astype(BF16), exp_w_up[l].astype(BF16), exp_w_down[l].astype(BF16),
                  sh_w_gate[l].astype(BF16), sh_w_up[l].astype(BF16), sh_w_down[l].astype(BF16), ln2,
                  t_lat=t_lat, tm=tm_moe, alpha=alpha)
    return xa[:, :t_lat]
```

```python
import functools

import jax
import jax.numpy as jnp
import numpy as np
from jax import lax
from jax.experimental import pallas as pl
from jax.experimental.pallas import tpu as pltpu

F32 = jnp.float32
BF16 = jnp.bfloat16
HIGHEST = lax.Precision.HIGHEST

GRID_W = 64
MLSTM_H, MLSTM_DH = 4, 64
MLSTM_W = MLSTM_H * MLSTM_DH
RWKV_H, RWKV_N = 4, 64
RWKV_W = RWKV_H * RWKV_N
DECAY_LORA, AAA_LORA, GATE_LORA = 64, 64, 128
RWKV_GN_EPS = 64e-5
MLA_H, MLA_NOPE, MLA_ROPE, MLA_V = 8, 64, 32, 64
Q_RANK, KV_RANK = 256, 128
ROPE_THETA = 10000.0
N_MLSTM_IN = 4 * MLSTM_W + 4 * MLSTM_H
N_RWKV_IN = 3 * RWKV_W + 2 * DECAY_LORA + 2 * AAA_LORA + GATE_LORA
N_EXPERTS, TOP_K, N_GROUPS, TOPK_GROUPS = 64, 8, 8, 4
ROUTED_SCALE = 2.5
LN_EPS = 1e-6
NEG = -1e30

C_QK, C_R, C_V, C_O, C_G, C_A = 0, 512, 1664, 1920, 2176, 2304
N_SHIFT = 1664
N_A = 640
N_EXT = C_A + N_A
HALO = 8
VMEM_LIMIT = 56 * 1024 * 1024


def _cp(sem):
    return pltpu.CompilerParams(dimension_semantics=sem, vmem_limit_bytes=VMEM_LIMIT)


def _ln(x):
    mu = jnp.mean(x, axis=-1, keepdims=True)
    xc = x - mu
    return xc * lax.rsqrt(jnp.mean(xc * xc, axis=-1, keepdims=True) + LN_EPS)


def _sigmoid(x):
    return 1.0 / (1.0 + jnp.exp(-x))


def _silu(x):
    return x * _sigmoid(x)


def _log_sigmoid(x):
    return jnp.minimum(x, 0.0) - jnp.log(1.0 + jnp.exp(-jnp.abs(x)))


def _softplus(x):
    return jnp.maximum(x, 0.0) + jnp.log(1.0 + jnp.exp(-jnp.abs(x)))


def _nt_dot(a, b, **kw):
    return lax.dot_general(a, b, (((1,), (1,)), ((), ())), preferred_element_type=F32, **kw)


def _tn_dot(a, b, **kw):
    return lax.dot_general(a, b, (((0,), (0,)), ((), ())), preferred_element_type=F32, **kw)


def _dot(a, b, **kw):
    return jnp.dot(a, b, preferred_element_type=F32, **kw)


def _mod_kernel(c_ref, w_ref, b_ref, o_ref):
    o_ref[0] = _dot(_silu(c_ref[...]), w_ref[0], precision=HIGHEST) + b_ref[0]


def _modulation(cc, w_mod, b_mod):
    n_layers, d, n = w_mod.shape
    tn = 1536
    return pl.pallas_call(
        _mod_kernel,
        grid=(n_layers, n // tn),
        in_specs=[pl.BlockSpec((8, d), lambda l, j: (0, 0)),
                  pl.BlockSpec((1, d, tn), lambda l, j: (l, 0, j)),
                  pl.BlockSpec((1, 1, tn), lambda l, j: (l, 0, j))],
        out_specs=pl.BlockSpec((1, 8, tn), lambda l, j: (l, 0, j)),
        out_shape=jax.ShapeDtypeStruct((n_layers, 8, n), F32),
        compiler_params=_cp(("parallel", "parallel")),
        name="modulation",
    )(cc, w_mod, b_mod.reshape(n_layers, 1, n))


def _in_kernel(xp_ref, x_ref, xn_ref, mod_ref, w_ref, conv_ref, mu_ref, zm_ref, zr_ref, za_ref, *, tm, nt_lat, nt):
    i = pl.program_id(1)
    has_prev = jnp.logical_and(i != 0, i != nt_lat)
    has_next = jnp.logical_and(i != nt_lat - 1, i != nt - 1)
    xt = jnp.concatenate([xp_ref[0], x_ref[0], xn_ref[0]], axis=0)
    sh = mod_ref[0, 0, 0:1, :]
    sc = mod_ref[0, 0, 1:2, :]
    h = (_ln(xt) * (1.0 + sc) + sh).astype(BF16)
    z = _dot(h, w_ref[...])
    rows = lax.broadcasted_iota(jnp.int32, (tm + 2 * HALO, 1), 0)
    lo = jnp.where(has_prev, 0, HALO)
    hi = jnp.where(has_next, tm + 2 * HALO, tm + HALO)
    keep = jnp.logical_and(rows >= lo, rows < hi)
    zs = jnp.where(keep, z[:, :N_SHIFT], 0.0)
    zc = zs[HALO:HALO + tm]
    zprev = pltpu.roll(zs, 1, axis=0)[HALO:HALO + tm]
    znext = pltpu.roll(zs, tm + 2 * HALO - 1, axis=0)[HALO:HALO + tm]
    cw = conv_ref[...]
    qk = (cw[0:1] * zprev[:, :C_R] + cw[1:2] * zc[:, :C_R] + cw[2:3] * znext[:, :C_R])
    qk = _silu(qk)
    lane = lax.broadcasted_iota(jnp.int32, (1, C_R), 1)
    qk = qk * jnp.where(lane < MLSTM_W, MLSTM_DH ** -0.5, 1.0)
    zm_ref[0, :, 0:512] = qk
    zm_ref[0, :, 512:1152] = z[HALO:HALO + tm, C_V:C_A]
    zr = zc[:, C_R:]
    zr_ref[0] = zr + mu_ref[...] * (0.5 * (zprev[:, C_R:] + znext[:, C_R:]) - zr)
    za_ref[0] = z[HALO:HALO + tm, C_A:]


def _in_proj(xa, modv, w_ext, conv, mu, *, t_lat, tm):
    b, ta, d = xa.shape
    nt, nt_lat = ta // tm, t_lat // tm
    nh = tm // HALO
    last = ta // HALO - 1
    kern = functools.partial(_in_kernel, tm=tm, nt_lat=nt_lat, nt=nt)
    return pl.pallas_call(
        kern,
        grid=(b, nt),
        in_specs=[pl.BlockSpec((1, HALO, d), lambda bi, i: (bi, jnp.maximum(i * nh - 1, 0), 0)),
                  pl.BlockSpec((1, tm, d), lambda bi, i: (bi, i, 0)),
                  pl.BlockSpec((1, HALO, d), lambda bi, i: (bi, jnp.minimum((i + 1) * nh, last), 0)),
                  pl.BlockSpec((1, 1, 6, d), lambda bi, i: (bi, i // nt_lat, 0, 0)),
                  pl.BlockSpec((d, N_EXT), lambda bi, i: (0, 0)),
                  pl.BlockSpec((3, C_R), lambda bi, i: (0, 0)),
                  pl.BlockSpec((1, N_RWKV_IN), lambda bi, i: (0, 0))],
        out_specs=[pl.BlockSpec((1, tm, 1152), lambda bi, i: (bi, i, 0)),
                   pl.BlockSpec((1, tm, N_RWKV_IN), lambda bi, i: (bi, i, 0)),
                   pl.BlockSpec((1, tm, N_A), lambda bi, i: (bi, i, 0))],
        out_shape=[jax.ShapeDtypeStruct((b, ta, 1152), F32),
                   jax.ShapeDtypeStruct((b, ta, N_RWKV_IN), F32),
                   jax.ShapeDtypeStruct((b, ta, N_A), F32)],
        compiler_params=_cp(("parallel", "parallel")),
        name="in_proj",
    )(xa, xa, xa, modv, w_ext, conv, mu)


def _mlstm_kernel(zm_ref, gc_ref, gr_ref, bc_ref, br_ref, o_ref, ct_ref, n_ref, m_ref, *, cl):
    d = pl.program_id(1)

    @pl.when(pl.program_id(2) == 0)
    def _():
        ct_ref[...] = jnp.zeros_like(ct_ref)
        n_ref[...] = jnp.zeros_like(n_ref)
        m_ref[...] = jnp.zeros_like(m_ref)

    sgn = 1 - 2 * d
    gc = gc_ref[0, 0] + bc_ref[0]
    gr = gr_ref[0, 0] + br_ref[0]
    lf_c = _log_sigmoid(gc)
    lf_r = _log_sigmoid(gr)
    ii = lax.broadcasted_iota(jnp.int32, (cl, cl), 0)
    jj = lax.broadcasted_iota(jnp.int32, (cl, cl), 1)
    mask = (jj - ii) * sgn <= 0
    mask_t = (ii - jj) * sgn <= 0
    cum_c = _dot(mask.astype(F32), lf_c, precision=HIGHEST)
    cum_r = _dot(lf_r, mask_t.astype(F32), precision=HIGHEST)
    outs = []
    for h in range(MLSTM_H):
        sl = slice(h * MLSTM_DH, (h + 1) * MLSTM_DH)
        q = zm_ref[0, :, sl]
        k = zm_ref[0, :, MLSTM_W + h * MLSTM_DH:MLSTM_W + (h + 1) * MLSTM_DH]
        v = zm_ref[0, :, 2 * MLSTM_W + h * MLSTM_DH:2 * MLSTM_W + (h + 1) * MLSTM_DH]
        qb, kb = q.astype(BF16), k.astype(BF16)
        li_c, li_r = gc[:, h:h + 1], gr[h:h + 1, :]
        cu_c, cu_r = cum_c[:, 4 + h:5 + h], cum_r[4 + h:5 + h, :]
        b_end = jnp.sum(lf_r[4 + h:5 + h, :], axis=1, keepdims=True)
        m_prev = m_ref[h:h + 1, 0:1]
        ct = ct_ref[h]
        nv = n_ref[h:h + 1, 0:MLSTM_DH]
        d_log = jnp.where(mask, cu_c - cu_r + li_r, NEG)
        g_log = cu_c + m_prev
        m_row = jnp.maximum(g_log, jnp.max(d_log, axis=1, keepdims=True))
        w_intra = jnp.exp(d_log - m_row) * _nt_dot(qb, kb)
        e_inter = jnp.exp(g_log - m_row)
        num = _dot(w_intra.astype(BF16), v.astype(BF16)) + e_inter * _dot(qb, ct.astype(BF16))
        den = jnp.sum(w_intra, axis=1, keepdims=True) + e_inter * jnp.sum(q * nv, axis=1, keepdims=True)
        outs.append(num / jnp.maximum(jnp.abs(den), jnp.exp(-m_row)))
        w_end = b_end - cu_c + li_c
        m_loc = jnp.max(w_end, axis=0, keepdims=True)
        e_end = jnp.exp(w_end - m_loc)
        m_new = jnp.maximum(b_end + m_prev, m_loc)
        a_old = jnp.exp(b_end + m_prev - m_new)
        a_loc = jnp.exp(m_loc - m_new)
        ct_ref[h] = a_old * ct + a_loc * _tn_dot(kb, (v * e_end).astype(BF16))
        n_ref[h:h + 1, 0:MLSTM_DH] = a_old * nv + a_loc * jnp.sum(k * e_end, axis=0, keepdims=True)
        m_ref[h:h + 1, :] = jnp.broadcast_to(m_new, (1, 128))
    o_ref[0, 0] = jnp.concatenate(outs, axis=1)


def _mlstm(zm, g_col, g_row, b_col, b_row, *, t_lat, cl):
    b, ta, _ = zm.shape
    nc, nc_lat = ta // cl, t_lat // cl
    nc_ctx = nc - nc_lat

    def chunk(di, c):
        fwd_idx = jnp.where(c < nc_ctx, nc_lat + c, c - nc_ctx)
        return jnp.where(di == 0, fwd_idx, nc - 1 - c)

    return pl.pallas_call(
        functools.partial(_mlstm_kernel, cl=cl),
        grid=(b, 2, nc),
        in_specs=[pl.BlockSpec((1, cl, 1152), lambda bi, di, c: (bi, chunk(di, c), 0)),
                  pl.BlockSpec((1, 1, cl, 8), lambda bi, di, c: (bi, di, chunk(di, c), 0)),
                  pl.BlockSpec((1, 1, 8, cl), lambda bi, di, c: (bi, di, 0, chunk(di, c))),
                  pl.BlockSpec((1, 1, 8), lambda bi, di, c: (di, 0, 0)),
                  pl.BlockSpec((1, 8, 1), lambda bi, di, c: (di, 0, 0))],
        out_specs=pl.BlockSpec((1, 1, cl, MLSTM_W), lambda bi, di, c: (bi, di, chunk(di, c), 0)),
        out_shape=jax.ShapeDtypeStruct((b, 2, ta, MLSTM_W), F32),
        scratch_shapes=[pltpu.VMEM((MLSTM_H, MLSTM_DH, MLSTM_DH), F32),
                        pltpu.VMEM((8, 128), F32),
                        pltpu.VMEM((8, 128), F32)],
        compiler_params=_cp(("parallel", "parallel", "arbitrary")),
        name="mlstm",
    )(zm, g_col, g_row, b_col, b_row)


def _rwkv_prep_kernel(zr_ref, wup_ref, aup_ref, gup_ref, vec_ref, hsum_ref, a_ref, s_ref, v_ref, g_ref, bonus_ref):
    z = zr_ref[0]
    zr_, zk, zv = z[:, 0:256], z[:, 256:512], z[:, 512:768]
    zw, za, zg = z[:, 768:896], z[:, 896:1024], z[:, 1024:1152]
    vec = vec_ref[...]
    hsum = hsum_ref[...]
    w_raw = _dot(jnp.tanh(zw).astype(BF16), wup_ref[...])
    a_raw = _dot(za.astype(BF16), aup_ref[...])
    g_ref[0] = _dot(_sigmoid(zg).astype(BF16), gup_ref[...])
    kk = zk * vec[4:5]
    kk_ss = _dot(kk * kk, hsum, precision=HIGHEST)
    kk = kk * lax.rsqrt(jnp.maximum(kk_ss, 1e-12))
    k_sum = jnp.zeros_like(zk)
    for di in range(2):
        sl = slice(di * RWKV_W, (di + 1) * RWKV_W)
        decay = jnp.exp(-jnp.exp(-_softplus(-(vec[di:di + 1] + w_raw[:, sl])) - 0.5))
        a = _sigmoid(vec[2 + di:3 + di] + a_raw[:, sl])
        k_dir = zk * (1.0 + (a - 1.0) * vec[5:6])
        k_sum = k_sum + k_dir
        a_ref[0, di, 0] = decay
        a_ref[0, di, 1] = kk * a
        a_ref[0, di, 2] = k_dir
    s_ref[0, 0] = kk
    s_ref[0, 1] = zr_
    v_ref[0] = zv
    bonus_ref[0] = _dot(zr_ * vec[6:7] * k_sum, hsum, precision=HIGHEST) * zv


def _rwkv_prep(zr, wup, aup, gup, vec, hsum, *, tm):
    b, ta, _ = zr.shape
    tok = pl.BlockSpec((1, tm, RWKV_W), lambda bi, i: (bi, i, 0))
    full = lambda a: pl.BlockSpec(a.shape, lambda bi, i: (0,) * a.ndim)
    return pl.pallas_call(
        _rwkv_prep_kernel,
        grid=(b, ta // tm),
        in_specs=[pl.BlockSpec((1, tm, N_RWKV_IN), lambda bi, i: (bi, i, 0)),
                  full(wup), full(aup), full(gup), full(vec), full(hsum)],
        out_specs=[pl.BlockSpec((1, 2, 3, tm, RWKV_W), lambda bi, i: (bi, 0, 0, i, 0)),
                   pl.BlockSpec((1, 2, tm, RWKV_W), lambda bi, i: (bi, 0, i, 0)), tok, tok, tok],
        out_shape=[jax.ShapeDtypeStruct((b, 2, 3, ta, RWKV_W), F32), jax.ShapeDtypeStruct((b, 2, ta, RWKV_W), F32)]
        + [jax.ShapeDtypeStruct((b, ta, RWKV_W), F32)] * 3,
        compiler_params=_cp(("parallel", "parallel")),
        name="rwkv_prep",
    )(zr, wup, aup, gup, vec, hsum)


def _rwkv_scan_kernel(xf_ref, xb_ref, sf_ref, sb_ref, vf_ref, vb_ref, of_ref, ob_ref, st_ref, *, ts, nvh):
    @pl.when(pl.program_id(0) == 0)
    def _():
        st_ref[...] = jnp.zeros_like(st_ref)

    def one(di, x_ref, s_ref, v_ref, o_ref, t):
        vt = v_ref[t]
        outs = []
        for vh in range(nvh):
            sv = st_ref[di, vh]
            sa = jnp.sum(sv * s_ref[t, 0], axis=0, keepdims=True)
            sv = sv * x_ref[0, t, 0] - sa * x_ref[0, t, 1] + vt[vh:vh + 1, :] * x_ref[0, t, 2]
            st_ref[di, vh] = sv
            outs.append(jnp.sum(sv * s_ref[t, 1], axis=0, keepdims=True))
        o_ref[t] = jnp.concatenate(outs, axis=0)

    def step(t, carry):
        one(0, xf_ref, sf_ref, vf_ref, of_ref, t)
        one(1, xb_ref, sb_ref, vb_ref, ob_ref, ts - 1 - t)
        return carry

    lax.fori_loop(0, ts, step, 0)


def _rwkv_scan(xd, xs, vs, *, t_lat, ts):
    _, ta, _, _, nl = xd.shape
    nvh = vs.shape[1]
    nt, nt_lat = ta // ts, t_lat // ts
    fwd = lambda i: (i + nt_lat) % nt
    bwd = lambda i: nt - 1 - i
    return pl.pallas_call(
        functools.partial(_rwkv_scan_kernel, ts=ts, nvh=nvh),
        grid=(nt,),
        in_specs=[pl.BlockSpec((1, ts, 3, RWKV_N, nl), lambda i: (0, fwd(i), 0, 0, 0)),
                  pl.BlockSpec((1, ts, 3, RWKV_N, nl), lambda i: (1, bwd(i), 0, 0, 0)),
                  pl.BlockSpec((ts, 2, RWKV_N, nl), lambda i: (fwd(i), 0, 0, 0)),
                  pl.BlockSpec((ts, 2, RWKV_N, nl), lambda i: (bwd(i), 0, 0, 0)),
                  pl.BlockSpec((ts, nvh, nl), lambda i: (fwd(i), 0, 0)),
                  pl.BlockSpec((ts, nvh, nl), lambda i: (bwd(i), 0, 0))],
        out_specs=[pl.BlockSpec((ts, nvh, nl), lambda i: (fwd(i), 0, 0)),
                   pl.BlockSpec((ts, nvh, nl), lambda i: (bwd(i), 0, 0))],
        out_shape=[jax.ShapeDtypeStruct((ta, nvh, nl), F32)] * 2,
        scratch_shapes=[pltpu.VMEM((2, nvh, RWKV_N, nl), F32)],
        compiler_params=_cp(("arbitrary",)),
        name="rwkv_scan",
    )(xd, xd, xs, xs, vs, vs)


def _mla_proj_kernel(za_ref, cos_ref, sin_ref, nq_ref, nkv_ref, wq_ref, wqr_ref, wk_ref, wv_ref, q_ref, k_ref, v_ref):
    za = za_ref[0]
    cq, ckv = za[:, 0:Q_RANK], za[:, Q_RANK:Q_RANK + KV_RANK]
    kr, krr = za[:, 384:512], za[:, 512:640]
    cos, sin = cos_ref[...], sin_ref[...]
    cqn = (cq * lax.rsqrt(jnp.mean(cq * cq, axis=-1, keepdims=True) + 1e-6) * nq_ref[...]).astype(BF16)
    ckvn = (ckv * lax.rsqrt(jnp.mean(ckv * ckv, axis=-1, keepdims=True) + 1e-6) * nkv_ref[...]).astype(BF16)
    q = _dot(cqn, wq_ref[...])
    qr = _dot(cqn, wqr_ref[...])
    kn = _dot(ckvn, wk_ref[...])
    v_ref[0] = _nt_dot(wv_ref[...], ckvn).astype(BF16)
    k_rope = kr * cos + krr * sin
    scale = (MLA_NOPE + MLA_ROPE) ** -0.5 * np.log2(np.e)
    for h in range(MLA_H):
        sl = slice(h * 128, (h + 1) * 128)
        q_ref[0, :, sl] = ((q[:, sl] * cos + qr[:, sl] * sin) * scale).astype(BF16)
        k_ref[0, :, sl] = (kn[:, sl] + k_rope).astype(BF16)


def _mla_proj(za, cos, sin, nq, nkv, wq, wqr, wk, wv, *, tm):
    b, ta, _ = za.shape
    full = lambda a: pl.BlockSpec(a.shape, lambda bi, i: (0,) * a.ndim)
    return pl.pallas_call(
        _mla_proj_kernel,
        grid=(b, ta // tm),
        in_specs=[pl.BlockSpec((1, tm, N_A), lambda bi, i: (bi, i, 0)),
                  pl.BlockSpec((tm, 128), lambda bi, i: (i, 0)),
                  pl.BlockSpec((tm, 128), lambda bi, i: (i, 0)),
                  full(nq), full(nkv), full(wq), full(wqr), full(wk), full(wv)],
        out_specs=[pl.BlockSpec((1, tm, 1024), lambda bi, i: (bi, i, 0)),
                   pl.BlockSpec((1, tm, 1024), lambda bi, i: (bi, i, 0)),
                   pl.BlockSpec((1, 512, tm), lambda bi, i: (bi, 0, i))],
        out_shape=[jax.ShapeDtypeStruct((b, ta, 1024), BF16),
                   jax.ShapeDtypeStruct((b, ta, 1024), BF16),
                   jax.ShapeDtypeStruct((b, 512, ta), BF16)],
        compiler_params=_cp(("parallel", "parallel")),
        name="mla_proj",
    )(za, cos, sin, nq, nkv, wq, wqr, wk, wv)


def _attn_kernel(q_ref, k_ref, vt_ref, o_ref):
    outs = []
    for hh in range(2):
        sl = slice(hh * 128, (hh + 1) * 128)
        st = _nt_dot(k_ref[0, :, sl], q_ref[0, :, sl])
        m = jnp.max(st, axis=0, keepdims=True)
        p = jnp.exp2(st - m)
        l = jnp.sum(p, axis=0, keepdims=True)
        ot = _dot(vt_ref[0], p.astype(BF16))
        outs.append(ot[hh * MLA_V:(hh + 1) * MLA_V] / l)
    o_ref[0] = jnp.concatenate(outs, axis=0).T


def _attention(q, k, vt, *, q_off, n_q, k_off, n_keys, tq):
    b = q.shape[0]
    qo, ko = q_off // tq, k_off // n_keys
    return pl.pallas_call(
        _attn_kernel,
        grid=(b, MLA_H // 2, n_q // tq),
        in_specs=[pl.BlockSpec((1, tq, 256), lambda bi, hp, i: (bi, qo + i, hp)),
                  pl.BlockSpec((1, n_keys, 256), lambda bi, hp, i: (bi, ko, hp)),
                  pl.BlockSpec((1, 128, n_keys), lambda bi, hp, i: (bi, hp, ko))],
        out_specs=pl.BlockSpec((1, tq, 128), lambda bi, hp, i: (bi, i, hp)),
        out_shape=jax.ShapeDtypeStruct((b, n_q, 512), F32),
        compiler_params=_cp(("parallel", "parallel", "parallel")),
        name="mla_attention",
    )(q, k, vt)


def _out_kernel(x_ref, hf_ref, hb_ref, zm_ref, of_ref, ob_ref, g_ref, bonus_ref, att_ref, mod_ref, vec_ref,
                hmean_ref, wo_ref, ln_ref, o_ref, *, alpha):
    hmean = hmean_ref[...]

    def head_norm(y, eps):
        mu = _dot(y, hmean, precision=HIGHEST)
        yc = y - mu
        return yc * lax.rsqrt(_dot(yc * yc, hmean, precision=HIGHEST) + eps)

    vec = vec_ref[...]
    zo = zm_ref[0, :, 768:1024]
    m_mix = _sigmoid(zo) * (head_norm(hf_ref[0, 0] + hb_ref[0, 0], 1e-6) * vec[0:1])
    r_o = head_norm(of_ref[0, 0] + ob_ref[0, 0], RWKV_GN_EPS) * vec[1:2] + vec[2:3] + bonus_ref[0]
    r_mix = r_o * g_ref[0]
    mix = jnp.concatenate([m_mix, r_mix, att_ref[0]], axis=1).astype(BF16)
    y = _dot(mix, wo_ref[...])
    gate = mod_ref[0, 0, 2:3, :]
    o_ref[0] = _ln(alpha * x_ref[0] + gate * y) * ln_ref[0:1] + ln_ref[1:2]


def _out_proj(xa, h_m, zm, o_r, g, bonus, att, modv, vec, hmean, wo, ln, *, t_lat, tm, alpha):
    b, ta, d = xa.shape
    nt_lat = t_lat // tm
    tok = lambda w: pl.BlockSpec((1, tm, w), lambda bi, i: (bi, i, 0))
    full = lambda a: pl.BlockSpec(a.shape, lambda bi, i: (0,) * a.ndim)
    return pl.pallas_call(
        functools.partial(_out_kernel, alpha=alpha),
        grid=(b, ta // tm),
        in_specs=[tok(d),
                  pl.BlockSpec((1, 1, tm, MLSTM_W), lambda bi, i: (bi, 0, i, 0)),
                  pl.BlockSpec((1, 1, tm, MLSTM_W), lambda bi, i: (bi, 1, i, 0)),
                  tok(1152),
                  pl.BlockSpec((1, 1, tm, RWKV_W), lambda bi, i: (0, bi, i, 0)),
                  pl.BlockSpec((1, 1, tm, RWKV_W), lambda bi, i: (1, bi, i, 0)),
                  tok(RWKV_W), tok(RWKV_W), tok(512),
                  pl.BlockSpec((1, 1, 6, d), lambda bi, i: (bi, i // nt_lat, 0, 0)),
                  full(vec), full(hmean), full(wo), full(ln)],
        out_specs=tok(d),
        out_shape=jax.ShapeDtypeStruct((b, ta, d), F32),
        compiler_params=_cp(("parallel", "parallel")),
        name="out_proj",
    )(xa, h_m, h_m, zm, o_r, o_r, g, bonus, att, modv, vec, hmean, wo, ln)


def _first_max(x, idx, axis):
    mx = jnp.max(x, axis=axis, keepdims=True)
    first = jnp.min(jnp.where(x == mx, idx, 1 << 20), axis=axis, keepdims=True)
    return mx, idx == first


def _router_kernel(x_ref, mod_ref, w_ref, bias_ref, o_ref):
    tm = x_ref.shape[1]
    sh, sc = mod_ref[0, 0, 3:4, :], mod_ref[0, 0, 4:5, :]
    h = _ln(x_ref[0]) * (1.0 + sc) + sh
    scores = _sigmoid(_nt_dot(w_ref[...], h, precision=HIGHEST))
    biased = scores + bias_ref[...]
    gsz = N_EXPERTS // N_GROUPS
    b3 = biased.reshape(N_GROUPS, gsz, tm)
    s3 = scores.reshape(N_GROUPS, gsz, tm)
    e_idx = lax.broadcasted_iota(jnp.int32, (N_GROUPS, gsz, tm), 1)
    m1, hit = _first_max(b3, e_idx, 1)
    m2 = jnp.max(jnp.where(hit, -jnp.inf, b3), axis=1, keepdims=True)
    gscore = m1 + m2
    g_idx = lax.broadcasted_iota(jnp.int32, (N_GROUPS, 1, tm), 0)
    gsel = jnp.zeros((N_GROUPS, 1, tm), F32)
    for _ in range(TOPK_GROUPS):
        _, hit = _first_max(jnp.where(gsel > 0, -jnp.inf, gscore), g_idx, 0)
        gsel = jnp.where(hit, 1.0, gsel)
    cand = jnp.where(jnp.broadcast_to(gsel, b3.shape) > 0, b3, -jnp.inf)
    x_idx = g_idx * gsz + e_idx
    sel = jnp.zeros(b3.shape, F32)
    for _ in range(TOP_K):
        _, hit = _first_max(jnp.where(sel > 0, -jnp.inf, cand), x_idx, (0, 1))
        sel = jnp.where(hit, 1.0, sel)
    picked = s3 * sel
    gates = ROUTED_SCALE * picked / jnp.sum(picked, axis=(0, 1), keepdims=True)
    o_ref[0] = gates.reshape(N_EXPERTS, tm)


def _router(xa, modv, w_t, bias, *, t_lat, tm):
    b, ta, d = xa.shape
    nt_lat = t_lat // tm
    return pl.pallas_call(
        _router_kernel,
        grid=(b, ta // tm),
        in_specs=[pl.BlockSpec((1, tm, d), lambda bi, i: (bi, i, 0)),
                  pl.BlockSpec((1, 1, 6, d), lambda bi, i: (bi, i // nt_lat, 0, 0)),
                  pl.BlockSpec((N_EXPERTS, d), lambda bi, i: (0, 0)),
                  pl.BlockSpec((N_EXPERTS, 1), lambda bi, i: (0, 0))],
        out_specs=pl.BlockSpec((1, N_EXPERTS, tm), lambda bi, i: (bi, 0, i)),
        out_shape=jax.ShapeDtypeStruct((b, N_EXPERTS, ta), F32),
        compiler_params=_cp(("parallel", "parallel")),
        name="router",
    )(xa, modv, w_t, bias)


def _moe_kernel(x_ref, gates_ref, mod_ref, wg_ref, wu_ref, wd_ref, sg_ref, su_ref, sd_ref, ln_ref, o_ref,
                h_ref, acc_ref, *, t_lat, alpha):
    e = pl.program_id(2)
    tm = x_ref.shape[1]
    row = pl.program_id(1) * tm + lax.broadcasted_iota(jnp.int32, (tm, 1), 0)
    is_ctx = row >= t_lat

    def mod(j):
        return jnp.where(is_ctx, mod_ref[0, 1, j:j + 1, :], mod_ref[0, 0, j:j + 1, :])

    def ffn(h, wg, wu, wd, gate):
        a = _silu(_dot(h, wg)) * _dot(h, wu)
        if gate is not None:
            a = a * gate
        return _dot(a.astype(BF16), wd)

    @pl.when(e == 0)
    def _():
        h = (_ln(x_ref[0]) * (1.0 + mod(4)) + mod(3)).astype(BF16)
        h_ref[...] = h
        acc_ref[...] = ffn(h, sg_ref[...], su_ref[...], sd_ref[...], None)

    lane = lax.broadcasted_iota(jnp.int32, (1, N_EXPERTS), 1)
    gate = jnp.sum(jnp.where(lane == e, gates_ref[0], 0.0), axis=1, keepdims=True)
    acc_ref[...] += ffn(h_ref[...], wg_ref[0], wu_ref[0], wd_ref[0], gate)

    @pl.when(e == N_EXPERTS - 1)
    def _():
        o_ref[0] = _ln(alpha * x_ref[0] + mod(5) * acc_ref[...]) * ln_ref[0:1] + ln_ref[1:2]


def _moe(xa, gates, modv, wg, wu, wd, sg, su, sd, ln, *, t_lat, tm, alpha):
    b, ta, d = xa.shape
    de = wg.shape[2]
    full = lambda a: pl.BlockSpec(a.shape, lambda bi, i, e: (0,) * a.ndim)
    return pl.pallas_call(
        functools.partial(_moe_kernel, t_lat=t_lat, alpha=alpha),
        grid=(b, ta // tm, N_EXPERTS),
        in_specs=[pl.BlockSpec((1, tm, d), lambda bi, i, e: (bi, i, 0)),
                  pl.BlockSpec((1, tm, N_EXPERTS), lambda bi, i, e: (bi, i, 0)),
                  pl.BlockSpec((1, 2, 6, d), lambda bi, i, e: (bi, 0, 0, 0)),
                  pl.BlockSpec((1, d, de), lambda bi, i, e: (e, 0, 0)),
                  pl.BlockSpec((1, d, de), lambda bi, i, e: (e, 0, 0)),
                  pl.BlockSpec((1, de, d), lambda bi, i, e: (e, 0, 0)),
                  full(sg), full(su), full(sd), full(ln)],
        out_specs=pl.BlockSpec((1, tm, d), lambda bi, i, e: (bi, i, 0)),
        out_shape=jax.ShapeDtypeStruct((b, ta, d), F32),
        scratch_shapes=[pltpu.VMEM((tm, d), BF16), pltpu.VMEM((tm, d), F32)],
        compiler_params=_cp(("parallel", "parallel", "arbitrary")),
        name="moe",
    )(xa, gates, modv, wg, wu, wd, sg, su, sd, ln)


def _rope_blocks(w_rope):
    k = w_rope.shape[0]
    ev, od = w_rope[:, 0::2], w_rope[:, 1::2]
    z64, z32 = jnp.zeros((k, 64), w_rope.dtype), jnp.zeros((k, 32), w_rope.dtype)
    return (jnp.concatenate([z64, ev, od, z32], 1), jnp.concatenate([z64, -od, ev, z32], 1))


def _layer_weights(l, w_in, mla_w_uq, mla_w_ukv):
    w = w_in[l]
    d = w.shape[0]
    wm, wr, wa = w[:, :N_MLSTM_IN], w[:, N_MLSTM_IN:N_MLSTM_IN + N_RWKV_IN], w[:, N_MLSTM_IN + N_RWKV_IN:]
    kr, krr = _rope_blocks(wa[:, Q_RANK + KV_RANK:])
    w_ext = jnp.concatenate([
        wm[:, :2 * MLSTM_W], wr, wm[:, 2 * MLSTM_W:4 * MLSTM_W],
        wm[:, 4 * MLSTM_W:], jnp.zeros((d, 128 - 4 * MLSTM_H), w.dtype),
        wa[:, :Q_RANK + KV_RANK], kr, krr], axis=1).astype(BF16)
    uq = mla_w_uq[l].reshape(Q_RANK, MLA_H, MLA_NOPE + MLA_ROPE)
    wq, wqr = [], []
    for h in range(MLA_H):
        r0, r1 = _rope_blocks(uq[:, h, MLA_NOPE:])
        wq.append(r0.at[:, :MLA_NOPE].set(uq[:, h, :MLA_NOPE]))
        wqr.append(r1)
    ukv = mla_w_ukv[l].reshape(KV_RANK, MLA_H, MLA_NOPE + MLA_V)
    wk = jnp.concatenate([ukv[:, :, :MLA_NOPE], jnp.zeros((KV_RANK, MLA_H, 64), F32)], axis=2).reshape(KV_RANK, MLA_H * 128)
    wv = ukv[:, :, MLA_NOPE:].reshape(KV_RANK, MLA_H * MLA_V)
    return w_ext, jnp.concatenate(wq, 1).astype(BF16), jnp.concatenate(wqr, 1).astype(BF16), wk.astype(BF16), wv.T.astype(BF16)


def _rope_tables(t_lat, n_ctx):
    rows = t_lat // GRID_W
    row = jnp.repeat(jnp.arange(rows), GRID_W).astype(F32)
    col = jnp.tile(jnp.arange(GRID_W), rows).astype(F32)
    n_freq = MLA_ROPE // 4
    freq = ROPE_THETA ** (-jnp.arange(n_freq, dtype=F32) / n_freq)
    ang = jnp.concatenate([row[:, None] * freq, col[:, None] * freq], -1)
    cos, sin = jnp.cos(ang), jnp.sin(ang)
    one, zero = jnp.ones((t_lat, 64), F32), jnp.zeros((t_lat, 32), F32)
    cos_l = jnp.concatenate([one, cos, cos, zero], 1)
    sin_l = jnp.concatenate([0 * one, sin, sin, zero], 1)
    cos_c = jnp.concatenate([jnp.ones((n_ctx, 96), F32), jnp.zeros((n_ctx, 32), F32)], 1)
    return jnp.concatenate([cos_l, cos_c], 0), jnp.concatenate([sin_l, jnp.zeros((n_ctx, 128), F32)], 0)


def kernel(x, c, ctx, c_ctx, w_mod, b_mod, w_in, mlstm_conv, mlstm_gate_bias, mlstm_norm_w, rwkv_mu, rwkv_w0, rwkv_w_up, rwkv_a0, rwkv_a_up, rwkv_g_up, rwkv_k_k, rwkv_k_a, rwkv_r_k, rwkv_ln_w, rwkv_ln_b, mla_q_norm, mla_kv_norm, mla_w_uq, mla_w_ukv, w_out, ln1_w, ln1_b, router_w, router_bias, exp_w_gate, exp_w_up, exp_w_down, sh_w_gate, sh_w_up, sh_w_down, ln2_w, ln2_b):
    b, t_lat, d = x.shape
    n_ctx = ctx.shape[1]
    ta = t_lat + n_ctx
    depth = w_in.shape[0]
    alpha = (2 * depth) ** 0.25
    tm = 256
    cl = 128
    tm_moe = ta // 4
    assert b + 1 <= 8 and n_ctx % tm == 0 and t_lat % tm == 0 and tm_moe % 8 == 0

    cc = jnp.concatenate([c, c_ctx[None], jnp.zeros((8 - b - 1, d), F32)], 0)
    mods = _modulation(cc, w_mod, b_mod)
    cos, sin = _rope_tables(t_lat, n_ctx)
    hsum = jnp.kron(jnp.eye(RWKV_H, dtype=F32), jnp.ones((RWKV_N, RWKV_N), F32))
    hmean = hsum / RWKV_N

    xa = jnp.concatenate([x, ctx], axis=1)
    for l in range(depth):
        m_lat = mods[l, :b].reshape(b, 1, 6, d)
        m_ctx = jnp.broadcast_to(mods[l, b].reshape(1, 1, 6, d), (b, 1, 6, d))
        modv = jnp.concatenate([m_lat, m_ctx], axis=1)
        w_ext, wq, wqr, wk, wv = _layer_weights(l, w_in, mla_w_uq, mla_w_ukv)
        mu = rwkv_mu[l].reshape(1, N_RWKV_IN)
        zm, zr, za = _in_proj(xa, modv, w_ext, mlstm_conv[l], mu, t_lat=t_lat, tm=tm)

        gates = zm[:, :, 1024:1024 + 4 * MLSTM_H].reshape(b, ta, 2, 8)
        g_col = gates.transpose(0, 2, 1, 3)
        g_row = gates.transpose(0, 2, 3, 1)
        gb = mlstm_gate_bias[l].reshape(2, 8)
        h_m = _mlstm(zm, g_col, g_row, gb.reshape(2, 1, 8), gb.reshape(2, 8, 1), t_lat=t_lat, cl=cl)

        zeros = jnp.zeros((DECAY_LORA, RWKV_W), F32)
        wup = jnp.concatenate([jnp.concatenate([rwkv_w_up[l, 0], zeros], 1),
                               jnp.concatenate([zeros, rwkv_w_up[l, 1]], 1)], 0).astype(BF16)
        aup = jnp.concatenate([jnp.concatenate([rwkv_a_up[l, 0], zeros], 1),
                               jnp.concatenate([zeros, rwkv_a_up[l, 1]], 1)], 0).astype(BF16)
        vec = jnp.concatenate([rwkv_w0[l], rwkv_a0[l], rwkv_k_k[l][None], rwkv_k_a[l][None], rwkv_r_k[l][None],
                               jnp.zeros((1, RWKV_W), F32)], 0)
        a_dir, a_sh, v_tok, g_tok, bonus = _rwkv_prep(zr, wup, aup, rwkv_g_up[l].astype(BF16), vec, hsum, tm=tm)
        rep = 128 // (b * RWKV_H)
        nvh = RWKV_N // rep
        xd = a_dir.reshape(b, 2, 3, ta, RWKV_H, RWKV_N).transpose(1, 3, 2, 5, 0, 4)
        xd = jnp.broadcast_to(xd[..., None], xd.shape + (rep,)).reshape(2, ta, 3, RWKV_N, 128)
        xs = a_sh.reshape(b, 2, ta, RWKV_H, RWKV_N).transpose(2, 1, 4, 0, 3)
        xs = jnp.broadcast_to(xs[..., None], xs.shape + (rep,)).reshape(ta, 2, RWKV_N, 128)
        vs = v_tok.reshape(b, ta, RWKV_H, nvh, rep).transpose(1, 3, 0, 2, 4).reshape(ta, nvh, 128)
        o_f, o_b = _rwkv_scan(xd, xs, vs, t_lat=t_lat, ts=16)
        o_s = jnp.stack([o_f, o_b], 0).reshape(2, ta, nvh, b, RWKV_H, rep)
        o_s = o_s.transpose(0, 3, 1, 4, 2, 5).reshape(2, b, ta, RWKV_W)

        q, k, vt = _mla_proj(za, cos, sin, mla_q_norm[l][None], mla_kv_norm[l][None], wq, wqr, wk, wv, tm=tm)
        att_l = _attention(q, k, vt, q_off=0, n_q=t_lat, k_off=0, n_keys=ta, tq=tm)
        att_c = _attention(q, k, vt, q_off=t_lat, n_q=n_ctx, k_off=t_lat, n_keys=n_ctx, tq=tm)
        att = jnp.concatenate([att_l, att_c], axis=1)

        vec_o = jnp.concatenate([mlstm_norm_w[l][None], rwkv_ln_w[l][None], rwkv_ln_b[l][None],
                                 jnp.zeros((5, RWKV_W), F32)], 0)
        ln1 = jnp.stack([ln1_w[l], ln1_b[l]], 0)
        xa = _out_proj(xa, h_m, zm, o_s, g_tok, bonus, att, modv, vec_o, hmean, w_out[l].astype(BF16), ln1,
                       t_lat=t_lat, tm=tm, alpha=alpha)

        gates_t = _router(xa, modv, router_w[l].T, router_bias[l].reshape(N_EXPERTS, 1), t_lat=t_lat, tm=tm)
        ln2 = jnp.stack([ln2_w[l], ln2_b[l]], 0)
        xa = _moe(xa, gates_t.transpose(0, 2, 1), modv,
                  exp_w_gate[l].astype(BF16), exp_w_up[l].astype(BF16), exp_w_down[l].astype(BF16),
                  sh_w_gate[l].astype(BF16), sh_w_up[l].astype(BF16), sh_w_down[l].astype(BF16), ln2,
                  t_lat=t_lat, tm=tm_moe, alpha=alpha)
    return xa[:, :t_lat]
```

```python
import functools

import jax
import jax.numpy as jnp
import numpy as np
from jax import lax
from jax.experimental import pallas as pl
from jax.experimental.pallas import tpu as pltpu

F32 = jnp.float32
BF16 = jnp.bfloat16
HIGHEST = lax.Precision.HIGHEST

GRID_W = 64
MLSTM_H, MLSTM_DH = 4, 64
MLSTM_W = MLSTM_H * MLSTM_DH
RWKV_H, RWKV_N = 4, 64
RWKV_W = RWKV_H * RWKV_N
DECAY_LORA, AAA_LORA, GATE_LORA = 64, 64, 128
RWKV_GN_EPS = 64e-5
MLA_H, MLA_NOPE, MLA_ROPE, MLA_V = 8, 64, 32, 64
Q_RANK, KV_RANK = 256, 128
ROPE_THETA = 10000.0
N_MLSTM_IN = 4 * MLSTM_W + 4 * MLSTM_H
N_RWKV_IN = 3 * RWKV_W + 2 * DECAY_LORA + 2 * AAA_LORA + GATE_LORA
N_EXPERTS, TOP_K, N_GROUPS, TOPK_GROUPS = 64, 8, 8, 4
ROUTED_SCALE = 2.5
LN_EPS = 1e-6
NEG = -1e30

C_QK, C_R, C_V, C_O, C_G, C_A = 0, 512, 1664, 1920, 2176, 2304
N_SHIFT = 1664
N_A = 640
N_EXT = C_A + N_A
HALO = 8
VMEM_LIMIT = 56 * 1024 * 1024


def _cp(sem):
    return pltpu.CompilerParams(dimension_semantics=sem, vmem_limit_bytes=VMEM_LIMIT)


def _ln(x):
    mu = jnp.mean(x, axis=-1, keepdims=True)
    xc = x - mu
    return xc * lax.rsqrt(jnp.mean(xc * xc, axis=-1, keepdims=True) + LN_EPS)


def _sigmoid(x):
    return 1.0 / (1.0 + jnp.exp(-x))


def _silu(x):
    return x * _sigmoid(x)


def _log_sigmoid(x):
    return jnp.minimum(x, 0.0) - jnp.log(1.0 + jnp.exp(-jnp.abs(x)))


def _softplus(x):
    return jnp.maximum(x, 0.0) + jnp.log(1.0 + jnp.exp(-jnp.abs(x)))


def _nt_dot(a, b, **kw):
    return lax.dot_general(a, b, (((1,), (1,)), ((), ())), preferred_element_type=F32, **kw)


def _tn_dot(a, b, **kw):
    return lax.dot_general(a, b, (((0,), (0,)), ((), ())), preferred_element_type=F32, **kw)


def _dot(a, b, **kw):
    return jnp.dot(a, b, preferred_element_type=F32, **kw)


def _mod_kernel(c_ref, w_ref, b_ref, o_ref):
    o_ref[0] = _dot(_silu(c_ref[...]), w_ref[0], precision=HIGHEST) + b_ref[0]


def _modulation(cc, w_mod, b_mod):
    n_layers, d, n = w_mod.shape
    tn = 1536
    return pl.pallas_call(
        _mod_kernel,
        grid=(n_layers, n // tn),
        in_specs=[pl.BlockSpec((8, d), lambda l, j: (0, 0)),
                  pl.BlockSpec((1, d, tn), lambda l, j: (l, 0, j)),
                  pl.BlockSpec((1, 1, tn), lambda l, j: (l, 0, j))],
        out_specs=pl.BlockSpec((1, 8, tn), lambda l, j: (l, 0, j)),
        out_shape=jax.ShapeDtypeStruct((n_layers, 8, n), F32),
        compiler_params=_cp(("parallel", "parallel")),
        name="modulation",
    )(cc, w_mod, b_mod.reshape(n_layers, 1, n))


def _in_kernel(xp_ref, x_ref, xn_ref, mod_ref, w_ref, conv_ref, mu_ref, zm_ref, zr_ref, za_ref, *, tm, nt_lat, nt):
    i = pl.program_id(1)
    has_prev = jnp.logical_and(i != 0, i != nt_lat)
    has_next = jnp.logical_and(i != nt_lat - 1, i != nt - 1)
    xt = jnp.concatenate([xp_ref[0], x_ref[0], xn_ref[0]], axis=0)
    sh = mod_ref[0, 0, 0:1, :]
    sc = mod_ref[0, 0, 1:2, :]
    h = (_ln(xt) * (1.0 + sc) + sh).astype(BF16)
    z = _dot(h, w_ref[...])
    rows = lax.broadcasted_iota(jnp.int32, (tm + 2 * HALO, 1), 0)
    lo = jnp.where(has_prev, 0, HALO)
    hi = jnp.where(has_next, tm + 2 * HALO, tm + HALO)
    keep = jnp.logical_and(rows >= lo, rows < hi)
    zs = jnp.where(keep, z[:, :N_SHIFT], 0.0)
    zc = zs[HALO:HALO + tm]
    zprev = pltpu.roll(zs, 1, axis=0)[HALO:HALO + tm]
    znext = pltpu.roll(zs, tm + 2 * HALO - 1, axis=0)[HALO:HALO + tm]
    cw = conv_ref[...]
    qk = (cw[0:1] * zprev[:, :C_R] + cw[1:2] * zc[:, :C_R] + cw[2:3] * znext[:, :C_R])
    qk = _silu(qk)
    lane = lax.broadcasted_iota(jnp.int32, (1, C_R), 1)
    qk = qk * jnp.where(lane < MLSTM_W, MLSTM_DH ** -0.5, 1.0)
    zm_ref[0, :, 0:512] = qk
    zm_ref[0, :, 512:1152] = z[HALO:HALO + tm, C_V:C_A]
    zr = zc[:, C_R:]
    zr_ref[0] = zr + mu_ref[...] * (0.5 * (zprev[:, C_R:] + znext[:, C_R:]) - zr)
    za_ref[0] = z[HALO:HALO + tm, C_A:]


def _in_proj(xa, modv, w_ext, conv, mu, *, t_lat, tm):
    b, ta, d = xa.shape
    nt, nt_lat = ta // tm, t_lat // tm
    nh = tm // HALO
    last = ta // HALO - 1
    kern = functools.partial(_in_kernel, tm=tm, nt_lat=nt_lat, nt=nt)
    return pl.pallas_call(
        kern,
        grid=(b, nt),
        in_specs=[pl.BlockSpec((1, HALO, d), lambda bi, i: (bi, jnp.maximum(i * nh - 1, 0), 0)),
                  pl.BlockSpec((1, tm, d), lambda bi, i: (bi, i, 0)),
                  pl.BlockSpec((1, HALO, d), lambda bi, i: (bi, jnp.minimum((i + 1) * nh, last), 0)),
                  pl.BlockSpec((1, 1, 6, d), lambda bi, i: (bi, i // nt_lat, 0, 0)),
                  pl.BlockSpec((d, N_EXT), lambda bi, i: (0, 0)),
                  pl.BlockSpec((3, C_R), lambda bi, i: (0, 0)),
                  pl.BlockSpec((1, N_RWKV_IN), lambda bi, i: (0, 0))],
        out_specs=[pl.BlockSpec((1, tm, 1152), lambda bi, i: (bi, i, 0)),
                   pl.BlockSpec((1, tm, N_RWKV_IN), lambda bi, i: (bi, i, 0)),
                   pl.BlockSpec((1, tm, N_A), lambda bi, i: (bi, i, 0))],
        out_shape=[jax.ShapeDtypeStruct((b, ta, 1152), F32),
                   jax.ShapeDtypeStruct((b, ta, N_RWKV_IN), F32),
                   jax.ShapeDtypeStruct((b, ta, N_A), F32)],
        compiler_params=_cp(("parallel", "parallel")),
        name="in_proj",
    )(xa, xa, xa, modv, w_ext, conv, mu)


def _mlstm_kernel(zm_ref, gc_ref, gr_ref, bc_ref, br_ref, o_ref, ct_ref, n_ref, m_ref, *, cl):
    d = pl.program_id(1)

    @pl.when(pl.program_id(2) == 0)
    def _():
        ct_ref[...] = jnp.zeros_like(ct_ref)
        n_ref[...] = jnp.zeros_like(n_ref)
        m_ref[...] = jnp.zeros_like(m_ref)

    sgn = 1 - 2 * d
    gc = gc_ref[0, 0] + bc_ref[0]
    gr = gr_ref[0, 0] + br_ref[0]
    lf_c = _log_sigmoid(gc)
    lf_r = _log_sigmoid(gr)
    ii = lax.broadcasted_iota(jnp.int32, (cl, cl), 0)
    jj = lax.broadcasted_iota(jnp.int32, (cl, cl), 1)
    mask = (jj - ii) * sgn <= 0
    mask_t = (ii - jj) * sgn <= 0
    cum_c = _dot(mask.astype(F32), lf_c, precision=HIGHEST)
    cum_r = _dot(lf_r, mask_t.astype(F32), precision=HIGHEST)
    outs = []
    for h in range(MLSTM_H):
        sl = slice(h * MLSTM_DH, (h + 1) * MLSTM_DH)
        q = zm_ref[0, :, sl]
        k = zm_ref[0, :, MLSTM_W + h * MLSTM_DH:MLSTM_W + (h + 1) * MLSTM_DH]
        v = zm_ref[0, :, 2 * MLSTM_W + h * MLSTM_DH:2 * MLSTM_W + (h + 1) * MLSTM_DH]
        qb, kb = q.astype(BF16), k.astype(BF16)
        li_c, li_r = gc[:, h:h + 1], gr[h:h + 1, :]
        cu_c, cu_r = cum_c[:, 4 + h:5 + h], cum_r[4 + h:5 + h, :]
        b_end = jnp.sum(lf_r[4 + h:5 + h, :], axis=1, keepdims=True)
        m_prev = m_ref[h:h + 1, 0:1]
        ct = ct_ref[h]
        nv = n_ref[h:h + 1, 0:MLSTM_DH]
        d_log = jnp.where(mask, cu_c - cu_r + li_r, NEG)
        g_log = cu_c + m_prev
        m_row = jnp.maximum(g_log, jnp.max(d_log, axis=1, keepdims=True))
        w_intra = jnp.exp(d_log - m_row) * _nt_dot(qb, kb)
        e_inter = jnp.exp(g_log - m_row)
        num = _dot(w_intra.astype(BF16), v.astype(BF16)) + e_inter * _dot(qb, ct.astype(BF16))
        den = jnp.sum(w_intra, axis=1, keepdims=True) + e_inter * jnp.sum(q * nv, axis=1, keepdims=True)
        outs.append(num / jnp.maximum(jnp.abs(den), jnp.exp(-m_row)))
        w_end = b_end - cu_c + li_c
        m_loc = jnp.max(w_end, axis=0, keepdims=True)
        e_end = jnp.exp(w_end - m_loc)
        m_new = jnp.maximum(b_end + m_prev, m_loc)
        a_old = jnp.exp(b_end + m_prev - m_new)
        a_loc = jnp.exp(m_loc - m_new)
        ct_ref[h] = a_old * ct + a_loc * _tn_dot(kb, (v * e_end).astype(BF16))
        n_ref[h:h + 1, 0:MLSTM_DH] = a_old * nv + a_loc * jnp.sum(k * e_end, axis=0, keepdims=True)
        m_ref[h:h + 1, :] = jnp.broadcast_to(m_new, (1, 128))
    o_ref[0, 0] = jnp.concatenate(outs, axis=1)


def _mlstm(zm, g_col, g_row, b_col, b_row, *, t_lat, cl):
    b, ta, _ = zm.shape
    nc, nc_lat = ta // cl, t_lat // cl
    nc_ctx = nc - nc_lat

    def chunk(di, c):
        fwd_idx = jnp.where(c < nc_ctx, nc_lat + c, c - nc_ctx)
        return jnp.where(di == 0, fwd_idx, nc - 1 - c)

    return pl.pallas_call(
        functools.partial(_mlstm_kernel, cl=cl),
        grid=(b, 2, nc),
        in_specs=[pl.BlockSpec((1, cl, 1152), lambda bi, di, c: (bi, chunk(di, c), 0)),
                  pl.BlockSpec((1, 1, cl, 8), lambda bi, di, c: (bi, di, chunk(di, c), 0)),
                  pl.BlockSpec((1, 1, 8, cl), lambda bi, di, c: (bi, di, 0, chunk(di, c))),
                  pl.BlockSpec((1, 1, 8), lambda bi, di, c: (di, 0, 0)),
                  pl.BlockSpec((1, 8, 1), lambda bi, di, c: (di, 0, 0))],
        out_specs=pl.BlockSpec((1, 1, cl, MLSTM_W), lambda bi, di, c: (bi, di, chunk(di, c), 0)),
        out_shape=jax.ShapeDtypeStruct((b, 2, ta, MLSTM_W), F32),
        scratch_shapes=[pltpu.VMEM((MLSTM_H, MLSTM_DH, MLSTM_DH), F32),
                        pltpu.VMEM((8, 128), F32),
                        pltpu.VMEM((8, 128), F32)],
        compiler_params=_cp(("parallel", "parallel", "arbitrary")),
        name="mlstm",
    )(zm, g_col, g_row, b_col, b_row)


def _rwkv_prep_kernel(zr_ref, wup_ref, aup_ref, gup_ref, vec_ref, hsum_ref, pd_ref, v_ref, g_ref, bonus_ref):
    nb, tm = zr_ref.shape[0], zr_ref.shape[1]
    nbh = nb * RWKV_H
    vec = vec_ref[...]
    hsum = hsum_ref[...]
    zero = jnp.zeros((tm, RWKV_N), F32)
    for bi in range(nb):
        z = zr_ref[bi]
        zr_, zk, zv = z[:, 0:256], z[:, 256:512], z[:, 512:768]
        zw, za, zg = z[:, 768:896], z[:, 896:1024], z[:, 1024:1152]
        w_raw = _dot(jnp.tanh(zw).astype(BF16), wup_ref[...])
        a_raw = _dot(za.astype(BF16), aup_ref[...])
        g_ref[bi] = _dot(_sigmoid(zg).astype(BF16), gup_ref[...])
        kk = zk * vec[4:5]
        kk_ss = _dot(kk * kk, hsum, precision=HIGHEST)
        kk = kk * lax.rsqrt(jnp.maximum(kk_ss, 1e-12))
        k_sum = jnp.zeros_like(zk)
        for di in range(2):
            sl = slice(di * RWKV_W, (di + 1) * RWKV_W)
            decay = jnp.exp(-jnp.exp(-_softplus(-(vec[di:di + 1] + w_raw[:, sl])) - 0.5))
            a = _sigmoid(vec[2 + di:3 + di] + a_raw[:, sl])
            k_dir = zk * (1.0 + (a - 1.0) * vec[5:6])
            k_sum = k_sum + k_dir
            ka = kk * a
            for h in range(RWKV_H):
                hs = slice(h * RWKV_N, (h + 1) * RWKV_N)
                rows = pl.ds(bi * RWKV_H + h, tm, stride=nbh)
                pd_ref[di, 0, rows, :] = jnp.concatenate([decay[:, hs], ka[:, hs]], axis=1)
                pd_ref[di, 1, rows, :] = jnp.concatenate([k_dir[:, hs], kk[:, hs]], axis=1)
                pd_ref[di, 2, rows, :] = jnp.concatenate([zr_[:, hs], zero], axis=1)
        v_ref[bi] = zv
        bonus_ref[bi] = _dot(zr_ * vec[6:7] * k_sum, hsum, precision=HIGHEST) * zv


def _rwkv_prep(zr, wup, aup, gup, vec, hsum, *, tm):
    b, ta, _ = zr.shape
    nbh = b * RWKV_H
    tok = pl.BlockSpec((b, tm, RWKV_W), lambda i: (0, i, 0))
    full = lambda a: pl.BlockSpec(a.shape, lambda i: (0,) * a.ndim)
    return pl.pallas_call(
        _rwkv_prep_kernel,
        grid=(ta // tm,),
        in_specs=[pl.BlockSpec((b, tm, N_RWKV_IN), lambda i: (0, i, 0)),
                  full(wup), full(aup), full(gup), full(vec), full(hsum)],
        out_specs=[pl.BlockSpec((2, 3, tm * nbh, 128), lambda i: (0, 0, i, 0)), tok, tok, tok],
        out_shape=[jax.ShapeDtypeStruct((2, 3, ta * nbh, 128), F32)] + [jax.ShapeDtypeStruct((b, ta, RWKV_W), F32)] * 3,
        compiler_params=_cp(("parallel",)),
        name="rwkv_prep",
    )(zr, wup, aup, gup, vec, hsum)


def _rwkv_scan_kernel(pf_ref, pb_ref, pat_ref, vf_ref, vb_ref, of_ref, ob_ref, st_ref, x_ref, tt_ref, *, ts, nvh, nbh):
    ng = ts // 8
    rows = 8 * nbh

    @pl.when(pl.program_id(0) == 0)
    def _():
        st_ref[...] = jnp.zeros_like(st_ref)

    def transpose(di, p_ref, grp):
        sl = pl.ds(pl.multiple_of(grp * rows, rows), rows)
        for j in range(3):
            tt_ref[di, j] = p_ref[0, j, sl, :].T

    def update(di, u, v_ref, o_ref, t):
        pat = pat_ref[u]
        for j, (tj, half) in enumerate(((0, 0), (0, 1), (1, 0), (1, 1), (2, 0))):
            x_ref[di, j] = jnp.take_along_axis(tt_ref[di, tj, half * RWKV_N:(half + 1) * RWKV_N, :], pat, axis=1)
        vt = v_ref[t]
        outs = []
        for vh in range(nvh):
            sv = st_ref[di, vh]
            sa = jnp.sum(sv * x_ref[di, 3], axis=0, keepdims=True)
            sv = sv * x_ref[di, 0] - sa * x_ref[di, 1] + vt[vh:vh + 1, :] * x_ref[di, 2]
            st_ref[di, vh] = sv
            outs.append(jnp.sum(sv * x_ref[di, 4], axis=0, keepdims=True))
        o_ref[t] = jnp.concatenate(outs, axis=0)

    def group(g, carry):
        transpose(0, pf_ref, g)
        transpose(1, pb_ref, ng - 1 - g)
        for u in range(8):
            update(0, u, vf_ref, of_ref, g * 8 + u)
            update(1, 7 - u, vb_ref, ob_ref, ts - 1 - (g * 8 + u))
        return carry

    lax.fori_loop(0, ng, group, 0)


def _rwkv_scan(pd, pats, vs, *, t_lat, ts):
    ta, nvh, nl = vs.shape
    nbh = pd.shape[2] // ta
    nt, nt_lat = ta // ts, t_lat // ts
    fwd = lambda i: (i + nt_lat) % nt
    bwd = lambda i: nt - 1 - i
    return pl.pallas_call(
        functools.partial(_rwkv_scan_kernel, ts=ts, nvh=nvh, nbh=nbh),
        grid=(nt,),
        in_specs=[pl.BlockSpec((1, 3, ts * nbh, 128), lambda i: (0, 0, fwd(i), 0)),
                  pl.BlockSpec((1, 3, ts * nbh, 128), lambda i: (1, 0, bwd(i), 0)),
                  pl.BlockSpec(pats.shape, lambda i: (0, 0, 0)),
                  pl.BlockSpec((ts, nvh, nl), lambda i: (fwd(i), 0, 0)),
                  pl.BlockSpec((ts, nvh, nl), lambda i: (bwd(i), 0, 0))],
        out_specs=[pl.BlockSpec((ts, nvh, nl), lambda i: (fwd(i), 0, 0)),
                   pl.BlockSpec((ts, nvh, nl), lambda i: (bwd(i), 0, 0))],
        out_shape=[jax.ShapeDtypeStruct((ta, nvh, nl), F32)] * 2,
        scratch_shapes=[pltpu.VMEM((2, nvh, RWKV_N, nl), F32), pltpu.VMEM((2, 5, RWKV_N, nl), F32),
                        pltpu.VMEM((2, 3, 128, 8 * nbh), F32)],
        compiler_params=_cp(("arbitrary",)),
        name="rwkv_scan",
    )(pd, pd, pats, vs, vs)


def _mla_proj_kernel(za_ref, cos_ref, sin_ref, nq_ref, nkv_ref, wq_ref, wqr_ref, wk_ref, wv_ref, q_ref, k_ref, v_ref):
    za = za_ref[0]
    cq, ckv = za[:, 0:Q_RANK], za[:, Q_RANK:Q_RANK + KV_RANK]
    kr, krr = za[:, 384:512], za[:, 512:640]
    cos, sin = cos_ref[...], sin_ref[...]
    cqn = (cq * lax.rsqrt(jnp.mean(cq * cq, axis=-1, keepdims=True) + 1e-6) * nq_ref[...]).astype(BF16)
    ckvn = (ckv * lax.rsqrt(jnp.mean(ckv * ckv, axis=-1, keepdims=True) + 1e-6) * nkv_ref[...]).astype(BF16)
    q = _dot(cqn, wq_ref[...])
    qr = _dot(cqn, wqr_ref[...])
    kn = _dot(ckvn, wk_ref[...])
    v_ref[0] = _nt_dot(wv_ref[...], ckvn).astype(BF16)
    k_rope = kr * cos + krr * sin
    scale = (MLA_NOPE + MLA_ROPE) ** -0.5 * np.log2(np.e)
    for h in range(MLA_H):
        sl = slice(h * 128, (h + 1) * 128)
        q_ref[0, :, sl] = ((q[:, sl] * cos + qr[:, sl] * sin) * scale).astype(BF16)
        k_ref[0, :, sl] = (kn[:, sl] + k_rope).astype(BF16)


def _mla_proj(za, cos, sin, nq, nkv, wq, wqr, wk, wv, *, tm):
    b, ta, _ = za.shape
    full = lambda a: pl.BlockSpec(a.shape, lambda bi, i: (0,) * a.ndim)
    return pl.pallas_call(
        _mla_proj_kernel,
        grid=(b, ta // tm),
        in_specs=[pl.BlockSpec((1, tm, N_A), lambda bi, i: (bi, i, 0)),
                  pl.BlockSpec((tm, 128), lambda bi, i: (i, 0)),
                  pl.BlockSpec((tm, 128), lambda bi, i: (i, 0)),
                  full(nq), full(nkv), full(wq), full(wqr), full(wk), full(wv)],
        out_specs=[pl.BlockSpec((1, tm, 1024), lambda bi, i: (bi, i, 0)),
                   pl.BlockSpec((1, tm, 1024), lambda bi, i: (bi, i, 0)),
                   pl.BlockSpec((1, 512, tm), lambda bi, i: (bi, 0, i))],
        out_shape=[jax.ShapeDtypeStruct((b, ta, 1024), BF16),
                   jax.ShapeDtypeStruct((b, ta, 1024), BF16),
                   jax.ShapeDtypeStruct((b, 512, ta), BF16)],
        compiler_params=_cp(("parallel", "parallel")),
        name="mla_proj",
    )(za, cos, sin, nq, nkv, wq, wqr, wk, wv)


def _attn_kernel(q_ref, k_ref, vt_ref, o_ref):
    outs = []
    for hh in range(2):
        sl = slice(hh * 128, (hh + 1) * 128)
        st = _nt_dot(k_ref[0, :, sl], q_ref[0, :, sl])
        m = jnp.max(st, axis=0, keepdims=True)
        p = jnp.exp2(st - m)
        l = jnp.sum(p, axis=0, keepdims=True)
        ot = _dot(vt_ref[0], p.astype(BF16))
        outs.append(ot[hh * MLA_V:(hh + 1) * MLA_V] / l)
    o_ref[0] = jnp.concatenate(outs, axis=0).T


def _attention(q, k, vt, *, q_off, n_q, k_off, n_keys, tq):
    b = q.shape[0]
    qo, ko = q_off // tq, k_off // n_keys
    return pl.pallas_call(
        _attn_kernel,
        grid=(b, MLA_H // 2, n_q // tq),
        in_specs=[pl.BlockSpec((1, tq, 256), lambda bi, hp, i: (bi, qo + i, hp)),
                  pl.BlockSpec((1, n_keys, 256), lambda bi, hp, i: (bi, ko, hp)),
                  pl.BlockSpec((1, 128, n_keys), lambda bi, hp, i: (bi, hp, ko))],
        out_specs=pl.BlockSpec((1, tq, 128), lambda bi, hp, i: (bi, i, hp)),
        out_shape=jax.ShapeDtypeStruct((b, n_q, 512), F32),
        compiler_params=_cp(("parallel", "parallel", "parallel")),
        name="mla_attention",
    )(q, k, vt)


def _out_kernel(x_ref, hf_ref, hb_ref, zm_ref, of_ref, ob_ref, g_ref, bonus_ref, att_ref, mod_ref, vec_ref,
                hmean_ref, wo_ref, ln_ref, o_ref, *, alpha):
    hmean = hmean_ref[...]

    def head_norm(y, eps):
        mu = _dot(y, hmean, precision=HIGHEST)
        yc = y - mu
        return yc * lax.rsqrt(_dot(yc * yc, hmean, precision=HIGHEST) + eps)

    vec = vec_ref[...]
    zo = zm_ref[0, :, 768:1024]
    m_mix = _sigmoid(zo) * (head_norm(hf_ref[0, 0] + hb_ref[0, 0], 1e-6) * vec[0:1])
    r_o = head_norm(of_ref[0, 0] + ob_ref[0, 0], RWKV_GN_EPS) * vec[1:2] + vec[2:3] + bonus_ref[0]
    r_mix = r_o * g_ref[0]
    mix = jnp.concatenate([m_mix, r_mix, att_ref[0]], axis=1).astype(BF16)
    y = _dot(mix, wo_ref[...])
    gate = mod_ref[0, 0, 2:3, :]
    o_ref[0] = _ln(alpha * x_ref[0] + gate * y) * ln_ref[0:1] + ln_ref[1:2]


def _out_proj(xa, h_m, zm, o_r, g, bonus, att, modv, vec, hmean, wo, ln, *, t_lat, tm, alpha):
    b, ta, d = xa.shape
    nt_lat = t_lat // tm
    tok = lambda w: pl.BlockSpec((1, tm, w), lambda bi, i: (bi, i, 0))
    full = lambda a: pl.BlockSpec(a.shape, lambda bi, i: (0,) * a.ndim)
    return pl.pallas_call(
        functools.partial(_out_kernel, alpha=alpha),
        grid=(b, ta // tm),
        in_specs=[tok(d),
                  pl.BlockSpec((1, 1, tm, MLSTM_W), lambda bi, i: (bi, 0, i, 0)),
                  pl.BlockSpec((1, 1, tm, MLSTM_W), lambda bi, i: (bi, 1, i, 0)),
                  tok(1152),
                  pl.BlockSpec((1, 1, tm, RWKV_W), lambda bi, i: (0, bi, i, 0)),
                  pl.BlockSpec((1, 1, tm, RWKV_W), lambda bi, i: (1, bi, i, 0)),
                  tok(RWKV_W), tok(RWKV_W), tok(512),
                  pl.BlockSpec((1, 1, 6, d), lambda bi, i: (bi, i // nt_lat, 0, 0)),
                  full(vec), full(hmean), full(wo), full(ln)],
        out_specs=tok(d),
        out_shape=jax.ShapeDtypeStruct((b, ta, d), F32),
        compiler_params=_cp(("parallel", "parallel")),
        name="out_proj",
    )(xa, h_m, h_m, zm, o_r, o_r, g, bonus, att, modv, vec, hmean, wo, ln)


def _first_max(x, idx, axis):
    mx = jnp.max(x, axis=axis, keepdims=True)
    first = jnp.min(jnp.where(x == mx, idx, 1 << 20), axis=axis, keepdims=True)
    return mx, idx == first


def _router_kernel(x_ref, mod_ref, w_ref, bias_ref, o_ref):
    tm = x_ref.shape[1]
    sh, sc = mod_ref[0, 0, 3:4, :], mod_ref[0, 0, 4:5, :]
    h = _ln(x_ref[0]) * (1.0 + sc) + sh
    scores = _sigmoid(_nt_dot(w_ref[...], h, precision=HIGHEST))
    biased = scores + bias_ref[...]
    gsz = N_EXPERTS // N_GROUPS
    b3 = biased.reshape(N_GROUPS, gsz, tm)
    s3 = scores.reshape(N_GROUPS, gsz, tm)
    e_idx = lax.broadcasted_iota(jnp.int32, (N_GROUPS, gsz, tm), 1)
    m1, hit = _first_max(b3, e_idx, 1)
    m2 = jnp.max(jnp.where(hit, -jnp.inf, b3), axis=1, keepdims=True)
    gscore = m1 + m2
    g_idx = lax.broadcasted_iota(jnp.int32, (N_GROUPS, 1, tm), 0)
    gsel = jnp.zeros((N_GROUPS, 1, tm), F32)
    for _ in range(TOPK_GROUPS):
        _, hit = _first_max(jnp.where(gsel > 0, -jnp.inf, gscore), g_idx, 0)
        gsel = jnp.where(hit, 1.0, gsel)
    cand = jnp.where(jnp.broadcast_to(gsel, b3.shape) > 0, b3, -jnp.inf)
    x_idx = g_idx * gsz + e_idx
    sel = jnp.zeros(b3.shape, F32)
    for _ in range(TOP_K):
        _, hit = _first_max(jnp.where(sel > 0, -jnp.inf, cand), x_idx, (0, 1))
        sel = jnp.where(hit, 1.0, sel)
    picked = s3 * sel
    gates = ROUTED_SCALE * picked / jnp.sum(picked, axis=(0, 1), keepdims=True)
    o_ref[0] = gates.reshape(N_EXPERTS, tm)


def _router(xa, modv, w_t, bias, *, t_lat, tm):
    b, ta, d = xa.shape
    nt_lat = t_lat // tm
    return pl.pallas_call(
        _router_kernel,
        grid=(b, ta // tm),
        in_specs=[pl.BlockSpec((1, tm, d), lambda bi, i: (bi, i, 0)),
                  pl.BlockSpec((1, 1, 6, d), lambda bi, i: (bi, i // nt_lat, 0, 0)),
                  pl.BlockSpec((N_EXPERTS, d), lambda bi, i: (0, 0)),
                  pl.BlockSpec((N_EXPERTS, 1), lambda bi, i: (0, 0))],
        out_specs=pl.BlockSpec((1, N_EXPERTS, tm), lambda bi, i: (bi, 0, i)),
        out_shape=jax.ShapeDtypeStruct((b, N_EXPERTS, ta), F32),
        compiler_params=_cp(("parallel", "parallel")),
        name="router",
    )(xa, modv, w_t, bias)


def _moe_kernel(x_ref, gates_ref, mod_ref, wg_ref, wu_ref, wd_ref, sg_ref, su_ref, sd_ref, ln_ref, o_ref,
                h_ref, acc_ref, *, t_lat, alpha):
    e = pl.program_id(2)
    tm = x_ref.shape[1]
    row = pl.program_id(1) * tm + lax.broadcasted_iota(jnp.int32, (tm, 1), 0)
    is_ctx = row >= t_lat

    def mod(j):
        return jnp.where(is_ctx, mod_ref[0, 1, j:j + 1, :], mod_ref[0, 0, j:j + 1, :])

    def ffn(h, wg, wu, wd, gate):
        a = _silu(_dot(h, wg)) * _dot(h, wu)
        if gate is not None:
            a = a * gate
        return _dot(a.astype(BF16), wd)

    @pl.when(e == 0)
    def _():
        h = (_ln(x_ref[0]) * (1.0 + mod(4)) + mod(3)).astype(BF16)
        h_ref[...] = h
        acc_ref[...] = ffn(h, sg_ref[...], su_ref[...], sd_ref[...], None)

    lane = lax.broadcasted_iota(jnp.int32, (1, N_EXPERTS), 1)
    gate = jnp.sum(jnp.where(lane == e, gates_ref[0], 0.0), axis=1, keepdims=True)
    acc_ref[...] += ffn(h_ref[...], wg_ref[0], wu_ref[0], wd_ref[0], gate)

    @pl.when(e == N_EXPERTS - 1)
    def _():
        o_ref[0] = _ln(alpha * x_ref[0] + mod(5) * acc_ref[...]) * ln_ref[0:1] + ln_ref[1:2]


def _moe(xa, gates, modv, wg, wu, wd, sg, su, sd, ln, *, t_lat, tm, alpha):
    b, ta, d = xa.shape
    de = wg.shape[2]
    full = lambda a: pl.BlockSpec(a.shape, lambda bi, i, e: (0,) * a.ndim)
    return pl.pallas_call(
        functools.partial(_moe_kernel, t_lat=t_lat, alpha=alpha),
        grid=(b, ta // tm, N_EXPERTS),
        in_specs=[pl.BlockSpec((1, tm, d), lambda bi, i, e: (bi, i, 0)),
                  pl.BlockSpec((1, tm, N_EXPERTS), lambda bi, i, e: (bi, i, 0)),
                  pl.BlockSpec((1, 2, 6, d), lambda bi, i, e: (bi, 0, 0, 0)),
                  pl.BlockSpec((1, d, de), lambda bi, i, e: (e, 0, 0)),
                  pl.BlockSpec((1, d, de), lambda bi, i, e: (e, 0, 0)),
                  pl.BlockSpec((1, de, d), lambda bi, i, e: (e, 0, 0)),
                  full(sg), full(su), full(sd), full(ln)],
        out_specs=pl.BlockSpec((1, tm, d), lambda bi, i, e: (bi, i, 0)),
        out_shape=jax.ShapeDtypeStruct((b, ta, d), F32),
        scratch_shapes=[pltpu.VMEM((tm, d), BF16), pltpu.VMEM((tm, d), F32)],
        compiler_params=_cp(("parallel", "parallel", "arbitrary")),
        name="moe",
    )(xa, gates, modv, wg, wu, wd, sg, su, sd, ln)


def _rope_blocks(w_rope):
    k = w_rope.shape[0]
    ev, od = w_rope[:, 0::2], w_rope[:, 1::2]
    z64, z32 = jnp.zeros((k, 64), w_rope.dtype), jnp.zeros((k, 32), w_rope.dtype)
    return (jnp.concatenate([z64, ev, od, z32], 1), jnp.concatenate([z64, -od, ev, z32], 1))


def _layer_weights(l, w_in, mla_w_uq, mla_w_ukv):
    w = w_in[l]
    d = w.shape[0]
    wm, wr, wa = w[:, :N_MLSTM_IN], w[:, N_MLSTM_IN:N_MLSTM_IN + N_RWKV_IN], w[:, N_MLSTM_IN + N_RWKV_IN:]
    kr, krr = _rope_blocks(wa[:, Q_RANK + KV_RANK:])
    w_ext = jnp.concatenate([
        wm[:, :2 * MLSTM_W], wr, wm[:, 2 * MLSTM_W:4 * MLSTM_W],
        wm[:, 4 * MLSTM_W:], jnp.zeros((d, 128 - 4 * MLSTM_H), w.dtype),
        wa[:, :Q_RANK + KV_RANK], kr, krr], axis=1).astype(BF16)
    uq = mla_w_uq[l].reshape(Q_RANK, MLA_H, MLA_NOPE + MLA_ROPE)
    wq, wqr = [], []
    for h in range(MLA_H):
        r0, r1 = _rope_blocks(uq[:, h, MLA_NOPE:])
        wq.append(r0.at[:, :MLA_NOPE].set(uq[:, h, :MLA_NOPE]))
        wqr.append(r1)
    ukv = mla_w_ukv[l].reshape(KV_RANK, MLA_H, MLA_NOPE + MLA_V)
    wk = jnp.concatenate([ukv[:, :, :MLA_NOPE], jnp.zeros((KV_RANK, MLA_H, 64), F32)], axis=2).reshape(KV_RANK, MLA_H * 128)
    wv = ukv[:, :, MLA_NOPE:].reshape(KV_RANK, MLA_H * MLA_V)
    return w_ext, jnp.concatenate(wq, 1).astype(BF16), jnp.concatenate(wqr, 1).astype(BF16), wk.astype(BF16), wv.T.astype(BF16)


def _rope_tables(t_lat, n_ctx):
    rows = t_lat // GRID_W
    row = jnp.repeat(jnp.arange(rows), GRID_W).astype(F32)
    col = jnp.tile(jnp.arange(GRID_W), rows).astype(F32)
    n_freq = MLA_ROPE // 4
    freq = ROPE_THETA ** (-jnp.arange(n_freq, dtype=F32) / n_freq)
    ang = jnp.concatenate([row[:, None] * freq, col[:, None] * freq], -1)
    cos, sin = jnp.cos(ang), jnp.sin(ang)
    one, zero = jnp.ones((t_lat, 64), F32), jnp.zeros((t_lat, 32), F32)
    cos_l = jnp.concatenate([one, cos, cos, zero], 1)
    sin_l = jnp.concatenate([0 * one, sin, sin, zero], 1)
    cos_c = jnp.concatenate([jnp.ones((n_ctx, 96), F32), jnp.zeros((n_ctx, 32), F32)], 1)
    return jnp.concatenate([cos_l, cos_c], 0), jnp.concatenate([sin_l, jnp.zeros((n_ctx, 128), F32)], 0)


def kernel(x, c, ctx, c_ctx, w_mod, b_mod, w_in, mlstm_conv, mlstm_gate_bias, mlstm_norm_w, rwkv_mu, rwkv_w0, rwkv_w_up, rwkv_a0, rwkv_a_up, rwkv_g_up, rwkv_k_k, rwkv_k_a, rwkv_r_k, rwkv_ln_w, rwkv_ln_b, mla_q_norm, mla_kv_norm, mla_w_uq, mla_w_ukv, w_out, ln1_w, ln1_b, router_w, router_bias, exp_w_gate, exp_w_up, exp_w_down, sh_w_gate, sh_w_up, sh_w_down, ln2_w, ln2_b):
    b, t_lat, d = x.shape
    n_ctx = ctx.shape[1]
    ta = t_lat + n_ctx
    depth = w_in.shape[0]
    alpha = (2 * depth) ** 0.25
    tm = 256
    cl = 256
    tm_moe = ta // 4
    assert b + 1 <= 8 and n_ctx % tm == 0 and t_lat % tm == 0 and tm_moe % 8 == 0

    cc = jnp.concatenate([c, c_ctx[None], jnp.zeros((8 - b - 1, d), F32)], 0)
    mods = _modulation(cc, w_mod, b_mod)
    cos, sin = _rope_tables(t_lat, n_ctx)
    hsum = jnp.kron(jnp.eye(RWKV_H, dtype=F32), jnp.ones((RWKV_N, RWKV_N), F32))
    hmean = hsum / RWKV_N
    lane = jnp.arange(128, dtype=jnp.int32) // (128 // (b * RWKV_H))
    pats = jnp.broadcast_to((jnp.arange(8, dtype=jnp.int32) * (b * RWKV_H))[:, None, None] + lane, (8, RWKV_N, 128))

    xa = jnp.concatenate([x, ctx], axis=1)
    for l in range(depth):
        m_lat = mods[l, :b].reshape(b, 1, 6, d)
        m_ctx = jnp.broadcast_to(mods[l, b].reshape(1, 1, 6, d), (b, 1, 6, d))
        modv = jnp.concatenate([m_lat, m_ctx], axis=1)
        w_ext, wq, wqr, wk, wv = _layer_weights(l, w_in, mla_w_uq, mla_w_ukv)
        mu = rwkv_mu[l].reshape(1, N_RWKV_IN)
        zm, zr, za = _in_proj(xa, modv, w_ext, mlstm_conv[l], mu, t_lat=t_lat, tm=tm)

        gates = zm[:, :, 1024:1024 + 4 * MLSTM_H].reshape(b, ta, 2, 8)
        g_col = gates.transpose(0, 2, 1, 3)
        g_row = gates.transpose(0, 2, 3, 1)
        gb = mlstm_gate_bias[l].reshape(2, 8)
        h_m = _mlstm(zm, g_col, g_row, gb.reshape(2, 1, 8), gb.reshape(2, 8, 1), t_lat=t_lat, cl=cl)

        zeros = jnp.zeros((DECAY_LORA, RWKV_W), F32)
        wup = jnp.concatenate([jnp.concatenate([rwkv_w_up[l, 0], zeros], 1),
                               jnp.concatenate([zeros, rwkv_w_up[l, 1]], 1)], 0).astype(BF16)
        aup = jnp.concatenate([jnp.concatenate([rwkv_a_up[l, 0], zeros], 1),
                               jnp.concatenate([zeros, rwkv_a_up[l, 1]], 1)], 0).astype(BF16)
        vec = jnp.concatenate([rwkv_w0[l], rwkv_a0[l], rwkv_k_k[l][None], rwkv_k_a[l][None], rwkv_r_k[l][None],
                               jnp.zeros((1, RWKV_W), F32)], 0)
        pd, v_tok, g_tok, bonus = _rwkv_prep(zr, wup, aup, rwkv_g_up[l].astype(BF16), vec, hsum, tm=128)
        rep = 128 // (b * RWKV_H)
        nvh = RWKV_N // rep
        vs = v_tok.reshape(b, ta, RWKV_H, nvh, rep).transpose(1, 3, 0, 2, 4).reshape(ta, nvh, 128)
        o_f, o_b = _rwkv_scan(pd, pats, vs, t_lat=t_lat, ts=32)
        o_s = jnp.stack([o_f, o_b], 0).reshape(2, ta, nvh, b, RWKV_H, rep)
        o_s = o_s.transpose(0, 3, 1, 4, 2, 5).reshape(2, b, ta, RWKV_W)

        q, k, vt = _mla_proj(za, cos, sin, mla_q_norm[l][None], mla_kv_norm[l][None], wq, wqr, wk, wv, tm=tm)
        att_l = _attention(q, k, vt, q_off=0, n_q=t_lat, k_off=0, n_keys=ta, tq=tm)
        att_c = _attention(q, k, vt, q_off=t_lat, n_q=n_ctx, k_off=t_lat, n_keys=n_ctx, tq=tm)
        att = jnp.concatenate([att_l, att_c], axis=1)

        vec_o = jnp.concatenate([mlstm_norm_w[l][None], rwkv_ln_w[l][None], rwkv_ln_b[l][None],
                                 jnp.zeros((5, RWKV_W), F32)], 0)
        ln1 = jnp.stack([ln1_w[l], ln1_b[l]], 0)
        xa = _out_proj(xa, h_m, zm, o_s, g_tok, bonus, att, modv, vec_o, hmean, w_out[l].astype(BF16), ln1,
                       t_lat=t_lat, tm=tm, alpha=alpha)

        gates_t = _router(xa, modv, router_w[l].T, router_bias[l].reshape(N_EXPERTS, 1), t_lat=t_lat, tm=tm)
        ln2 = jnp.stack([ln2_w[l], ln2_b[l]], 0)
        xa = _moe(xa, gates_t.transpose(0, 2, 1), modv,
                  exp_w_gate[l].astype(BF16), exp_w_up[l].astype(BF16), exp_w_down[l].astype(BF16),
                  sh_w_gate[l].astype(BF16), sh_w_up[l].astype(BF16), sh_w_down[l].astype(BF16), ln2,
                  t_lat=t_lat, tm=tm_moe, alpha=alpha)
    return xa[:, :t_lat]
```

```python
import functools

import jax
import jax.numpy as jnp
import numpy as np
from jax import lax
from jax.experimental import pallas as pl
from jax.experimental.pallas import tpu as pltpu

F32 = jnp.float32
BF16 = jnp.bfloat16
HIGHEST = lax.Precision.HIGHEST

GRID_W = 64
MLSTM_H, MLSTM_DH = 4, 64
MLSTM_W = MLSTM_H * MLSTM_DH
RWKV_H, RWKV_N = 4, 64
RWKV_W = RWKV_H * RWKV_N
DECAY_LORA, AAA_LORA, GATE_LORA = 64, 64, 128
RWKV_GN_EPS = 64e-5
MLA_H, MLA_NOPE, MLA_ROPE, MLA_V = 8, 64, 32, 64
Q_RANK, KV_RANK = 256, 128
ROPE_THETA = 10000.0
N_MLSTM_IN = 4 * MLSTM_W + 4 * MLSTM_H
N_RWKV_IN = 3 * RWKV_W + 2 * DECAY_LORA + 2 * AAA_LORA + GATE_LORA
N_EXPERTS, TOP_K, N_GROUPS, TOPK_GROUPS = 64, 8, 8, 4
ROUTED_SCALE = 2.5
LN_EPS = 1e-6
NEG = -1e30

C_QK, C_R, C_V, C_O, C_G, C_A = 0, 512, 1664, 1920, 2176, 2304
N_SHIFT = 1664
N_A = 640
N_EXT = C_A + N_A
HALO = 8
VMEM_LIMIT = 56 * 1024 * 1024


def _cp(sem):
    return pltpu.CompilerParams(dimension_semantics=sem, vmem_limit_bytes=VMEM_LIMIT)


def _ln(x):
    mu = jnp.mean(x, axis=-1, keepdims=True)
    xc = x - mu
    return xc * lax.rsqrt(jnp.mean(xc * xc, axis=-1, keepdims=True) + LN_EPS)


def _sigmoid(x):
    return 1.0 / (1.0 + jnp.exp(-x))


def _silu(x):
    return x * _sigmoid(x)


def _log_sigmoid(x):
    return jnp.minimum(x, 0.0) - jnp.log(1.0 + jnp.exp(-jnp.abs(x)))


def _softplus(x):
    return jnp.maximum(x, 0.0) + jnp.log(1.0 + jnp.exp(-jnp.abs(x)))


def _nt_dot(a, b, **kw):
    return lax.dot_general(a, b, (((1,), (1,)), ((), ())), preferred_element_type=F32, **kw)


def _tn_dot(a, b, **kw):
    return lax.dot_general(a, b, (((0,), (0,)), ((), ())), preferred_element_type=F32, **kw)


def _dot(a, b, **kw):
    return jnp.dot(a, b, preferred_element_type=F32, **kw)


def _mod_kernel(c_ref, w_ref, b_ref, o_ref):
    o_ref[0] = _dot(_silu(c_ref[...]), w_ref[0], precision=HIGHEST) + b_ref[0]


def _modulation(cc, w_mod, b_mod):
    n_layers, d, n = w_mod.shape
    tn = 1536
    return pl.pallas_call(
        _mod_kernel,
        grid=(n_layers, n // tn),
        in_specs=[pl.BlockSpec((8, d), lambda l, j: (0, 0)),
                  pl.BlockSpec((1, d, tn), lambda l, j: (l, 0, j)),
                  pl.BlockSpec((1, 1, tn), lambda l, j: (l, 0, j))],
        out_specs=pl.BlockSpec((1, 8, tn), lambda l, j: (l, 0, j)),
        out_shape=jax.ShapeDtypeStruct((n_layers, 8, n), F32),
        compiler_params=_cp(("parallel", "parallel")),
        name="modulation",
    )(cc, w_mod, b_mod.reshape(n_layers, 1, n))


def _in_kernel(xp_ref, x_ref, xn_ref, mod_ref, w_ref, conv_ref, mu_ref, zm_ref, zr_ref, za_ref, *, tm, nt_lat, nt):
    i = pl.program_id(1)
    has_prev = jnp.logical_and(i != 0, i != nt_lat)
    has_next = jnp.logical_and(i != nt_lat - 1, i != nt - 1)
    xt = jnp.concatenate([xp_ref[0], x_ref[0], xn_ref[0]], axis=0)
    sh = mod_ref[0, 0, 0:1, :]
    sc = mod_ref[0, 0, 1:2, :]
    h = (_ln(xt) * (1.0 + sc) + sh).astype(BF16)
    z = _dot(h, w_ref[...])
    rows = lax.broadcasted_iota(jnp.int32, (tm + 2 * HALO, 1), 0)
    lo = jnp.where(has_prev, 0, HALO)
    hi = jnp.where(has_next, tm + 2 * HALO, tm + HALO)
    keep = jnp.logical_and(rows >= lo, rows < hi)
    zs = jnp.where(keep, z[:, :N_SHIFT], 0.0)
    zc = zs[HALO:HALO + tm]
    zprev = pltpu.roll(zs, 1, axis=0)[HALO:HALO + tm]
    znext = pltpu.roll(zs, tm + 2 * HALO - 1, axis=0)[HALO:HALO + tm]
    cw = conv_ref[...]
    qk = (cw[0:1] * zprev[:, :C_R] + cw[1:2] * zc[:, :C_R] + cw[2:3] * znext[:, :C_R])
    qk = _silu(qk)
    lane = lax.broadcasted_iota(jnp.int32, (1, C_R), 1)
    qk = qk * jnp.where(lane < MLSTM_W, MLSTM_DH ** -0.5, 1.0)
    zm_ref[0, :, 0:512] = qk
    zm_ref[0, :, 512:1152] = z[HALO:HALO + tm, C_V:C_A]
    zr = zc[:, C_R:]
    zr_ref[0] = zr + mu_ref[...] * (0.5 * (zprev[:, C_R:] + znext[:, C_R:]) - zr)
    za_ref[0] = z[HALO:HALO + tm, C_A:]


def _in_proj(xa, modv, w_ext, conv, mu, *, t_lat, tm):
    b, ta, d = xa.shape
    nt, nt_lat = ta // tm, t_lat // tm
    nh = tm // HALO
    last = ta // HALO - 1
    kern = functools.partial(_in_kernel, tm=tm, nt_lat=nt_lat, nt=nt)
    return pl.pallas_call(
        kern,
        grid=(b, nt),
        in_specs=[pl.BlockSpec((1, HALO, d), lambda bi, i: (bi, jnp.maximum(i * nh - 1, 0), 0)),
                  pl.BlockSpec((1, tm, d), lambda bi, i: (bi, i, 0)),
                  pl.BlockSpec((1, HALO, d), lambda bi, i: (bi, jnp.minimum((i + 1) * nh, last), 0)),
                  pl.BlockSpec((1, 1, 6, d), lambda bi, i: (bi, i // nt_lat, 0, 0)),
                  pl.BlockSpec((d, N_EXT), lambda bi, i: (0, 0)),
                  pl.BlockSpec((3, C_R), lambda bi, i: (0, 0)),
                  pl.BlockSpec((1, N_RWKV_IN), lambda bi, i: (0, 0))],
        out_specs=[pl.BlockSpec((1, tm, 1152), lambda bi, i: (bi, i, 0)),
                   pl.BlockSpec((1, tm, N_RWKV_IN), lambda bi, i: (bi, i, 0)),
                   pl.BlockSpec((1, tm, N_A), lambda bi, i: (bi, i, 0))],
        out_shape=[jax.ShapeDtypeStruct((b, ta, 1152), F32),
                   jax.ShapeDtypeStruct((b, ta, N_RWKV_IN), F32),
                   jax.ShapeDtypeStruct((b, ta, N_A), F32)],
        compiler_params=_cp(("parallel", "parallel")),
        name="in_proj",
    )(xa, xa, xa, modv, w_ext, conv, mu)


def _mlstm_kernel(zm_ref, gc_ref, gr_ref, bc_ref, br_ref, o_ref, ct_ref, n_ref, m_ref, *, cl):
    d = pl.program_id(1)

    @pl.when(pl.program_id(2) == 0)
    def _():
        ct_ref[...] = jnp.zeros_like(ct_ref)
        n_ref[...] = jnp.zeros_like(n_ref)
        m_ref[...] = jnp.zeros_like(m_ref)

    sgn = 1 - 2 * d
    gc = gc_ref[0, 0] + bc_ref[0]
    gr = gr_ref[0, 0] + br_ref[0]
    lf_c = _log_sigmoid(gc)
    lf_r = _log_sigmoid(gr)
    ii = lax.broadcasted_iota(jnp.int32, (cl, cl), 0)
    jj = lax.broadcasted_iota(jnp.int32, (cl, cl), 1)
    mask = (jj - ii) * sgn <= 0
    mask_t = (ii - jj) * sgn <= 0
    cum_c = _dot(mask.astype(F32), lf_c, precision=HIGHEST)
    cum_r = _dot(lf_r, mask_t.astype(F32), precision=HIGHEST)
    outs = []
    for h in range(MLSTM_H):
        sl = slice(h * MLSTM_DH, (h + 1) * MLSTM_DH)
        q = zm_ref[0, :, sl]
        k = zm_ref[0, :, MLSTM_W + h * MLSTM_DH:MLSTM_W + (h + 1) * MLSTM_DH]
        v = zm_ref[0, :, 2 * MLSTM_W + h * MLSTM_DH:2 * MLSTM_W + (h + 1) * MLSTM_DH]
        qb, kb = q.astype(BF16), k.astype(BF16)
        li_c, li_r = gc[:, h:h + 1], gr[h:h + 1, :]
        cu_c, cu_r = cum_c[:, 4 + h:5 + h], cum_r[4 + h:5 + h, :]
        b_end = jnp.sum(lf_r[4 + h:5 + h, :], axis=1, keepdims=True)
        m_prev = m_ref[h:h + 1, 0:1]
        ct = ct_ref[h]
        nv = n_ref[h:h + 1, 0:MLSTM_DH]
        d_log = jnp.where(mask, cu_c - cu_r + li_r, NEG)
        g_log = cu_c + m_prev
        m_row = jnp.maximum(g_log, jnp.max(d_log, axis=1, keepdims=True))
        w_intra = jnp.exp(d_log - m_row) * _nt_dot(qb, kb)
        e_inter = jnp.exp(g_log - m_row)
        num = _dot(w_intra.astype(BF16), v.astype(BF16)) + e_inter * _dot(qb, ct.astype(BF16))
        den = jnp.sum(w_intra, axis=1, keepdims=True) + e_inter * jnp.sum(q * nv, axis=1, keepdims=True)
        outs.append(num / jnp.maximum(jnp.abs(den), jnp.exp(-m_row)))
        w_end = b_end - cu_c + li_c
        m_loc = jnp.max(w_end, axis=0, keepdims=True)
        e_end = jnp.exp(w_end - m_loc)
        m_new = jnp.maximum(b_end + m_prev, m_loc)
        a_old = jnp.exp(b_end + m_prev - m_new)
        a_loc = jnp.exp(m_loc - m_new)
        ct_ref[h] = a_old * ct + a_loc * _tn_dot(kb, (v * e_end).astype(BF16))
        n_ref[h:h + 1, 0:MLSTM_DH] = a_old * nv + a_loc * jnp.sum(k * e_end, axis=0, keepdims=True)
        m_ref[h:h + 1, :] = jnp.broadcast_to(m_new, (1, 128))
    o_ref[0, 0] = jnp.concatenate(outs, axis=1)


def _mlstm(zm, g_col, g_row, b_col, b_row, *, t_lat, cl):
    b, ta, _ = zm.shape
    nc, nc_lat = ta // cl, t_lat // cl
    nc_ctx = nc - nc_lat

    def chunk(di, c):
        fwd_idx = jnp.where(c < nc_ctx, nc_lat + c, c - nc_ctx)
        return jnp.where(di == 0, fwd_idx, nc - 1 - c)

    return pl.pallas_call(
        functools.partial(_mlstm_kernel, cl=cl),
        grid=(b, 2, nc),
        in_specs=[pl.BlockSpec((1, cl, 1152), lambda bi, di, c: (bi, chunk(di, c), 0)),
                  pl.BlockSpec((1, 1, cl, 8), lambda bi, di, c: (bi, di, chunk(di, c), 0)),
                  pl.BlockSpec((1, 1, 8, cl), lambda bi, di, c: (bi, di, 0, chunk(di, c))),
                  pl.BlockSpec((1, 1, 8), lambda bi, di, c: (di, 0, 0)),
                  pl.BlockSpec((1, 8, 1), lambda bi, di, c: (di, 0, 0))],
        out_specs=pl.BlockSpec((1, 1, cl, MLSTM_W), lambda bi, di, c: (bi, di, chunk(di, c), 0)),
        out_shape=jax.ShapeDtypeStruct((b, 2, ta, MLSTM_W), F32),
        scratch_shapes=[pltpu.VMEM((MLSTM_H, MLSTM_DH, MLSTM_DH), F32),
                        pltpu.VMEM((8, 128), F32),
                        pltpu.VMEM((8, 128), F32)],
        compiler_params=_cp(("parallel", "parallel", "arbitrary")),
        name="mlstm",
    )(zm, g_col, g_row, b_col, b_row)


def _rwkv_prep_kernel(zr_ref, wup_ref, aup_ref, gup_ref, vec_ref, hsum_ref, pd_ref, v_ref, g_ref, bonus_ref):
    nb, tm = zr_ref.shape[0], zr_ref.shape[1]
    nbh = nb * RWKV_H
    vec = vec_ref[...]
    hsum = hsum_ref[...]
    zero = jnp.zeros((tm, RWKV_N), F32)
    for bi in range(nb):
        z = zr_ref[bi]
        zr_, zk, zv = z[:, 0:256], z[:, 256:512], z[:, 512:768]
        zw, za, zg = z[:, 768:896], z[:, 896:1024], z[:, 1024:1152]
        w_raw = _dot(jnp.tanh(zw).astype(BF16), wup_ref[...])
        a_raw = _dot(za.astype(BF16), aup_ref[...])
        g_ref[bi] = _dot(_sigmoid(zg).astype(BF16), gup_ref[...])
        kk = zk * vec[4:5]
        kk_ss = _dot(kk * kk, hsum, precision=HIGHEST)
        kk = kk * lax.rsqrt(jnp.maximum(kk_ss, 1e-12))
        k_sum = jnp.zeros_like(zk)
        for di in range(2):
            sl = slice(di * RWKV_W, (di + 1) * RWKV_W)
            decay = jnp.exp(-jnp.exp(-_softplus(-(vec[di:di + 1] + w_raw[:, sl])) - 0.5))
            a = _sigmoid(vec[2 + di:3 + di] + a_raw[:, sl])
            k_dir = zk * (1.0 + (a - 1.0) * vec[5:6])
            k_sum = k_sum + k_dir
            ka = kk * a
            for h in range(RWKV_H):
                hs = slice(h * RWKV_N, (h + 1) * RWKV_N)
                rows = pl.ds(bi * RWKV_H + h, tm, stride=nbh)
                pd_ref[di, 0, rows, :] = jnp.concatenate([decay[:, hs], ka[:, hs]], axis=1)
                pd_ref[di, 1, rows, :] = jnp.concatenate([k_dir[:, hs], kk[:, hs]], axis=1)
                pd_ref[di, 2, rows, :] = jnp.concatenate([zr_[:, hs], zero], axis=1)
        v_ref[bi] = zv
        bonus_ref[bi] = _dot(zr_ * vec[6:7] * k_sum, hsum, precision=HIGHEST) * zv


def _rwkv_prep(zr, wup, aup, gup, vec, hsum, *, tm):
    b, ta, _ = zr.shape
    nbh = b * RWKV_H
    tok = pl.BlockSpec((b, tm, RWKV_W), lambda i: (0, i, 0))
    full = lambda a: pl.BlockSpec(a.shape, lambda i: (0,) * a.ndim)
    return pl.pallas_call(
        _rwkv_prep_kernel,
        grid=(ta // tm,),
        in_specs=[pl.BlockSpec((b, tm, N_RWKV_IN), lambda i: (0, i, 0)),
                  full(wup), full(aup), full(gup), full(vec), full(hsum)],
        out_specs=[pl.BlockSpec((2, 3, tm * nbh, 128), lambda i: (0, 0, i, 0)), tok, tok, tok],
        out_shape=[jax.ShapeDtypeStruct((2, 3, ta * nbh, 128), F32)] + [jax.ShapeDtypeStruct((b, ta, RWKV_W), F32)] * 3,
        compiler_params=_cp(("parallel",)),
        name="rwkv_prep",
    )(zr, wup, aup, gup, vec, hsum)


def _rwkv_scan_kernel(pf_ref, pb_ref, pat_ref, vf_ref, vb_ref, of_ref, ob_ref, st_ref, x_ref, tt_ref, *, ts, nvh, nbh):
    ng = ts // 8
    rows = 8 * nbh

    @pl.when(pl.program_id(0) == 0)
    def _():
        st_ref[...] = jnp.zeros_like(st_ref)

    def transpose(di, p_ref, grp):
        sl = pl.ds(pl.multiple_of(grp * rows, rows), rows)
        for j in range(3):
            tt_ref[di, j] = p_ref[0, j, sl, :].T

    def update(di, u, v_ref, o_ref, t):
        pat = pat_ref[u]
        for j, (tj, half) in enumerate(((0, 0), (0, 1), (1, 0), (1, 1), (2, 0))):
            x_ref[di, j] = jnp.take_along_axis(tt_ref[di, tj, half * RWKV_N:(half + 1) * RWKV_N, :], pat, axis=1)
        vt = v_ref[t]
        outs = []
        for vh in range(nvh):
            sv = st_ref[di, vh]
            sa = jnp.sum(sv * x_ref[di, 3], axis=0, keepdims=True)
            sv = sv * x_ref[di, 0] - sa * x_ref[di, 1] + vt[vh:vh + 1, :] * x_ref[di, 2]
            st_ref[di, vh] = sv
            outs.append(jnp.sum(sv * x_ref[di, 4], axis=0, keepdims=True))
        o_ref[t] = jnp.concatenate(outs, axis=0)

    def group(g, carry):
        transpose(0, pf_ref, g)
        transpose(1, pb_ref, ng - 1 - g)
        for u in range(8):
            update(0, u, vf_ref, of_ref, g * 8 + u)
            update(1, 7 - u, vb_ref, ob_ref, ts - 1 - (g * 8 + u))
        return carry

    lax.fori_loop(0, ng, group, 0)


def _rwkv_scan(pd, pats, vs, *, t_lat, ts):
    ta, nvh, nl = vs.shape
    nbh = pd.shape[2] // ta
    nt, nt_lat = ta // ts, t_lat // ts
    fwd = lambda i: (i + nt_lat) % nt
    bwd = lambda i: nt - 1 - i
    return pl.pallas_call(
        functools.partial(_rwkv_scan_kernel, ts=ts, nvh=nvh, nbh=nbh),
        grid=(nt,),
        in_specs=[pl.BlockSpec((1, 3, ts * nbh, 128), lambda i: (0, 0, fwd(i), 0)),
                  pl.BlockSpec((1, 3, ts * nbh, 128), lambda i: (1, 0, bwd(i), 0)),
                  pl.BlockSpec(pats.shape, lambda i: (0, 0, 0)),
                  pl.BlockSpec((ts, nvh, nl), lambda i: (fwd(i), 0, 0)),
                  pl.BlockSpec((ts, nvh, nl), lambda i: (bwd(i), 0, 0))],
        out_specs=[pl.BlockSpec((ts, nvh, nl), lambda i: (fwd(i), 0, 0)),
                   pl.BlockSpec((ts, nvh, nl), lambda i: (bwd(i), 0, 0))],
        out_shape=[jax.ShapeDtypeStruct((ta, nvh, nl), F32)] * 2,
        scratch_shapes=[pltpu.VMEM((2, nvh, RWKV_N, nl), F32), pltpu.VMEM((2, 5, RWKV_N, nl), F32),
                        pltpu.VMEM((2, 3, 128, 8 * nbh), F32)],
        compiler_params=_cp(("arbitrary",)),
        name="rwkv_scan",
    )(pd, pd, pats, vs, vs)


def _mla_proj_kernel(za_ref, cos_ref, sin_ref, nq_ref, nkv_ref, wq_ref, wqr_ref, wk_ref, wv_ref, q_ref, k_ref, v_ref):
    za = za_ref[0]
    cq, ckv = za[:, 0:Q_RANK], za[:, Q_RANK:Q_RANK + KV_RANK]
    kr, krr = za[:, 384:512], za[:, 512:640]
    cos, sin = cos_ref[...], sin_ref[...]
    cqn = (cq * lax.rsqrt(jnp.mean(cq * cq, axis=-1, keepdims=True) + 1e-6) * nq_ref[...]).astype(BF16)
    ckvn = (ckv * lax.rsqrt(jnp.mean(ckv * ckv, axis=-1, keepdims=True) + 1e-6) * nkv_ref[...]).astype(BF16)
    q = _dot(cqn, wq_ref[...])
    qr = _dot(cqn, wqr_ref[...])
    kn = _dot(ckvn, wk_ref[...])
    v_ref[0] = _nt_dot(wv_ref[...], ckvn).astype(BF16)
    k_rope = kr * cos + krr * sin
    scale = (MLA_NOPE + MLA_ROPE) ** -0.5 * np.log2(np.e)
    for h in range(MLA_H):
        sl = slice(h * 128, (h + 1) * 128)
        q_ref[0, :, sl] = ((q[:, sl] * cos + qr[:, sl] * sin) * scale).astype(BF16)
        k_ref[0, :, sl] = (kn[:, sl] + k_rope).astype(BF16)


def _mla_proj(za, cos, sin, nq, nkv, wq, wqr, wk, wv, *, tm):
    b, ta, _ = za.shape
    full = lambda a: pl.BlockSpec(a.shape, lambda bi, i: (0,) * a.ndim)
    return pl.pallas_call(
        _mla_proj_kernel,
        grid=(b, ta // tm),
        in_specs=[pl.BlockSpec((1, tm, N_A), lambda bi, i: (bi, i, 0)),
                  pl.BlockSpec((tm, 128), lambda bi, i: (i, 0)),
                  pl.BlockSpec((tm, 128), lambda bi, i: (i, 0)),
                  full(nq), full(nkv), full(wq), full(wqr), full(wk), full(wv)],
        out_specs=[pl.BlockSpec((1, tm, 1024), lambda bi, i: (bi, i, 0)),
                   pl.BlockSpec((1, tm, 1024), lambda bi, i: (bi, i, 0)),
                   pl.BlockSpec((1, 512, tm), lambda bi, i: (bi, 0, i))],
        out_shape=[jax.ShapeDtypeStruct((b, ta, 1024), BF16),
                   jax.ShapeDtypeStruct((b, ta, 1024), BF16),
                   jax.ShapeDtypeStruct((b, 512, ta), BF16)],
        compiler_params=_cp(("parallel", "parallel")),
        name="mla_proj",
    )(za, cos, sin, nq, nkv, wq, wqr, wk, wv)


def _attn_kernel(q_ref, k_ref, vt_ref, o_ref):
    outs = []
    for hh in range(2):
        sl = slice(hh * 128, (hh + 1) * 128)
        st = _nt_dot(k_ref[0, :, sl], q_ref[0, :, sl])
        m = jnp.max(st, axis=0, keepdims=True)
        p = jnp.exp2(st - m)
        l = jnp.sum(p, axis=0, keepdims=True)
        ot = _dot(vt_ref[0], p.astype(BF16))
        outs.append(ot[hh * MLA_V:(hh + 1) * MLA_V] / l)
    o_ref[0] = jnp.concatenate(outs, axis=0).T


def _attention(q, k, vt, *, q_off, n_q, k_off, n_keys, tq):
    b = q.shape[0]
    qo, ko = q_off // tq, k_off // n_keys
    return pl.pallas_call(
        _attn_kernel,
        grid=(b, MLA_H // 2, n_q // tq),
        in_specs=[pl.BlockSpec((1, tq, 256), lambda bi, hp, i: (bi, qo + i, hp)),
                  pl.BlockSpec((1, n_keys, 256), lambda bi, hp, i: (bi, ko, hp)),
                  pl.BlockSpec((1, 128, n_keys), lambda bi, hp, i: (bi, hp, ko))],
        out_specs=pl.BlockSpec((1, tq, 128), lambda bi, hp, i: (bi, i, hp)),
        out_shape=jax.ShapeDtypeStruct((b, n_q, 512), F32),
        compiler_params=_cp(("parallel", "parallel", "parallel")),
        name="mla_attention",
    )(q, k, vt)


def _out_kernel(x_ref, hf_ref, hb_ref, zm_ref, of_ref, ob_ref, g_ref, bonus_ref, att_ref, mod_ref, vec_ref,
                hmean_ref, wo_ref, ln_ref, o_ref, *, alpha):
    hmean = hmean_ref[...]

    def head_norm(y, eps):
        mu = _dot(y, hmean, precision=HIGHEST)
        yc = y - mu
        return yc * lax.rsqrt(_dot(yc * yc, hmean, precision=HIGHEST) + eps)

    vec = vec_ref[...]
    zo = zm_ref[0, :, 768:1024]
    m_mix = _sigmoid(zo) * (head_norm(hf_ref[0, 0] + hb_ref[0, 0], 1e-6) * vec[0:1])
    r_o = head_norm(of_ref[0, 0] + ob_ref[0, 0], RWKV_GN_EPS) * vec[1:2] + vec[2:3] + bonus_ref[0]
    r_mix = r_o * g_ref[0]
    mix = jnp.concatenate([m_mix, r_mix, att_ref[0]], axis=1).astype(BF16)
    y = _dot(mix, wo_ref[...])
    gate = mod_ref[0, 0, 2:3, :]
    o_ref[0] = _ln(alpha * x_ref[0] + gate * y) * ln_ref[0:1] + ln_ref[1:2]


def _out_proj(xa, h_m, zm, o_r, g, bonus, att, modv, vec, hmean, wo, ln, *, t_lat, tm, alpha):
    b, ta, d = xa.shape
    nt_lat = t_lat // tm
    tok = lambda w: pl.BlockSpec((1, tm, w), lambda bi, i: (bi, i, 0))
    full = lambda a: pl.BlockSpec(a.shape, lambda bi, i: (0,) * a.ndim)
    return pl.pallas_call(
        functools.partial(_out_kernel, alpha=alpha),
        grid=(b, ta // tm),
        in_specs=[tok(d),
                  pl.BlockSpec((1, 1, tm, MLSTM_W), lambda bi, i: (bi, 0, i, 0)),
                  pl.BlockSpec((1, 1, tm, MLSTM_W), lambda bi, i: (bi, 1, i, 0)),
                  tok(1152),
                  pl.BlockSpec((1, 1, tm, RWKV_W), lambda bi, i: (0, bi, i, 0)),
                  pl.BlockSpec((1, 1, tm, RWKV_W), lambda bi, i: (1, bi, i, 0)),
                  tok(RWKV_W), tok(RWKV_W), tok(512),
                  pl.BlockSpec((1, 1, 6, d), lambda bi, i: (bi, i // nt_lat, 0, 0)),
                  full(vec), full(hmean), full(wo), full(ln)],
        out_specs=tok(d),
        out_shape=jax.ShapeDtypeStruct((b, ta, d), F32),
        compiler_params=_cp(("parallel", "parallel")),
        name="out_proj",
    )(xa, h_m, h_m, zm, o_r, o_r, g, bonus, att, modv, vec, hmean, wo, ln)


def _first_max(x, idx, axis):
    mx = jnp.max(x, axis=axis, keepdims=True)
    first = jnp.min(jnp.where(x == mx, idx, 1 << 20), axis=axis, keepdims=True)
    return mx, first, idx == first


def _router_kernel(x_ref, mod_ref, w_ref, bias_ref, id_ref, gate_ref, h_ref):
    tm = x_ref.shape[1]
    sh, sc = mod_ref[0, 0, 3:4, :], mod_ref[0, 0, 4:5, :]
    h = _ln(x_ref[0]) * (1.0 + sc) + sh
    for j in range(h.shape[1] // 128):
        h_ref[0, pl.ds(j, tm, stride=h.shape[1] // 128), :] = h[:, j * 128:(j + 1) * 128]
    scores = _sigmoid(_nt_dot(w_ref[...], h, precision=HIGHEST))
    biased = scores + bias_ref[...]
    gsz = N_EXPERTS // N_GROUPS
    b3 = biased.reshape(N_GROUPS, gsz, tm)
    s3 = scores.reshape(N_GROUPS, gsz, tm)
    e_idx = lax.broadcasted_iota(jnp.int32, (N_GROUPS, gsz, tm), 1)
    m1, _, hit = _first_max(b3, e_idx, 1)
    m2 = jnp.max(jnp.where(hit, -jnp.inf, b3), axis=1, keepdims=True)
    gscore = m1 + m2
    g_idx = lax.broadcasted_iota(jnp.int32, (N_GROUPS, 1, tm), 0)
    gsel = jnp.zeros((N_GROUPS, 1, tm), F32)
    for _ in range(TOPK_GROUPS):
        _, _, hit = _first_max(jnp.where(gsel > 0, -jnp.inf, gscore), g_idx, 0)
        gsel = jnp.where(hit, 1.0, gsel)
    cand = jnp.where(jnp.broadcast_to(gsel, b3.shape) > 0, b3, -jnp.inf)
    x_idx = g_idx * gsz + e_idx
    sel = jnp.zeros(b3.shape, F32)
    ids, picked = [], []
    for _ in range(TOP_K):
        _, first, hit = _first_max(jnp.where(sel > 0, -jnp.inf, cand), x_idx, (0, 1))
        sel = jnp.where(hit, 1.0, sel)
        ids.append(first[0])
        picked.append(jnp.sum(jnp.where(hit, s3, 0.0), axis=(0, 1), keepdims=True)[0])
    picked = jnp.concatenate(picked, axis=0)
    id_ref[0] = jnp.concatenate(ids, axis=0)
    gate_ref[0] = ROUTED_SCALE * picked / jnp.sum(picked, axis=0, keepdims=True)


def _router(xa, modv, w_t, bias, *, t_lat, tm):
    b, ta, d = xa.shape
    nt_lat = t_lat // tm
    return pl.pallas_call(
        _router_kernel,
        grid=(b, ta // tm),
        in_specs=[pl.BlockSpec((1, tm, d), lambda bi, i: (bi, i, 0)),
                  pl.BlockSpec((1, 1, 6, d), lambda bi, i: (bi, i // nt_lat, 0, 0)),
                  pl.BlockSpec((N_EXPERTS, d), lambda bi, i: (0, 0)),
                  pl.BlockSpec((N_EXPERTS, 1), lambda bi, i: (0, 0))],
        out_specs=[pl.BlockSpec((1, TOP_K, tm), lambda bi, i: (bi, 0, i)),
                   pl.BlockSpec((1, TOP_K, tm), lambda bi, i: (bi, 0, i)),
                   pl.BlockSpec((1, tm * (d // 128), 128), lambda bi, i: (bi, i, 0))],
        out_shape=[jax.ShapeDtypeStruct((b, TOP_K, ta), jnp.int32),
                   jax.ShapeDtypeStruct((b, TOP_K, ta), F32),
                   jax.ShapeDtypeStruct((b, ta * (d // 128), 128), F32)],
        compiler_params=_cp(("parallel", "parallel")),
        name="router",
    )(xa, modv, w_t, bias)


MOE_RB = 256
MOE_PITCH = MOE_RB + 8
MOE_U = 8


def _moe_routed_kernel(vblk_ref, vexp_ref, vlo_ref, vhi_ref, vvalid_ref, vfirst_ref,
                       tok_ref, gate_ref, src_ref, wg_ref, wu_ref, wd_ref, acc_ref, xt_ref, yt_ref, *, nv, ta, nchunk):
    bi, v = pl.program_id(0), pl.program_id(1)
    pos = bi * nv + v
    lo, hi = vlo_ref[pos], vhi_ref[pos]

    @pl.when(v == 0)
    def _():
        acc_ref[...] = jnp.zeros_like(acc_ref)

    @pl.when(vfirst_ref[pos] == 1)
    def _():
        for r in range(MOE_RB):
            off = pl.multiple_of(tok_ref[0, 0, 0, r] * nchunk, nchunk)
            xt_ref[pl.ds(r, nchunk, stride=MOE_PITCH), :] = src_ref[0, pl.ds(off, nchunk), :]

    @pl.when(vvalid_ref[pos] == 1)
    def _():
        x = jnp.concatenate([xt_ref[j * MOE_PITCH:j * MOE_PITCH + MOE_RB, :] for j in range(nchunk)], axis=1)
        x = x.astype(BF16)
        ii = lax.broadcasted_iota(jnp.int32, (MOE_RB, MOE_RB), 0)
        jj = lax.broadcasted_iota(jnp.int32, (MOE_RB, MOE_RB), 1)
        gcol = jnp.sum(jnp.where(ii == jj, gate_ref[0, 0], 0.0), axis=1, keepdims=True)
        row = lax.broadcasted_iota(jnp.int32, (MOE_RB, 1), 0)
        gcol = jnp.where(jnp.logical_and(row >= lo, row < hi), gcol, 0.0)
        a = _silu(_dot(x, wg_ref[0])) * _dot(x, wu_ref[0]) * gcol
        y = _dot(a.astype(BF16), wd_ref[0])
        for j in range(nchunk):
            yt_ref[j * MOE_PITCH:j * MOE_PITCH + MOE_RB, :] = y[:, j * 128:(j + 1) * 128]
        for g in range(MOE_RB // MOE_U):
            pending = []
            for u in range(MOE_U):
                r = g * MOE_U + u
                ok = jnp.logical_and(r >= lo, r < hi)
                off = pl.multiple_of(jnp.where(ok, tok_ref[0, 0, 0, r], ta + u) * nchunk, nchunk)
                pending.append((off, acc_ref[0, pl.ds(off, nchunk), :] + yt_ref[pl.ds(r, nchunk, stride=MOE_PITCH), :]))
            for off, val in pending:
                acc_ref[0, pl.ds(off, nchunk), :] = val


def _moe_routed(h_rows, tok, gate, tables, wg, wu, wd, *, nv):
    b, rows, _ = h_rows.shape
    d, de = wg.shape[1], wg.shape[2]
    nchunk = d // 128
    ta = rows // nchunk
    nblk = tok.shape[1] // MOE_RB
    blk = lambda bi, v, vblk, *_: (bi, vblk[bi * nv + v], 0, 0)
    exp = lambda bi, v, vblk, vexp, *_: (vexp[bi * nv + v], 0, 0)
    grid_spec = pltpu.PrefetchScalarGridSpec(
        num_scalar_prefetch=6,
        grid=(b, nv),
        in_specs=[pl.BlockSpec((1, 1, 1, MOE_RB), blk, memory_space=pltpu.SMEM),
                  pl.BlockSpec((1, 1, 1, MOE_RB), blk),
                  pl.BlockSpec((1, rows, 128), lambda bi, v, *_: (bi, 0, 0), pipeline_mode=pl.Buffered(1)),
                  pl.BlockSpec((1, d, de), exp),
                  pl.BlockSpec((1, d, de), exp),
                  pl.BlockSpec((1, de, d), exp)],
        out_specs=pl.BlockSpec((1, rows + MOE_U * nchunk, 128), lambda bi, v, *_: (bi, 0, 0),
                               pipeline_mode=pl.Buffered(1)),
        scratch_shapes=[pltpu.VMEM((nchunk * MOE_PITCH, 128), F32), pltpu.VMEM((nchunk * MOE_PITCH, 128), F32)],
    )
    return pl.pallas_call(
        functools.partial(_moe_routed_kernel, nv=nv, ta=ta, nchunk=nchunk),
        grid_spec=grid_spec,
        out_shape=jax.ShapeDtypeStruct((b, rows + MOE_U * nchunk, 128), F32),
        compiler_params=_cp(("parallel", "arbitrary")),
        name="moe_routed",
    )(*tables, tok.reshape(b, nblk, 1, MOE_RB), gate.reshape(b, nblk, 1, MOE_RB), h_rows, wg, wu, wd)


def _moe_tables(ids, gates, *, ta, nv):
    b = ids.shape[0]
    n = TOP_K * ta
    nblk = n // MOE_RB
    t_idx = jnp.broadcast_to(jnp.arange(ta, dtype=jnp.int32), (b, TOP_K, ta))
    keys, g_sorted = lax.sort(((ids * ta + t_idx).reshape(b, n), gates.reshape(b, n)), dimension=1, num_keys=1)
    tok, e_sorted = keys % ta, keys // ta
    ends = jnp.sum(e_sorted[:, None, :] <= jnp.arange(N_EXPERTS, dtype=jnp.int32)[None, :, None], axis=2)
    starts = ends - jnp.sum(e_sorted[:, None, :] == jnp.arange(N_EXPERTS, dtype=jnp.int32)[None, :, None], axis=2)
    e_lo, e_hi = e_sorted[:, ::MOE_RB], e_sorted[:, MOE_RB - 1::MOE_RB]
    n_vis = e_hi - e_lo + 1
    v_end = jnp.cumsum(n_vis, axis=1)
    v_start = v_end - n_vis
    v = jnp.arange(nv, dtype=jnp.int32)
    blk = jnp.minimum(jnp.sum(v_end[:, None, :] <= v[None, :, None], axis=2), nblk - 1).astype(jnp.int32)
    take = lambda a, i: jnp.take_along_axis(a, i, axis=1)
    valid = v[None, :] < v_end[:, -1:]
    exp = jnp.where(valid, take(e_lo, blk) + v[None, :] - take(v_start, blk), take(e_hi, blk)).astype(jnp.int32)
    lo = jnp.clip(take(starts, exp) - blk * MOE_RB, 0, MOE_RB)
    hi = jnp.where(valid, jnp.clip(take(ends, exp) - blk * MOE_RB, 0, MOE_RB), 0)
    first = jnp.logical_and(valid, v[None, :] == take(v_start, blk))
    flat = lambda a: a.astype(jnp.int32).reshape(-1)
    return tok.astype(jnp.int32), g_sorted, tuple(flat(a) for a in (blk, exp, lo, hi, valid, first))


def _moe_finish_kernel(x_ref, r_ref, mod_ref, sg_ref, su_ref, sd_ref, ln_ref, o_ref, *, alpha):
    tm, d = x_ref.shape[1], x_ref.shape[2]
    nchunk = d // 128
    x = x_ref[0]
    h = (_ln(x) * (1.0 + mod_ref[0, 0, 4:5, :]) + mod_ref[0, 0, 3:4, :]).astype(BF16)
    shared = _dot((_silu(_dot(h, sg_ref[...])) * _dot(h, su_ref[...])).astype(BF16), sd_ref[...])
    routed = jnp.concatenate([r_ref[0, pl.ds(j, tm, stride=nchunk), :] for j in range(nchunk)], axis=1)
    o_ref[0] = _ln(alpha * x + mod_ref[0, 0, 5:6, :] * (routed + shared)) * ln_ref[0:1] + ln_ref[1:2]


def _moe_finish(xa, routed, modv, sg, su, sd, ln, *, t_lat, tm, alpha):
    b, ta, d = xa.shape
    nt_lat = t_lat // tm
    full = lambda a: pl.BlockSpec(a.shape, lambda bi, i: (0,) * a.ndim)
    return pl.pallas_call(
        functools.partial(_moe_finish_kernel, alpha=alpha),
        grid=(b, ta // tm),
        in_specs=[pl.BlockSpec((1, tm, d), lambda bi, i: (bi, i, 0)),
                  pl.BlockSpec((1, tm * (d // 128), 128), lambda bi, i: (bi, i, 0)),
                  pl.BlockSpec((1, 1, 6, d), lambda bi, i: (bi, i // nt_lat, 0, 0)),
                  full(sg), full(su), full(sd), full(ln)],
        out_specs=pl.BlockSpec((1, tm, d), lambda bi, i: (bi, i, 0)),
        out_shape=jax.ShapeDtypeStruct((b, ta, d), F32),
        compiler_params=_cp(("parallel", "parallel")),
        name="moe_finish",
    )(xa, routed, modv, sg, su, sd, ln)


def _rope_blocks(w_rope):
    k = w_rope.shape[0]
    ev, od = w_rope[:, 0::2], w_rope[:, 1::2]
    z64, z32 = jnp.zeros((k, 64), w_rope.dtype), jnp.zeros((k, 32), w_rope.dtype)
    return (jnp.concatenate([z64, ev, od, z32], 1), jnp.concatenate([z64, -od, ev, z32], 1))


def _layer_weights(l, w_in, mla_w_uq, mla_w_ukv):
    w = w_in[l]
    d = w.shape[0]
    wm, wr, wa = w[:, :N_MLSTM_IN], w[:, N_MLSTM_IN:N_MLSTM_IN + N_RWKV_IN], w[:, N_MLSTM_IN + N_RWKV_IN:]
    kr, krr = _rope_blocks(wa[:, Q_RANK + KV_RANK:])
    w_ext = jnp.concatenate([
        wm[:, :2 * MLSTM_W], wr, wm[:, 2 * MLSTM_W:4 * MLSTM_W],
        wm[:, 4 * MLSTM_W:], jnp.zeros((d, 128 - 4 * MLSTM_H), w.dtype),
        wa[:, :Q_RANK + KV_RANK], kr, krr], axis=1).astype(BF16)
    uq = mla_w_uq[l].reshape(Q_RANK, MLA_H, MLA_NOPE + MLA_ROPE)
    wq, wqr = [], []
    for h in range(MLA_H):
        r0, r1 = _rope_blocks(uq[:, h, MLA_NOPE:])
        wq.append(r0.at[:, :MLA_NOPE].set(uq[:, h, :MLA_NOPE]))
        wqr.append(r1)
    ukv = mla_w_ukv[l].reshape(KV_RANK, MLA_H, MLA_NOPE + MLA_V)
    wk = jnp.concatenate([ukv[:, :, :MLA_NOPE], jnp.zeros((KV_RANK, MLA_H, 64), F32)], axis=2).reshape(KV_RANK, MLA_H * 128)
    wv = ukv[:, :, MLA_NOPE:].reshape(KV_RANK, MLA_H * MLA_V)
    return w_ext, jnp.concatenate(wq, 1).astype(BF16), jnp.concatenate(wqr, 1).astype(BF16), wk.astype(BF16), wv.T.astype(BF16)


def _rope_tables(t_lat, n_ctx):
    rows = t_lat // GRID_W
    row = jnp.repeat(jnp.arange(rows), GRID_W).astype(F32)
    col = jnp.tile(jnp.arange(GRID_W), rows).astype(F32)
    n_freq = MLA_ROPE // 4
    freq = ROPE_THETA ** (-jnp.arange(n_freq, dtype=F32) / n_freq)
    ang = jnp.concatenate([row[:, None] * freq, col[:, None] * freq], -1)
    cos, sin = jnp.cos(ang), jnp.sin(ang)
    one, zero = jnp.ones((t_lat, 64), F32), jnp.zeros((t_lat, 32), F32)
    cos_l = jnp.concatenate([one, cos, cos, zero], 1)
    sin_l = jnp.concatenate([0 * one, sin, sin, zero], 1)
    cos_c = jnp.concatenate([jnp.ones((n_ctx, 96), F32), jnp.zeros((n_ctx, 32), F32)], 1)
    return jnp.concatenate([cos_l, cos_c], 0), jnp.concatenate([sin_l, jnp.zeros((n_ctx, 128), F32)], 0)


def kernel(x, c, ctx, c_ctx, w_mod, b_mod, w_in, mlstm_conv, mlstm_gate_bias, mlstm_norm_w, rwkv_mu, rwkv_w0, rwkv_w_up, rwkv_a0, rwkv_a_up, rwkv_g_up, rwkv_k_k, rwkv_k_a, rwkv_r_k, rwkv_ln_w, rwkv_ln_b, mla_q_norm, mla_kv_norm, mla_w_uq, mla_w_ukv, w_out, ln1_w, ln1_b, router_w, router_bias, exp_w_gate, exp_w_up, exp_w_down, sh_w_gate, sh_w_up, sh_w_down, ln2_w, ln2_b):
    b, t_lat, d = x.shape
    n_ctx = ctx.shape[1]
    ta = t_lat + n_ctx
    depth = w_in.shape[0]
    alpha = (2 * depth) ** 0.25
    tm = 256
    cl = 256
    nv_moe = TOP_K * ta // MOE_RB + N_EXPERTS
    assert b + 1 <= 8 and n_ctx % tm == 0 and t_lat % tm == 0 and (TOP_K * ta) % MOE_RB == 0

    cc = jnp.concatenate([c, c_ctx[None], jnp.zeros((8 - b - 1, d), F32)], 0)
    mods = _modulation(cc, w_mod, b_mod)
    cos, sin = _rope_tables(t_lat, n_ctx)
    hsum = jnp.kron(jnp.eye(RWKV_H, dtype=F32), jnp.ones((RWKV_N, RWKV_N), F32))
    hmean = hsum / RWKV_N
    lane = jnp.arange(128, dtype=jnp.int32) // (128 // (b * RWKV_H))
    pats = jnp.broadcast_to((jnp.arange(8, dtype=jnp.int32) * (b * RWKV_H))[:, None, None] + lane, (8, RWKV_N, 128))

    xa = jnp.concatenate([x, ctx], axis=1)
    for l in range(depth):
        m_lat = mods[l, :b].reshape(b, 1, 6, d)
        m_ctx = jnp.broadcast_to(mods[l, b].reshape(1, 1, 6, d), (b, 1, 6, d))
        modv = jnp.concatenate([m_lat, m_ctx], axis=1)
        w_ext, wq, wqr, wk, wv = _layer_weights(l, w_in, mla_w_uq, mla_w_ukv)
        mu = rwkv_mu[l].reshape(1, N_RWKV_IN)
        zm, zr, za = _in_proj(xa, modv, w_ext, mlstm_conv[l], mu, t_lat=t_lat, tm=tm)

        gates = zm[:, :, 1024:1024 + 4 * MLSTM_H].reshape(b, ta, 2, 8)
        g_col = gates.transpose(0, 2, 1, 3)
        g_row = gates.transpose(0, 2, 3, 1)
        gb = mlstm_gate_bias[l].reshape(2, 8)
        h_m = _mlstm(zm, g_col, g_row, gb.reshape(2, 1, 8), gb.reshape(2, 8, 1), t_lat=t_lat, cl=cl)

        zeros = jnp.zeros((DECAY_LORA, RWKV_W), F32)
        wup = jnp.concatenate([jnp.concatenate([rwkv_w_up[l, 0], zeros], 1),
                               jnp.concatenate([zeros, rwkv_w_up[l, 1]], 1)], 0).astype(BF16)
        aup = jnp.concatenate([jnp.concatenate([rwkv_a_up[l, 0], zeros], 1),
                               jnp.concatenate([zeros, rwkv_a_up[l, 1]], 1)], 0).astype(BF16)
        vec = jnp.concatenate([rwkv_w0[l], rwkv_a0[l], rwkv_k_k[l][None], rwkv_k_a[l][None], rwkv_r_k[l][None],
                               jnp.zeros((1, RWKV_W), F32)], 0)
        pd, v_tok, g_tok, bonus = _rwkv_prep(zr, wup, aup, rwkv_g_up[l].astype(BF16), vec, hsum, tm=128)
        rep = 128 // (b * RWKV_H)
        nvh = RWKV_N // rep
        vs = v_tok.reshape(b, ta, RWKV_H, nvh, rep).transpose(1, 3, 0, 2, 4).reshape(ta, nvh, 128)
        o_f, o_b = _rwkv_scan(pd, pats, vs, t_lat=t_lat, ts=32)
        o_s = jnp.stack([o_f, o_b], 0).reshape(2, ta, nvh, b, RWKV_H, rep)
        o_s = o_s.transpose(0, 3, 1, 4, 2, 5).reshape(2, b, ta, RWKV_W)

        q, k, vt = _mla_proj(za, cos, sin, mla_q_norm[l][None], mla_kv_norm[l][None], wq, wqr, wk, wv, tm=tm)
        att_l = _attention(q, k, vt, q_off=0, n_q=t_lat, k_off=0, n_keys=ta, tq=tm)
        att_c = _attention(q, k, vt, q_off=t_lat, n_q=n_ctx, k_off=t_lat, n_keys=n_ctx, tq=tm)
        att = jnp.concatenate([att_l, att_c], axis=1)

        vec_o = jnp.concatenate([mlstm_norm_w[l][None], rwkv_ln_w[l][None], rwkv_ln_b[l][None],
                                 jnp.zeros((5, RWKV_W), F32)], 0)
        ln1 = jnp.stack([ln1_w[l], ln1_b[l]], 0)
        xa = _out_proj(xa, h_m, zm, o_s, g_tok, bonus, att, modv, vec_o, hmean, w_out[l].astype(BF16), ln1,
                       t_lat=t_lat, tm=tm, alpha=alpha)

        ids, gts, h_rows = _router(xa, modv, router_w[l].T, router_bias[l].reshape(N_EXPERTS, 1), t_lat=t_lat, tm=tm)
        tok, g_sorted, tables = _moe_tables(ids, gts, ta=ta, nv=nv_moe)
        routed = _moe_routed(h_rows, tok, g_sorted, tables, exp_w_gate[l].astype(BF16), exp_w_up[l].astype(BF16),
                             exp_w_down[l].astype(BF16), nv=nv_moe)
        ln2 = jnp.stack([ln2_w[l], ln2_b[l]], 0)
        xa = _moe_finish(xa, routed, modv, sh_w_gate[l].astype(BF16), sh_w_up[l].astype(BF16),
                         sh_w_down[l].astype(BF16), ln2, t_lat=t_lat, tm=tm, alpha=alpha)
    return xa[:, :t_lat]
```

```python
import functools

import jax
import jax.numpy as jnp
import numpy as np
from jax import lax
from jax.experimental import pallas as pl
from jax.experimental.pallas import tpu as pltpu

F32 = jnp.float32
BF16 = jnp.bfloat16
HIGHEST = lax.Precision.HIGHEST

GRID_W = 64
MLSTM_H, MLSTM_DH = 4, 64
MLSTM_W = MLSTM_H * MLSTM_DH
RWKV_H, RWKV_N = 4, 64
RWKV_W = RWKV_H * RWKV_N
DECAY_LORA, AAA_LORA, GATE_LORA = 64, 64, 128
RWKV_GN_EPS = 64e-5
MLA_H, MLA_NOPE, MLA_ROPE, MLA_V = 8, 64, 32, 64
Q_RANK, KV_RANK = 256, 128
ROPE_THETA = 10000.0
N_MLSTM_IN = 4 * MLSTM_W + 4 * MLSTM_H
N_RWKV_IN = 3 * RWKV_W + 2 * DECAY_LORA + 2 * AAA_LORA + GATE_LORA
N_EXPERTS, TOP_K, N_GROUPS, TOPK_GROUPS = 64, 8, 8, 4
ROUTED_SCALE = 2.5
LN_EPS = 1e-6
NEG = -1e30

C_QK, C_R, C_V, C_O, C_G, C_A = 0, 512, 1664, 1920, 2176, 2304
N_SHIFT = 1664
N_A = 640
N_EXT = C_A + N_A
HALO = 8
VMEM_LIMIT = 56 * 1024 * 1024


def _cp(sem):
    return pltpu.CompilerParams(dimension_semantics=sem, vmem_limit_bytes=VMEM_LIMIT)


def _ln(x):
    mu = jnp.mean(x, axis=-1, keepdims=True)
    xc = x - mu
    return xc * lax.rsqrt(jnp.mean(xc * xc, axis=-1, keepdims=True) + LN_EPS)


def _sigmoid(x):
    return 1.0 / (1.0 + jnp.exp(-x))


def _silu(x):
    return x * _sigmoid(x)


def _log_sigmoid(x):
    return jnp.minimum(x, 0.0) - jnp.log(1.0 + jnp.exp(-jnp.abs(x)))


def _softplus(x):
    return jnp.maximum(x, 0.0) + jnp.log(1.0 + jnp.exp(-jnp.abs(x)))


def _nt_dot(a, b, **kw):
    return lax.dot_general(a, b, (((1,), (1,)), ((), ())), preferred_element_type=F32, **kw)


def _tn_dot(a, b, **kw):
    return lax.dot_general(a, b, (((0,), (0,)), ((), ())), preferred_element_type=F32, **kw)


def _dot(a, b, **kw):
    return jnp.dot(a, b, preferred_element_type=F32, **kw)


def _mod_kernel(c_ref, w_ref, b_ref, o_ref):
    o_ref[0] = _dot(_silu(c_ref[...]), w_ref[0], precision=HIGHEST) + b_ref[0]


def _modulation(cc, w_mod, b_mod):
    n_layers, d, n = w_mod.shape
    tn = 1536
    return pl.pallas_call(
        _mod_kernel,
        grid=(n_layers, n // tn),
        in_specs=[pl.BlockSpec((8, d), lambda l, j: (0, 0)),
                  pl.BlockSpec((1, d, tn), lambda l, j: (l, 0, j)),
                  pl.BlockSpec((1, 1, tn), lambda l, j: (l, 0, j))],
        out_specs=pl.BlockSpec((1, 8, tn), lambda l, j: (l, 0, j)),
        out_shape=jax.ShapeDtypeStruct((n_layers, 8, n), F32),
        compiler_params=_cp(("parallel", "parallel")),
        name="modulation",
    )(cc, w_mod, b_mod.reshape(n_layers, 1, n))


def _in_kernel(xp_ref, x_ref, xn_ref, mod_ref, w_ref, conv_ref, mu_ref, zm_ref, zg_ref, zr_ref, za_ref, *,
               tm, nt_lat, nt):
    i = pl.program_id(1)
    has_prev = jnp.logical_and(i != 0, i != nt_lat)
    has_next = jnp.logical_and(i != nt_lat - 1, i != nt - 1)
    xt = jnp.concatenate([xp_ref[0], x_ref[0], xn_ref[0]], axis=0)
    sh = mod_ref[0, 0, 0:1, :]
    sc = mod_ref[0, 0, 1:2, :]
    h = (_ln(xt) * (1.0 + sc) + sh).astype(BF16)
    z = _dot(h, w_ref[...])
    rows = lax.broadcasted_iota(jnp.int32, (tm + 2 * HALO, 1), 0)
    lo = jnp.where(has_prev, 0, HALO)
    hi = jnp.where(has_next, tm + 2 * HALO, tm + HALO)
    keep = jnp.logical_and(rows >= lo, rows < hi)
    zs = jnp.where(keep, z[:, :N_SHIFT], 0.0)
    zc = zs[HALO:HALO + tm]
    zprev = pltpu.roll(zs, 1, axis=0)[HALO:HALO + tm]
    znext = pltpu.roll(zs, tm + 2 * HALO - 1, axis=0)[HALO:HALO + tm]
    cw = conv_ref[...]
    qk = (cw[0:1] * zprev[:, :C_R] + cw[1:2] * zc[:, :C_R] + cw[2:3] * znext[:, :C_R])
    qk = _silu(qk)
    lane = lax.broadcasted_iota(jnp.int32, (1, C_R), 1)
    qk = qk * jnp.where(lane < MLSTM_W, MLSTM_DH ** -0.5, 1.0)
    zm_ref[0, :, 0:512] = qk
    zm_ref[0, :, 512:1024] = z[HALO:HALO + tm, C_V:C_G]
    zg_ref[0] = z[HALO:HALO + tm, C_G:C_A]
    zr = zc[:, C_R:]
    zr_ref[0] = zr + mu_ref[...] * (0.5 * (zprev[:, C_R:] + znext[:, C_R:]) - zr)
    za_ref[0] = z[HALO:HALO + tm, C_A:]


def _in_proj(xa, modv, w_ext, conv, mu, *, t_lat, tm):
    b, ta, d = xa.shape
    nt, nt_lat = ta // tm, t_lat // tm
    nh = tm // HALO
    last = ta // HALO - 1
    kern = functools.partial(_in_kernel, tm=tm, nt_lat=nt_lat, nt=nt)
    return pl.pallas_call(
        kern,
        grid=(b, nt),
        in_specs=[pl.BlockSpec((1, HALO, d), lambda bi, i: (bi, jnp.maximum(i * nh - 1, 0), 0)),
                  pl.BlockSpec((1, tm, d), lambda bi, i: (bi, i, 0)),
                  pl.BlockSpec((1, HALO, d), lambda bi, i: (bi, jnp.minimum((i + 1) * nh, last), 0)),
                  pl.BlockSpec((1, 1, 6, d), lambda bi, i: (bi, i // nt_lat, 0, 0)),
                  pl.BlockSpec((d, N_EXT), lambda bi, i: (0, 0)),
                  pl.BlockSpec((3, C_R), lambda bi, i: (0, 0)),
                  pl.BlockSpec((1, N_RWKV_IN), lambda bi, i: (0, 0))],
        out_specs=[pl.BlockSpec((1, tm, 1024), lambda bi, i: (bi, i, 0)),
                   pl.BlockSpec((1, tm, 128), lambda bi, i: (bi, i, 0)),
                   pl.BlockSpec((1, tm, N_RWKV_IN), lambda bi, i: (bi, i, 0)),
                   pl.BlockSpec((1, tm, N_A), lambda bi, i: (bi, i, 0))],
        out_shape=[jax.ShapeDtypeStruct((b, ta, 1024), F32),
                   jax.ShapeDtypeStruct((b, ta, 128), F32),
                   jax.ShapeDtypeStruct((b, ta, N_RWKV_IN), F32),
                   jax.ShapeDtypeStruct((b, ta, N_A), F32)],
        compiler_params=_cp(("parallel", "parallel")),
        name="in_proj",
    )(xa, xa, xa, modv, w_ext, conv, mu)


def _mlstm_kernel(zm_ref, gc_ref, gr_ref, bc_ref, br_ref, o_ref, ct_ref, n_ref, m_ref, *, cl):
    d = pl.program_id(1)

    @pl.when(pl.program_id(2) == 0)
    def _():
        ct_ref[...] = jnp.zeros_like(ct_ref)
        n_ref[...] = jnp.zeros_like(n_ref)
        m_ref[...] = jnp.zeros_like(m_ref)

    sgn = 1 - 2 * d
    gc = gc_ref[0, 0] + bc_ref[0]
    gr = gr_ref[0, 0] + br_ref[0]
    lf_c = _log_sigmoid(gc)
    lf_r = _log_sigmoid(gr)
    ii = lax.broadcasted_iota(jnp.int32, (cl, cl), 0)
    jj = lax.broadcasted_iota(jnp.int32, (cl, cl), 1)
    mask = (jj - ii) * sgn <= 0
    mask_t = (ii - jj) * sgn <= 0
    cum_c = _dot(mask.astype(F32), lf_c, precision=HIGHEST)
    cum_r = _dot(lf_r, mask_t.astype(F32), precision=HIGHEST)
    outs = []
    for h in range(MLSTM_H):
        sl = slice(h * MLSTM_DH, (h + 1) * MLSTM_DH)
        q = zm_ref[0, :, sl]
        k = zm_ref[0, :, MLSTM_W + h * MLSTM_DH:MLSTM_W + (h + 1) * MLSTM_DH]
        v = zm_ref[0, :, 2 * MLSTM_W + h * MLSTM_DH:2 * MLSTM_W + (h + 1) * MLSTM_DH]
        qb, kb = q.astype(BF16), k.astype(BF16)
        li_c, li_r = gc[:, h:h + 1], gr[h:h + 1, :]
        cu_c, cu_r = cum_c[:, 4 + h:5 + h], cum_r[4 + h:5 + h, :]
        b_end = jnp.sum(lf_r[4 + h:5 + h, :], axis=1, keepdims=True)
        m_prev = m_ref[h:h + 1, 0:1]
        ct = ct_ref[h]
        nv = n_ref[h:h + 1, 0:MLSTM_DH]
        d_log = jnp.where(mask, cu_c - cu_r + li_r, NEG)
        g_log = cu_c + m_prev
        m_row = jnp.maximum(g_log, jnp.max(d_log, axis=1, keepdims=True))
        w_intra = jnp.exp(d_log - m_row) * _nt_dot(qb, kb)
        e_inter = jnp.exp(g_log - m_row)
        num = _dot(w_intra.astype(BF16), v.astype(BF16)) + e_inter * _dot(qb, ct.astype(BF16))
        den = jnp.sum(w_intra, axis=1, keepdims=True) + e_inter * jnp.sum(q * nv, axis=1, keepdims=True)
        outs.append(num / jnp.maximum(jnp.abs(den), jnp.exp(-m_row)))
        w_end = b_end - cu_c + li_c
        m_loc = jnp.max(w_end, axis=0, keepdims=True)
        e_end = jnp.exp(w_end - m_loc)
        m_new = jnp.maximum(b_end + m_prev, m_loc)
        a_old = jnp.exp(b_end + m_prev - m_new)
        a_loc = jnp.exp(m_loc - m_new)
        ct_ref[h] = a_old * ct + a_loc * _tn_dot(kb, (v * e_end).astype(BF16))
        n_ref[h:h + 1, 0:MLSTM_DH] = a_old * nv + a_loc * jnp.sum(k * e_end, axis=0, keepdims=True)
        m_ref[h:h + 1, :] = jnp.broadcast_to(m_new, (1, 128))
    o_ref[0, 0] = jnp.concatenate(outs, axis=1)


def _mlstm(zm, g_col, g_row, b_col, b_row, *, t_lat, cl):
    b, ta, _ = zm.shape
    nc, nc_lat = ta // cl, t_lat // cl
    nc_ctx = nc - nc_lat

    def chunk(di, c):
        fwd_idx = jnp.where(c < nc_ctx, nc_lat + c, c - nc_ctx)
        return jnp.where(di == 0, fwd_idx, nc - 1 - c)

    return pl.pallas_call(
        functools.partial(_mlstm_kernel, cl=cl),
        grid=(b, 2, nc),
        in_specs=[pl.BlockSpec((1, cl, 1024), lambda bi, di, c: (bi, chunk(di, c), 0)),
                  pl.BlockSpec((1, 1, cl, 8), lambda bi, di, c: (bi, di, chunk(di, c), 0)),
                  pl.BlockSpec((1, 1, 8, cl), lambda bi, di, c: (bi, di, 0, chunk(di, c))),
                  pl.BlockSpec((1, 1, 8), lambda bi, di, c: (di, 0, 0)),
                  pl.BlockSpec((1, 8, 1), lambda bi, di, c: (di, 0, 0))],
        out_specs=pl.BlockSpec((1, 1, cl, MLSTM_W), lambda bi, di, c: (bi, di, chunk(di, c), 0)),
        out_shape=jax.ShapeDtypeStruct((b, 2, ta, MLSTM_W), F32),
        scratch_shapes=[pltpu.VMEM((MLSTM_H, MLSTM_DH, MLSTM_DH), F32),
                        pltpu.VMEM((8, 128), F32),
                        pltpu.VMEM((8, 128), F32)],
        compiler_params=_cp(("parallel", "parallel", "arbitrary")),
        name="mlstm",
    )(zm, g_col, g_row, b_col, b_row)


def _rwkv_prep_kernel(zr_ref, wup_ref, aup_ref, gup_ref, vec_ref, hsum_ref, pd_ref, v_ref, g_ref, bonus_ref):
    nb, tm = zr_ref.shape[0], zr_ref.shape[1]
    nbh = nb * RWKV_H
    vec = vec_ref[...]
    hsum = hsum_ref[...]
    zero = jnp.zeros((tm, RWKV_N), F32)
    for bi in range(nb):
        z = zr_ref[bi]
        zr_, zk, zv = z[:, 0:256], z[:, 256:512], z[:, 512:768]
        zw, za, zg = z[:, 768:896], z[:, 896:1024], z[:, 1024:1152]
        w_raw = _dot(jnp.tanh(zw).astype(BF16), wup_ref[...])
        a_raw = _dot(za.astype(BF16), aup_ref[...])
        g_ref[bi] = _dot(_sigmoid(zg).astype(BF16), gup_ref[...])
        kk = zk * vec[4:5]
        kk_ss = _dot(kk * kk, hsum, precision=HIGHEST)
        kk = kk * lax.rsqrt(jnp.maximum(kk_ss, 1e-12))
        k_sum = jnp.zeros_like(zk)
        for di in range(2):
            sl = slice(di * RWKV_W, (di + 1) * RWKV_W)
            decay = jnp.exp(-jnp.exp(-_softplus(-(vec[di:di + 1] + w_raw[:, sl])) - 0.5))
            a = _sigmoid(vec[2 + di:3 + di] + a_raw[:, sl])
            k_dir = zk * (1.0 + (a - 1.0) * vec[5:6])
            k_sum = k_sum + k_dir
            ka = kk * a
            for h in range(RWKV_H):
                hs = slice(h * RWKV_N, (h + 1) * RWKV_N)
                rows = pl.ds(bi * RWKV_H + h, tm, stride=nbh)
                pd_ref[di, 0, rows, :] = jnp.concatenate([decay[:, hs], ka[:, hs]], axis=1)
                pd_ref[di, 1, rows, :] = jnp.concatenate([k_dir[:, hs], kk[:, hs]], axis=1)
                pd_ref[di, 2, rows, :] = jnp.concatenate([zr_[:, hs], zero], axis=1)
        v_ref[bi] = zv
        bonus_ref[bi] = _dot(zr_ * vec[6:7] * k_sum, hsum, precision=HIGHEST) * zv


def _rwkv_prep(zr, wup, aup, gup, vec, hsum, *, tm):
    b, ta, _ = zr.shape
    nbh = b * RWKV_H
    tok = pl.BlockSpec((b, tm, RWKV_W), lambda i: (0, i, 0))
    full = lambda a: pl.BlockSpec(a.shape, lambda i: (0,) * a.ndim)
    return pl.pallas_call(
        _rwkv_prep_kernel,
        grid=(ta // tm,),
        in_specs=[pl.BlockSpec((b, tm, N_RWKV_IN), lambda i: (0, i, 0)),
                  full(wup), full(aup), full(gup), full(vec), full(hsum)],
        out_specs=[pl.BlockSpec((2, 3, tm * nbh, 128), lambda i: (0, 0, i, 0)), tok, tok, tok],
        out_shape=[jax.ShapeDtypeStruct((2, 3, ta * nbh, 128), F32)] + [jax.ShapeDtypeStruct((b, ta, RWKV_W), F32)] * 3,
        compiler_params=_cp(("parallel",)),
        name="rwkv_prep",
    )(zr, wup, aup, gup, vec, hsum)


def _rwkv_scan_kernel(pf_ref, pb_ref, pat_ref, vf_ref, vb_ref, of_ref, ob_ref, st_ref, x_ref, tt_ref, *, ts, nvh, nbh):
    ng = ts // 8
    rows = 8 * nbh

    @pl.when(pl.program_id(0) == 0)
    def _():
        st_ref[...] = jnp.zeros_like(st_ref)

    def transpose(di, p_ref, grp):
        sl = pl.ds(pl.multiple_of(grp * rows, rows), rows)
        for j in range(3):
            tt_ref[di, j] = p_ref[0, j, sl, :].T

    def update(di, u, v_ref, o_ref, t):
        pat = pat_ref[u]
        for j, (tj, half) in enumerate(((0, 0), (0, 1), (1, 0), (1, 1), (2, 0))):
            x_ref[di, j] = jnp.take_along_axis(tt_ref[di, tj, half * RWKV_N:(half + 1) * RWKV_N, :], pat, axis=1)
        vt = v_ref[t]
        nkb = RWKV_N // 8
        kblk = lambda kb: pl.ds(kb * 8, 8)
        acc = [None] * nvh
        for kb in range(nkb):
            kk = x_ref[di, 3, kblk(kb), :]
            for vh in range(nvh):
                term = st_ref[di, vh, kblk(kb), :] * kk
                acc[vh] = term if kb == 0 else acc[vh] + term
        sa = [jnp.sum(a, axis=0, keepdims=True) for a in acc]
        out = [None] * nvh
        for kb in range(nkb):
            w, ka = x_ref[di, 0, kblk(kb), :], x_ref[di, 1, kblk(kb), :]
            k, r = x_ref[di, 2, kblk(kb), :], x_ref[di, 4, kblk(kb), :]
            for vh in range(nvh):
                sv = st_ref[di, vh, kblk(kb), :] * w - sa[vh] * ka + vt[vh:vh + 1, :] * k
                st_ref[di, vh, kblk(kb), :] = sv
                out[vh] = sv * r if kb == 0 else out[vh] + sv * r
        o_ref[t] = jnp.concatenate([jnp.sum(o, axis=0, keepdims=True) for o in out], axis=0)

    def group(g, carry):
        transpose(0, pf_ref, g)
        transpose(1, pb_ref, ng - 1 - g)
        for u in range(8):
            update(0, u, vf_ref, of_ref, g * 8 + u)
            update(1, 7 - u, vb_ref, ob_ref, ts - 1 - (g * 8 + u))
        return carry

    lax.fori_loop(0, ng, group, 0)


def _rwkv_scan(pd, pats, vs, *, t_lat, ts):
    ta, nvh, nl = vs.shape
    nbh = pd.shape[2] // ta
    nt, nt_lat = ta // ts, t_lat // ts
    fwd = lambda i: (i + nt_lat) % nt
    bwd = lambda i: nt - 1 - i
    return pl.pallas_call(
        functools.partial(_rwkv_scan_kernel, ts=ts, nvh=nvh, nbh=nbh),
        grid=(nt,),
        in_specs=[pl.BlockSpec((1, 3, ts * nbh, 128), lambda i: (0, 0, fwd(i), 0)),
                  pl.BlockSpec((1, 3, ts * nbh, 128), lambda i: (1, 0, bwd(i), 0)),
                  pl.BlockSpec(pats.shape, lambda i: (0, 0, 0)),
                  pl.BlockSpec((ts, nvh, nl), lambda i: (fwd(i), 0, 0)),
                  pl.BlockSpec((ts, nvh, nl), lambda i: (bwd(i), 0, 0))],
        out_specs=[pl.BlockSpec((ts, nvh, nl), lambda i: (fwd(i), 0, 0)),
                   pl.BlockSpec((ts, nvh, nl), lambda i: (bwd(i), 0, 0))],
        out_shape=[jax.ShapeDtypeStruct((ta, nvh, nl), F32)] * 2,
        scratch_shapes=[pltpu.VMEM((2, nvh, RWKV_N, nl), F32), pltpu.VMEM((2, 5, RWKV_N, nl), F32),
                        pltpu.VMEM((2, 3, 128, 8 * nbh), F32)],
        compiler_params=_cp(("arbitrary",)),
        name="rwkv_scan",
    )(pd, pd, pats, vs, vs)


def _mla_proj_kernel(za_ref, cos_ref, sin_ref, nq_ref, nkv_ref, wq_ref, wqr_ref, wk_ref, wv_ref, q_ref, k_ref, v_ref):
    za = za_ref[0]
    cq, ckv = za[:, 0:Q_RANK], za[:, Q_RANK:Q_RANK + KV_RANK]
    kr, krr = za[:, 384:512], za[:, 512:640]
    cos, sin = cos_ref[...], sin_ref[...]
    cqn = (cq * lax.rsqrt(jnp.mean(cq * cq, axis=-1, keepdims=True) + 1e-6) * nq_ref[...]).astype(BF16)
    ckvn = (ckv * lax.rsqrt(jnp.mean(ckv * ckv, axis=-1, keepdims=True) + 1e-6) * nkv_ref[...]).astype(BF16)
    q = _dot(cqn, wq_ref[...])
    qr = _dot(cqn, wqr_ref[...])
    kn = _dot(ckvn, wk_ref[...])
    v_ref[0] = _nt_dot(wv_ref[...], ckvn).astype(BF16)
    k_rope = kr * cos + krr * sin
    scale = (MLA_NOPE + MLA_ROPE) ** -0.5 * np.log2(np.e)
    for h in range(MLA_H):
        sl = slice(h * 128, (h + 1) * 128)
        q_ref[0, :, sl] = ((q[:, sl] * cos + qr[:, sl] * sin) * scale).astype(BF16)
        k_ref[0, :, sl] = (kn[:, sl] + k_rope).astype(BF16)


def _mla_proj(za, cos, sin, nq, nkv, wq, wqr, wk, wv, *, tm):
    b, ta, _ = za.shape
    full = lambda a: pl.BlockSpec(a.shape, lambda bi, i: (0,) * a.ndim)
    return pl.pallas_call(
        _mla_proj_kernel,
        grid=(b, ta // tm),
        in_specs=[pl.BlockSpec((1, tm, N_A), lambda bi, i: (bi, i, 0)),
                  pl.BlockSpec((tm, 128), lambda bi, i: (i, 0)),
                  pl.BlockSpec((tm, 128), lambda bi, i: (i, 0)),
                  full(nq), full(nkv), full(wq), full(wqr), full(wk), full(wv)],
        out_specs=[pl.BlockSpec((1, tm, 1024), lambda bi, i: (bi, i, 0)),
                   pl.BlockSpec((1, tm, 1024), lambda bi, i: (bi, i, 0)),
                   pl.BlockSpec((1, 512, tm), lambda bi, i: (bi, 0, i))],
        out_shape=[jax.ShapeDtypeStruct((b, ta, 1024), BF16),
                   jax.ShapeDtypeStruct((b, ta, 1024), BF16),
                   jax.ShapeDtypeStruct((b, 512, ta), BF16)],
        compiler_params=_cp(("parallel", "parallel")),
        name="mla_proj",
    )(za, cos, sin, nq, nkv, wq, wqr, wk, wv)


def _attn_kernel(q_ref, k_ref, vt_ref, o_ref):
    outs = []
    for h in range(ATTN_HEADS):
        sl = slice(h * 128, (h + 1) * 128)
        pair = h // 2
        st = _nt_dot(k_ref[0, :, sl], q_ref[0, :, sl])
        m = jnp.max(st, axis=0, keepdims=True)
        p = jnp.exp2(st - m)
        l = jnp.sum(p, axis=0, keepdims=True)
        ot = _dot(vt_ref[0, pair * 128:(pair + 1) * 128, :], p.astype(BF16))
        outs.append(ot[(h % 2) * MLA_V:(h % 2 + 1) * MLA_V] / l)
    o_ref[0] = jnp.concatenate(outs, axis=0).T


ATTN_HEADS = 4


def _attention(q, k, vt, *, q_off, n_q, k_off, n_keys, tq):
    b = q.shape[0]
    qo, ko = q_off // tq, k_off // n_keys
    hq, hv = ATTN_HEADS * 128, ATTN_HEADS * MLA_V
    return pl.pallas_call(
        _attn_kernel,
        grid=(b, MLA_H // ATTN_HEADS, n_q // tq),
        in_specs=[pl.BlockSpec((1, tq, hq), lambda bi, hp, i: (bi, qo + i, hp)),
                  pl.BlockSpec((1, n_keys, hq), lambda bi, hp, i: (bi, ko, hp)),
                  pl.BlockSpec((1, hv, n_keys), lambda bi, hp, i: (bi, hp, ko))],
        out_specs=pl.BlockSpec((1, tq, hv), lambda bi, hp, i: (bi, i, hp)),
        out_shape=jax.ShapeDtypeStruct((b, n_q, 512), F32),
        compiler_params=_cp(("parallel", "parallel", "parallel")),
        name="mla_attention",
    )(q, k, vt)


def _out_kernel(x_ref, hf_ref, hb_ref, zm_ref, of_ref, ob_ref, g_ref, bonus_ref, attl_ref, attc_ref, mod_ref, vec_ref,
                hmean_ref, wo_ref, ln_ref, o_ref, *, alpha, nt_lat):
    hmean = hmean_ref[...]
    att = jnp.where(pl.program_id(1) < nt_lat, attl_ref[0], attc_ref[0])

    def head_norm(y, eps):
        mu = _dot(y, hmean, precision=HIGHEST)
        yc = y - mu
        return yc * lax.rsqrt(_dot(yc * yc, hmean, precision=HIGHEST) + eps)

    vec = vec_ref[...]
    zo = zm_ref[0, :, 768:1024]
    m_mix = _sigmoid(zo) * (head_norm(hf_ref[0, 0] + hb_ref[0, 0], 1e-6) * vec[0:1])
    r_o = head_norm(of_ref[0, 0] + ob_ref[0, 0], RWKV_GN_EPS) * vec[1:2] + vec[2:3] + bonus_ref[0]
    r_mix = r_o * g_ref[0]
    mix = jnp.concatenate([m_mix, r_mix, att], axis=1).astype(BF16)
    y = _dot(mix, wo_ref[...])
    gate = mod_ref[0, 0, 2:3, :]
    o_ref[0] = _ln(alpha * x_ref[0] + gate * y) * ln_ref[0:1] + ln_ref[1:2]


def _out_proj(xa, h_m, zm, o_r, g, bonus, att_l, att_c, modv, vec, hmean, wo, ln, *, t_lat, tm, alpha):
    b, ta, d = xa.shape
    nt_lat = t_lat // tm
    tok = lambda w: pl.BlockSpec((1, tm, w), lambda bi, i: (bi, i, 0))
    full = lambda a: pl.BlockSpec(a.shape, lambda bi, i: (0,) * a.ndim)
    return pl.pallas_call(
        functools.partial(_out_kernel, alpha=alpha, nt_lat=nt_lat),
        grid=(b, ta // tm),
        in_specs=[tok(d),
                  pl.BlockSpec((1, 1, tm, MLSTM_W), lambda bi, i: (bi, 0, i, 0)),
                  pl.BlockSpec((1, 1, tm, MLSTM_W), lambda bi, i: (bi, 1, i, 0)),
                  tok(1024),
                  pl.BlockSpec((1, 1, tm, RWKV_W), lambda bi, i: (0, bi, i, 0)),
                  pl.BlockSpec((1, 1, tm, RWKV_W), lambda bi, i: (1, bi, i, 0)),
                  tok(RWKV_W), tok(RWKV_W),
                  pl.BlockSpec((1, tm, 512), lambda bi, i: (bi, jnp.minimum(i, nt_lat - 1), 0)),
                  pl.BlockSpec((1, tm, 512), lambda bi, i: (bi, jnp.maximum(i - nt_lat, 0), 0)),
                  pl.BlockSpec((1, 1, 6, d), lambda bi, i: (bi, i // nt_lat, 0, 0)),
                  full(vec), full(hmean), full(wo), full(ln)],
        out_specs=tok(d),
        out_shape=jax.ShapeDtypeStruct((b, ta, d), F32),
        compiler_params=_cp(("parallel", "parallel")),
        name="out_proj",
    )(xa, h_m, h_m, zm, o_r, o_r, g, bonus, att_l, att_c, modv, vec, hmean, wo, ln)


def _first_max(x, idx, axis):
    mx = jnp.max(x, axis=axis, keepdims=True)
    first = jnp.min(jnp.where(x == mx, idx, 1 << 20), axis=axis, keepdims=True)
    return mx, first, idx == first


def _router_kernel(x_ref, mod_ref, w_ref, bias_ref, id_ref, gate_ref, h_ref):
    tm = x_ref.shape[1]
    sh, sc = mod_ref[0, 0, 3:4, :], mod_ref[0, 0, 4:5, :]
    h = _ln(x_ref[0]) * (1.0 + sc) + sh
    for j in range(h.shape[1] // 128):
        h_ref[0, pl.ds(j, tm, stride=h.shape[1] // 128), :] = h[:, j * 128:(j + 1) * 128]
    scores = _sigmoid(_nt_dot(w_ref[...], h, precision=HIGHEST))
    biased = scores + bias_ref[...]
    gsz = N_EXPERTS // N_GROUPS
    b3 = biased.reshape(N_GROUPS, gsz, tm)
    s3 = scores.reshape(N_GROUPS, gsz, tm)
    e_idx = lax.broadcasted_iota(jnp.int32, (N_GROUPS, gsz, tm), 1)
    m1, _, hit = _first_max(b3, e_idx, 1)
    m2 = jnp.max(jnp.where(hit, -jnp.inf, b3), axis=1, keepdims=True)
    gscore = m1 + m2
    g_idx = lax.broadcasted_iota(jnp.int32, (N_GROUPS, 1, tm), 0)
    gsel = jnp.zeros((N_GROUPS, 1, tm), F32)
    for _ in range(TOPK_GROUPS):
        _, _, hit = _first_max(jnp.where(gsel > 0, -jnp.inf, gscore), g_idx, 0)
        gsel = jnp.where(hit, 1.0, gsel)
    cand = jnp.where(jnp.broadcast_to(gsel, b3.shape) > 0, b3, -jnp.inf)
    x_idx = g_idx * gsz + e_idx
    sel = jnp.zeros(b3.shape, F32)
    ids, picked = [], []
    for _ in range(TOP_K):
        _, first, hit = _first_max(jnp.where(sel > 0, -jnp.inf, cand), x_idx, (0, 1))
        sel = jnp.where(hit, 1.0, sel)
        ids.append(first[0])
        picked.append(jnp.sum(jnp.where(hit, s3, 0.0), axis=(0, 1), keepdims=True)[0])
    picked = jnp.concatenate(picked, axis=0)
    id_ref[0] = jnp.concatenate(ids, axis=0)
    gate_ref[0] = ROUTED_SCALE * picked / jnp.sum(picked, axis=0, keepdims=True)


def _router(xa, modv, w_t, bias, *, t_lat, tm):
    b, ta, d = xa.shape
    nt_lat = t_lat // tm
    return pl.pallas_call(
        _router_kernel,
        grid=(b, ta // tm),
        in_specs=[pl.BlockSpec((1, tm, d), lambda bi, i: (bi, i, 0)),
                  pl.BlockSpec((1, 1, 6, d), lambda bi, i: (bi, i // nt_lat, 0, 0)),
                  pl.BlockSpec((N_EXPERTS, d), lambda bi, i: (0, 0)),
                  pl.BlockSpec((N_EXPERTS, 1), lambda bi, i: (0, 0))],
        out_specs=[pl.BlockSpec((1, TOP_K, tm), lambda bi, i: (bi, 0, i)),
                   pl.BlockSpec((1, TOP_K, tm), lambda bi, i: (bi, 0, i)),
                   pl.BlockSpec((1, tm * (d // 128), 128), lambda bi, i: (bi, i, 0))],
        out_shape=[jax.ShapeDtypeStruct((b, TOP_K, ta), jnp.int32),
                   jax.ShapeDtypeStruct((b, TOP_K, ta), F32),
                   jax.ShapeDtypeStruct((b, ta * (d // 128), 128), F32)],
        compiler_params=_cp(("parallel", "parallel")),
        name="router",
    )(xa, modv, w_t, bias)


MOE_RB = 256
MOE_PITCH = MOE_RB + 8
MOE_U = 8


def _moe_routed_kernel(vblk_ref, vexp_ref, vlo_ref, vhi_ref, vvalid_ref, vfirst_ref,
                       tok_ref, dst_ref, gate_ref, src_ref, wg_ref, wu_ref, wd_ref, acc_ref, xt_ref, yt_ref, xb_ref,
                       gc_ref, *, nv, nchunk):
    bi, v = pl.program_id(0), pl.program_id(1)
    pos = bi * nv + v
    lo, hi = vlo_ref[pos], vhi_ref[pos]

    @pl.when(v == 0)
    def _():
        acc_ref[...] = jnp.zeros_like(acc_ref)

    @pl.when(vfirst_ref[pos] == 1)
    def _():
        for r in range(MOE_RB):
            off = pl.multiple_of(tok_ref[0, 0, 0, r], nchunk)
            xt_ref[pl.ds(r, nchunk, stride=MOE_PITCH), :] = src_ref[0, pl.ds(off, nchunk), :]
        for j in range(nchunk):
            xb_ref[:, j * 128:(j + 1) * 128] = xt_ref[j * MOE_PITCH:j * MOE_PITCH + MOE_RB, :].astype(BF16)
        ii = lax.broadcasted_iota(jnp.int32, (MOE_RB, MOE_RB), 0)
        jj = lax.broadcasted_iota(jnp.int32, (MOE_RB, MOE_RB), 1)
        gc_ref[...] = jnp.sum(jnp.where(ii == jj, gate_ref[0, 0], 0.0), axis=1, keepdims=True)

    @pl.when(vvalid_ref[pos] == 1)
    def _():
        x = xb_ref[...]
        row = lax.broadcasted_iota(jnp.int32, (MOE_RB, 1), 0)
        gcol = jnp.where(jnp.logical_and(row >= lo, row < hi), gc_ref[...], 0.0)
        a = _silu(_dot(x, wg_ref[0])) * _dot(x, wu_ref[0]) * gcol
        y = _dot(a.astype(BF16), wd_ref[0])
        for j in range(nchunk):
            yt_ref[j * MOE_PITCH:j * MOE_PITCH + MOE_RB, :] = y[:, j * 128:(j + 1) * 128]
        for g in range(MOE_RB // MOE_U):
            pending = []
            for u in range(MOE_U):
                r = g * MOE_U + u
                off = pl.multiple_of(dst_ref[0, 0, 0, r], nchunk)
                pending.append((off, acc_ref[0, pl.ds(off, nchunk), :] + yt_ref[pl.ds(r, nchunk, stride=MOE_PITCH), :]))
            for off, val in pending:
                acc_ref[0, pl.ds(off, nchunk), :] = val


def _moe_routed(h_rows, tok, dst, gate, tables, wg, wu, wd, *, nv):
    b, rows, _ = h_rows.shape
    d, de = wg.shape[1], wg.shape[2]
    nchunk = d // 128
    nblk = tok.shape[1] // MOE_RB
    blk = lambda bi, v, vblk, *_: (bi, vblk[bi * nv + v], 0, 0)
    exp = lambda bi, v, vblk, vexp, *_: (vexp[bi * nv + v], 0, 0)
    grid_spec = pltpu.PrefetchScalarGridSpec(
        num_scalar_prefetch=6,
        grid=(b, nv),
        in_specs=[pl.BlockSpec((1, 1, 1, MOE_RB), blk, memory_space=pltpu.SMEM),
                  pl.BlockSpec((1, 1, 1, MOE_RB), lambda bi, v, *_: (bi, v, 0, 0), memory_space=pltpu.SMEM),
                  pl.BlockSpec((1, 1, 1, MOE_RB), blk),
                  pl.BlockSpec((1, rows, 128), lambda bi, v, *_: (bi, 0, 0), pipeline_mode=pl.Buffered(1)),
                  pl.BlockSpec((1, d, de), exp),
                  pl.BlockSpec((1, d, de), exp),
                  pl.BlockSpec((1, de, d), exp)],
        out_specs=pl.BlockSpec((1, rows + MOE_U * nchunk, 128), lambda bi, v, *_: (bi, 0, 0),
                               pipeline_mode=pl.Buffered(1)),
        scratch_shapes=[pltpu.VMEM((nchunk * MOE_PITCH, 128), F32), pltpu.VMEM((nchunk * MOE_PITCH, 128), F32),
                        pltpu.VMEM((MOE_RB, d), BF16), pltpu.VMEM((MOE_RB, 1), F32)],
    )
    return pl.pallas_call(
        functools.partial(_moe_routed_kernel, nv=nv, nchunk=nchunk),
        grid_spec=grid_spec,
        out_shape=jax.ShapeDtypeStruct((b, rows + MOE_U * nchunk, 128), F32),
        compiler_params=_cp(("parallel", "arbitrary")),
        name="moe_routed",
    )(*tables, tok.reshape(b, nblk, 1, MOE_RB), dst.reshape(b, nv, 1, MOE_RB), gate.reshape(b, nblk, 1, MOE_RB),
      h_rows, wg, wu, wd)


def _moe_tables(ids, gates, *, ta, nv, nchunk):
    b = ids.shape[0]
    n = TOP_K * ta
    nblk = n // MOE_RB
    t_idx = jnp.broadcast_to(jnp.arange(ta, dtype=jnp.int32), (b, TOP_K, ta))
    keys, g_sorted = lax.sort(((ids * ta + t_idx).reshape(b, n), gates.reshape(b, n)), dimension=1, num_keys=1)
    tok, e_sorted = keys % ta, keys // ta
    ends = jnp.sum(e_sorted[:, None, :] <= jnp.arange(N_EXPERTS, dtype=jnp.int32)[None, :, None], axis=2)
    starts = ends - jnp.sum(e_sorted[:, None, :] == jnp.arange(N_EXPERTS, dtype=jnp.int32)[None, :, None], axis=2)
    e_lo, e_hi = e_sorted[:, ::MOE_RB], e_sorted[:, MOE_RB - 1::MOE_RB]
    n_vis = e_hi - e_lo + 1
    v_end = jnp.cumsum(n_vis, axis=1)
    v_start = v_end - n_vis
    v = jnp.arange(nv, dtype=jnp.int32)
    blk = jnp.minimum(jnp.sum(v_end[:, None, :] <= v[None, :, None], axis=2), nblk - 1).astype(jnp.int32)
    take = lambda a, i: jnp.take_along_axis(a, i, axis=1)
    valid = v[None, :] < v_end[:, -1:]
    exp = jnp.where(valid, take(e_lo, blk) + v[None, :] - take(v_start, blk), take(e_hi, blk)).astype(jnp.int32)
    lo = jnp.clip(take(starts, exp) - blk * MOE_RB, 0, MOE_RB)
    hi = jnp.where(valid, jnp.clip(take(ends, exp) - blk * MOE_RB, 0, MOE_RB), 0)
    first = jnp.logical_and(valid, v[None, :] == take(v_start, blk))
    flat = lambda a: a.astype(jnp.int32).reshape(-1)
    r = jnp.arange(MOE_RB, dtype=jnp.int32)
    tok_v = jnp.take_along_axis(tok.reshape(b, nblk, MOE_RB), blk[:, :, None], axis=1)
    mine = jnp.logical_and(r >= lo[:, :, None], r < hi[:, :, None])
    dst = jnp.where(mine, tok_v, ta + r % MOE_U)
    rows = nchunk
    return ((tok * rows).astype(jnp.int32), (dst * rows).astype(jnp.int32), g_sorted,
            tuple(flat(a) for a in (blk, exp, lo, hi, valid, first)))


def _moe_finish_kernel(x_ref, r_ref, mod_ref, sg_ref, su_ref, sd_ref, ln_ref, o_ref, *, alpha):
    tm, d = x_ref.shape[1], x_ref.shape[2]
    nchunk = d // 128
    x = x_ref[0]
    h = (_ln(x) * (1.0 + mod_ref[0, 0, 4:5, :]) + mod_ref[0, 0, 3:4, :]).astype(BF16)
    shared = _dot((_silu(_dot(h, sg_ref[...])) * _dot(h, su_ref[...])).astype(BF16), sd_ref[...])
    routed = jnp.concatenate([r_ref[0, pl.ds(j, tm, stride=nchunk), :] for j in range(nchunk)], axis=1)
    o_ref[0] = _ln(alpha * x + mod_ref[0, 0, 5:6, :] * (routed + shared)) * ln_ref[0:1] + ln_ref[1:2]


def _moe_finish(xa, routed, modv, sg, su, sd, ln, *, t_lat, tm, alpha, n_rows):
    b, ta, d = xa.shape
    nt_lat = t_lat // tm
    full = lambda a: pl.BlockSpec(a.shape, lambda bi, i: (0,) * a.ndim)
    return pl.pallas_call(
        functools.partial(_moe_finish_kernel, alpha=alpha),
        grid=(b, n_rows // tm),
        in_specs=[pl.BlockSpec((1, tm, d), lambda bi, i: (bi, i, 0)),
                  pl.BlockSpec((1, tm * (d // 128), 128), lambda bi, i: (bi, i, 0)),
                  pl.BlockSpec((1, 1, 6, d), lambda bi, i: (bi, i // nt_lat, 0, 0)),
                  full(sg), full(su), full(sd), full(ln)],
        out_specs=pl.BlockSpec((1, tm, d), lambda bi, i: (bi, i, 0)),
        out_shape=jax.ShapeDtypeStruct((b, n_rows, d), F32),
        compiler_params=_cp(("parallel", "parallel")),
        name="moe_finish",
    )(xa, routed, modv, sg, su, sd, ln)


def _rope_blocks(w_rope):
    k = w_rope.shape[0]
    ev, od = w_rope[:, 0::2], w_rope[:, 1::2]
    z64, z32 = jnp.zeros((k, 64), w_rope.dtype), jnp.zeros((k, 32), w_rope.dtype)
    return (jnp.concatenate([z64, ev, od, z32], 1), jnp.concatenate([z64, -od, ev, z32], 1))


def _layer_weights(l, w_in, mla_w_uq, mla_w_ukv):
    w = w_in[l]
    d = w.shape[0]
    wm, wr, wa = w[:, :N_MLSTM_IN], w[:, N_MLSTM_IN:N_MLSTM_IN + N_RWKV_IN], w[:, N_MLSTM_IN + N_RWKV_IN:]
    kr, krr = _rope_blocks(wa[:, Q_RANK + KV_RANK:])
    w_ext = jnp.concatenate([
        wm[:, :2 * MLSTM_W], wr, wm[:, 2 * MLSTM_W:4 * MLSTM_W],
        wm[:, 4 * MLSTM_W:], jnp.zeros((d, 128 - 4 * MLSTM_H), w.dtype),
        wa[:, :Q_RANK + KV_RANK], kr, krr], axis=1).astype(BF16)
    uq = mla_w_uq[l].reshape(Q_RANK, MLA_H, MLA_NOPE + MLA_ROPE)
    wq, wqr = [], []
    for h in range(MLA_H):
        r0, r1 = _rope_blocks(uq[:, h, MLA_NOPE:])
        wq.append(r0.at[:, :MLA_NOPE].set(uq[:, h, :MLA_NOPE]))
        wqr.append(r1)
    ukv = mla_w_ukv[l].reshape(KV_RANK, MLA_H, MLA_NOPE + MLA_V)
    wk = jnp.concatenate([ukv[:, :, :MLA_NOPE], jnp.zeros((KV_RANK, MLA_H, 64), F32)], axis=2).reshape(KV_RANK, MLA_H * 128)
    wv = ukv[:, :, MLA_NOPE:].reshape(KV_RANK, MLA_H * MLA_V)
    return w_ext, jnp.concatenate(wq, 1).astype(BF16), jnp.concatenate(wqr, 1).astype(BF16), wk.astype(BF16), wv.T.astype(BF16)


def _rope_tables(t_lat, n_ctx):
    rows = t_lat // GRID_W
    row = jnp.repeat(jnp.arange(rows), GRID_W).astype(F32)
    col = jnp.tile(jnp.arange(GRID_W), rows).astype(F32)
    n_freq = MLA_ROPE // 4
    freq = ROPE_THETA ** (-jnp.arange(n_freq, dtype=F32) / n_freq)
    ang = jnp.concatenate([row[:, None] * freq, col[:, None] * freq], -1)
    cos, sin = jnp.cos(ang), jnp.sin(ang)
    one, zero = jnp.ones((t_lat, 64), F32), jnp.zeros((t_lat, 32), F32)
    cos_l = jnp.concatenate([one, cos, cos, zero], 1)
    sin_l = jnp.concatenate([0 * one, sin, sin, zero], 1)
    cos_c = jnp.concatenate([jnp.ones((n_ctx, 96), F32), jnp.zeros((n_ctx, 32), F32)], 1)
    return jnp.concatenate([cos_l, cos_c], 0), jnp.concatenate([sin_l, jnp.zeros((n_ctx, 128), F32)], 0)


def kernel(x, c, ctx, c_ctx, w_mod, b_mod, w_in, mlstm_conv, mlstm_gate_bias, mlstm_norm_w, rwkv_mu, rwkv_w0, rwkv_w_up, rwkv_a0, rwkv_a_up, rwkv_g_up, rwkv_k_k, rwkv_k_a, rwkv_r_k, rwkv_ln_w, rwkv_ln_b, mla_q_norm, mla_kv_norm, mla_w_uq, mla_w_ukv, w_out, ln1_w, ln1_b, router_w, router_bias, exp_w_gate, exp_w_up, exp_w_down, sh_w_gate, sh_w_up, sh_w_down, ln2_w, ln2_b):
    b, t_lat, d = x.shape
    n_ctx = ctx.shape[1]
    ta = t_lat + n_ctx
    depth = w_in.shape[0]
    alpha = (2 * depth) ** 0.25
    tm = 256
    cl = 256
    nv_moe = TOP_K * ta // MOE_RB + N_EXPERTS
    assert b + 1 <= 8 and n_ctx % tm == 0 and t_lat % tm == 0 and (TOP_K * ta) % MOE_RB == 0

    cc = jnp.concatenate([c, c_ctx[None], jnp.zeros((8 - b - 1, d), F32)], 0)
    mods = _modulation(cc, w_mod, b_mod)
    cos, sin = _rope_tables(t_lat, n_ctx)
    hsum = jnp.kron(jnp.eye(RWKV_H, dtype=F32), jnp.ones((RWKV_N, RWKV_N), F32))
    hmean = hsum / RWKV_N
    lane = jnp.arange(128, dtype=jnp.int32) // (128 // (b * RWKV_H))
    pats = jnp.broadcast_to((jnp.arange(8, dtype=jnp.int32) * (b * RWKV_H))[:, None, None] + lane, (8, RWKV_N, 128))

    xa = jnp.concatenate([x, ctx], axis=1)
    for l in range(depth):
        m_lat = mods[l, :b].reshape(b, 1, 6, d)
        m_ctx = jnp.broadcast_to(mods[l, b].reshape(1, 1, 6, d), (b, 1, 6, d))
        modv = jnp.concatenate([m_lat, m_ctx], axis=1)
        w_ext, wq, wqr, wk, wv = _layer_weights(l, w_in, mla_w_uq, mla_w_ukv)
        mu = rwkv_mu[l].reshape(1, N_RWKV_IN)
        zm, zg, zr, za = _in_proj(xa, modv, w_ext, mlstm_conv[l], mu, t_lat=t_lat, tm=tm)

        gates = zg[:, :, :4 * MLSTM_H].reshape(b, ta, 2, 8)
        g_col = gates.transpose(0, 2, 1, 3)
        g_row = gates.transpose(0, 2, 3, 1)
        gb = mlstm_gate_bias[l].reshape(2, 8)
        h_m = _mlstm(zm, g_col, g_row, gb.reshape(2, 1, 8), gb.reshape(2, 8, 1), t_lat=t_lat, cl=cl)

        zeros = jnp.zeros((DECAY_LORA, RWKV_W), F32)
        wup = jnp.concatenate([jnp.concatenate([rwkv_w_up[l, 0], zeros], 1),
                               jnp.concatenate([zeros, rwkv_w_up[l, 1]], 1)], 0).astype(BF16)
        aup = jnp.concatenate([jnp.concatenate([rwkv_a_up[l, 0], zeros], 1),
                               jnp.concatenate([zeros, rwkv_a_up[l, 1]], 1)], 0).astype(BF16)
        vec = jnp.concatenate([rwkv_w0[l], rwkv_a0[l], rwkv_k_k[l][None], rwkv_k_a[l][None], rwkv_r_k[l][None],
                               jnp.zeros((1, RWKV_W), F32)], 0)
        pd, v_tok, g_tok, bonus = _rwkv_prep(zr, wup, aup, rwkv_g_up[l].astype(BF16), vec, hsum, tm=128)
        rep = 128 // (b * RWKV_H)
        nvh = RWKV_N // rep
        vs = v_tok.reshape(b, ta, RWKV_H, nvh, rep).transpose(1, 3, 0, 2, 4).reshape(ta, nvh, 128)
        o_f, o_b = _rwkv_scan(pd, pats, vs, t_lat=t_lat, ts=32)
        o_s = jnp.stack([o_f, o_b], 0).reshape(2, ta, nvh, b, RWKV_H, rep)
        o_s = o_s.transpose(0, 3, 1, 4, 2, 5).reshape(2, b, ta, RWKV_W)

        q, k, vt = _mla_proj(za, cos, sin, mla_q_norm[l][None], mla_kv_norm[l][None], wq, wqr, wk, wv, tm=tm)
        att_l = _attention(q, k, vt, q_off=0, n_q=t_lat, k_off=0, n_keys=ta, tq=tm)
        att_c = _attention(q, k, vt, q_off=t_lat, n_q=n_ctx, k_off=t_lat, n_keys=n_ctx, tq=tm)

        vec_o = jnp.concatenate([mlstm_norm_w[l][None], rwkv_ln_w[l][None], rwkv_ln_b[l][None],
                                 jnp.zeros((5, RWKV_W), F32)], 0)
        ln1 = jnp.stack([ln1_w[l], ln1_b[l]], 0)
        xa = _out_proj(xa, h_m, zm, o_s, g_tok, bonus, att_l, att_c, modv, vec_o, hmean, w_out[l].astype(BF16), ln1,
                       t_lat=t_lat, tm=tm, alpha=alpha)

        ids, gts, h_rows = _router(xa, modv, router_w[l].T, router_bias[l].reshape(N_EXPERTS, 1), t_lat=t_lat, tm=tm)
        tok, dst, g_sorted, tables = _moe_tables(ids, gts, ta=ta, nv=nv_moe, nchunk=d // 128)
        routed = _moe_routed(h_rows, tok, dst, g_sorted, tables, exp_w_gate[l].astype(BF16), exp_w_up[l].astype(BF16),
                             exp_w_down[l].astype(BF16), nv=nv_moe)
        ln2 = jnp.stack([ln2_w[l], ln2_b[l]], 0)
        xa = _moe_finish(xa, routed, modv, sh_w_gate[l].astype(BF16), sh_w_up[l].astype(BF16),
                         sh_w_down[l].astype(BF16), ln2, t_lat=t_lat, tm=tm, alpha=alpha,
                         n_rows=t_lat if l == depth - 1 else ta)
    return xa
```

```python
import functools

import jax
import jax.numpy as jnp
import numpy as np
from jax import lax
from jax.experimental import pallas as pl
from jax.experimental.pallas import tpu as pltpu

F32 = jnp.float32
BF16 = jnp.bfloat16
HIGHEST = lax.Precision.HIGHEST

GRID_W = 64
MLSTM_H, MLSTM_DH = 4, 64
MLSTM_W = MLSTM_H * MLSTM_DH
RWKV_H, RWKV_N = 4, 64
RWKV_W = RWKV_H * RWKV_N
DECAY_LORA, AAA_LORA, GATE_LORA = 64, 64, 128
RWKV_GN_EPS = 64e-5
MLA_H, MLA_NOPE, MLA_ROPE, MLA_V = 8, 64, 32, 64
Q_RANK, KV_RANK = 256, 128
ROPE_THETA = 10000.0
N_MLSTM_IN = 4 * MLSTM_W + 4 * MLSTM_H
N_RWKV_IN = 3 * RWKV_W + 2 * DECAY_LORA + 2 * AAA_LORA + GATE_LORA
N_EXPERTS, TOP_K, N_GROUPS, TOPK_GROUPS = 64, 8, 8, 4
ROUTED_SCALE = 2.5
LN_EPS = 1e-6
NEG = -1e30

C_QK, C_R, C_V, C_O, C_G, C_A = 0, 512, 1664, 1920, 2176, 2304
N_SHIFT = 1664
N_A = 640
N_EXT = C_A + N_A
HALO = 8
VMEM_LIMIT = 56 * 1024 * 1024


def _cp(sem):
    return pltpu.CompilerParams(dimension_semantics=sem, vmem_limit_bytes=VMEM_LIMIT)


def _ln(x):
    mu = jnp.mean(x, axis=-1, keepdims=True)
    xc = x - mu
    return xc * lax.rsqrt(jnp.mean(xc * xc, axis=-1, keepdims=True) + LN_EPS)


def _sigmoid(x):
    return 1.0 / (1.0 + jnp.exp(-x))


def _silu(x):
    return x * _sigmoid(x)


def _log_sigmoid(x):
    return jnp.minimum(x, 0.0) - jnp.log(1.0 + jnp.exp(-jnp.abs(x)))


def _softplus(x):
    return jnp.maximum(x, 0.0) + jnp.log(1.0 + jnp.exp(-jnp.abs(x)))


def _nt_dot(a, b, **kw):
    return lax.dot_general(a, b, (((1,), (1,)), ((), ())), preferred_element_type=F32, **kw)


def _tn_dot(a, b, **kw):
    return lax.dot_general(a, b, (((0,), (0,)), ((), ())), preferred_element_type=F32, **kw)


def _dot(a, b, **kw):
    return jnp.dot(a, b, preferred_element_type=F32, **kw)


def _mod_kernel(c_ref, w_ref, b_ref, o_ref):
    o_ref[0] = _dot(_silu(c_ref[...]), w_ref[0], precision=HIGHEST) + b_ref[0]


def _modulation(cc, w_mod, b_mod):
    n_layers, d, n = w_mod.shape
    tn = 1536
    return pl.pallas_call(
        _mod_kernel,
        grid=(n_layers, n // tn),
        in_specs=[pl.BlockSpec((8, d), lambda l, j: (0, 0)),
                  pl.BlockSpec((1, d, tn), lambda l, j: (l, 0, j)),
                  pl.BlockSpec((1, 1, tn), lambda l, j: (l, 0, j))],
        out_specs=pl.BlockSpec((1, 8, tn), lambda l, j: (l, 0, j)),
        out_shape=jax.ShapeDtypeStruct((n_layers, 8, n), F32),
        compiler_params=_cp(("parallel", "parallel")),
        name="modulation",
    )(cc, w_mod, b_mod.reshape(n_layers, 1, n))


def _in_kernel(xp_ref, x_ref, xn_ref, mod_ref, w_ref, conv_ref, mu_ref, zm_ref, zg_ref, zr_ref, za_ref, *,
               tm, nt_lat, nt):
    i = pl.program_id(1)
    has_prev = jnp.logical_and(i != 0, i != nt_lat)
    has_next = jnp.logical_and(i != nt_lat - 1, i != nt - 1)
    xt = jnp.concatenate([xp_ref[0], x_ref[0], xn_ref[0]], axis=0)
    sh = mod_ref[0, 0, 0:1, :]
    sc = mod_ref[0, 0, 1:2, :]
    h = (_ln(xt) * (1.0 + sc) + sh).astype(BF16)
    z = _dot(h, w_ref[...])
    rows = lax.broadcasted_iota(jnp.int32, (tm + 2 * HALO, 1), 0)
    lo = jnp.where(has_prev, 0, HALO)
    hi = jnp.where(has_next, tm + 2 * HALO, tm + HALO)
    keep = jnp.logical_and(rows >= lo, rows < hi)
    zs = jnp.where(keep, z[:, :N_SHIFT], 0.0)
    zc = zs[HALO:HALO + tm]
    zprev = pltpu.roll(zs, 1, axis=0)[HALO:HALO + tm]
    znext = pltpu.roll(zs, tm + 2 * HALO - 1, axis=0)[HALO:HALO + tm]
    cw = conv_ref[...]
    qk = (cw[0:1] * zprev[:, :C_R] + cw[1:2] * zc[:, :C_R] + cw[2:3] * znext[:, :C_R])
    qk = _silu(qk)
    lane = lax.broadcasted_iota(jnp.int32, (1, C_R), 1)
    qk = qk * jnp.where(lane < MLSTM_W, MLSTM_DH ** -0.5, 1.0)
    zm_ref[0, :, 0:512] = qk
    zm_ref[0, :, 512:1024] = z[HALO:HALO + tm, C_V:C_G]
    zg_ref[0] = z[HALO:HALO + tm, C_G:C_A]
    zr = zc[:, C_R:]
    zr_ref[0] = zr + mu_ref[...] * (0.5 * (zprev[:, C_R:] + znext[:, C_R:]) - zr)
    za_ref[0] = z[HALO:HALO + tm, C_A:]


def _in_proj(xa, modv, w_ext, conv, mu, *, t_lat, tm):
    b, ta, d = xa.shape
    nt, nt_lat = ta // tm, t_lat // tm
    nh = tm // HALO
    last = ta // HALO - 1
    kern = functools.partial(_in_kernel, tm=tm, nt_lat=nt_lat, nt=nt)
    return pl.pallas_call(
        kern,
        grid=(b, nt),
        in_specs=[pl.BlockSpec((1, HALO, d), lambda bi, i: (bi, jnp.maximum(i * nh - 1, 0), 0)),
                  pl.BlockSpec((1, tm, d), lambda bi, i: (bi, i, 0)),
                  pl.BlockSpec((1, HALO, d), lambda bi, i: (bi, jnp.minimum((i + 1) * nh, last), 0)),
                  pl.BlockSpec((1, 1, 6, d), lambda bi, i: (bi, i // nt_lat, 0, 0)),
                  pl.BlockSpec((d, N_EXT), lambda bi, i: (0, 0)),
                  pl.BlockSpec((3, C_R), lambda bi, i: (0, 0)),
                  pl.BlockSpec((1, N_RWKV_IN), lambda bi, i: (0, 0))],
        out_specs=[pl.BlockSpec((1, tm, 1024), lambda bi, i: (bi, i, 0)),
                   pl.BlockSpec((1, tm, 128), lambda bi, i: (bi, i, 0)),
                   pl.BlockSpec((1, tm, N_RWKV_IN), lambda bi, i: (bi, i, 0)),
                   pl.BlockSpec((1, tm, N_A), lambda bi, i: (bi, i, 0))],
        out_shape=[jax.ShapeDtypeStruct((b, ta, 1024), F32),
                   jax.ShapeDtypeStruct((b, ta, 128), F32),
                   jax.ShapeDtypeStruct((b, ta, N_RWKV_IN), F32),
                   jax.ShapeDtypeStruct((b, ta, N_A), F32)],
        compiler_params=_cp(("parallel", "parallel")),
        name="in_proj",
    )(xa, xa, xa, modv, w_ext, conv, mu)


def _mlstm_kernel(zm_ref, gc_ref, gr_ref, bc_ref, br_ref, o_ref, ct_ref, n_ref, m_ref, *, cl):
    d = pl.program_id(1)

    @pl.when(pl.program_id(2) == 0)
    def _():
        ct_ref[...] = jnp.zeros_like(ct_ref)
        n_ref[...] = jnp.zeros_like(n_ref)
        m_ref[...] = jnp.zeros_like(m_ref)

    sgn = 1 - 2 * d
    gc = gc_ref[0, 0] + bc_ref[0]
    gr = gr_ref[0, 0] + br_ref[0]
    lf_c = _log_sigmoid(gc)
    lf_r = _log_sigmoid(gr)
    ii = lax.broadcasted_iota(jnp.int32, (cl, cl), 0)
    jj = lax.broadcasted_iota(jnp.int32, (cl, cl), 1)
    mask = (jj - ii) * sgn <= 0
    mask_t = (ii - jj) * sgn <= 0
    cum_c = _dot(mask.astype(F32), lf_c, precision=HIGHEST)
    cum_r = _dot(lf_r, mask_t.astype(F32), precision=HIGHEST)
    outs = []
    for h in range(MLSTM_H):
        sl = slice(h * MLSTM_DH, (h + 1) * MLSTM_DH)
        q = zm_ref[0, :, sl]
        k = zm_ref[0, :, MLSTM_W + h * MLSTM_DH:MLSTM_W + (h + 1) * MLSTM_DH]
        v = zm_ref[0, :, 2 * MLSTM_W + h * MLSTM_DH:2 * MLSTM_W + (h + 1) * MLSTM_DH]
        qb, kb = q.astype(BF16), k.astype(BF16)
        li_c, li_r = gc[:, h:h + 1], gr[h:h + 1, :]
        cu_c, cu_r = cum_c[:, 4 + h:5 + h], cum_r[4 + h:5 + h, :]
        b_end = jnp.sum(lf_r[4 + h:5 + h, :], axis=1, keepdims=True)
        m_prev = m_ref[h:h + 1, 0:1]
        ct = ct_ref[h]
        nv = n_ref[h:h + 1, 0:MLSTM_DH]
        d_log = jnp.where(mask, cu_c - cu_r + li_r, NEG)
        g_log = cu_c + m_prev
        m_row = jnp.maximum(g_log, jnp.max(d_log, axis=1, keepdims=True))
        w_intra = jnp.exp(d_log - m_row) * _nt_dot(qb, kb)
        e_inter = jnp.exp(g_log - m_row)
        num = _dot(w_intra.astype(BF16), v.astype(BF16)) + e_inter * _dot(qb, ct.astype(BF16))
        den = jnp.sum(w_intra, axis=1, keepdims=True) + e_inter * jnp.sum(q * nv, axis=1, keepdims=True)
        outs.append(num / jnp.maximum(jnp.abs(den), jnp.exp(-m_row)))
        w_end = b_end - cu_c + li_c
        m_loc = jnp.max(w_end, axis=0, keepdims=True)
        e_end = jnp.exp(w_end - m_loc)
        m_new = jnp.maximum(b_end + m_prev, m_loc)
        a_old = jnp.exp(b_end + m_prev - m_new)
        a_loc = jnp.exp(m_loc - m_new)
        ct_ref[h] = a_old * ct + a_loc * _tn_dot(kb, (v * e_end).astype(BF16))
        n_ref[h:h + 1, 0:MLSTM_DH] = a_old * nv + a_loc * jnp.sum(k * e_end, axis=0, keepdims=True)
        m_ref[h:h + 1, :] = jnp.broadcast_to(m_new, (1, 128))
    o_ref[0, 0] = jnp.concatenate(outs, axis=1)


def _mlstm(zm, g_col, g_row, b_col, b_row, *, t_lat, cl):
    b, ta, _ = zm.shape
    nc, nc_lat = ta // cl, t_lat // cl
    nc_ctx = nc - nc_lat

    def chunk(di, c):
        fwd_idx = jnp.where(c < nc_ctx, nc_lat + c, c - nc_ctx)
        return jnp.where(di == 0, fwd_idx, nc - 1 - c)

    return pl.pallas_call(
        functools.partial(_mlstm_kernel, cl=cl),
        grid=(b, 2, nc),
        in_specs=[pl.BlockSpec((1, cl, 1024), lambda bi, di, c: (bi, chunk(di, c), 0)),
                  pl.BlockSpec((1, 1, cl, 8), lambda bi, di, c: (bi, di, chunk(di, c), 0)),
                  pl.BlockSpec((1, 1, 8, cl), lambda bi, di, c: (bi, di, 0, chunk(di, c))),
                  pl.BlockSpec((1, 1, 8), lambda bi, di, c: (di, 0, 0)),
                  pl.BlockSpec((1, 8, 1), lambda bi, di, c: (di, 0, 0))],
        out_specs=pl.BlockSpec((1, 1, cl, MLSTM_W), lambda bi, di, c: (bi, di, chunk(di, c), 0)),
        out_shape=jax.ShapeDtypeStruct((b, 2, ta, MLSTM_W), F32),
        scratch_shapes=[pltpu.VMEM((MLSTM_H, MLSTM_DH, MLSTM_DH), F32),
                        pltpu.VMEM((8, 128), F32),
                        pltpu.VMEM((8, 128), F32)],
        compiler_params=_cp(("parallel", "parallel", "arbitrary")),
        name="mlstm",
    )(zm, g_col, g_row, b_col, b_row)


def _rwkv_prep_kernel(zr_ref, wup_ref, aup_ref, gup_ref, vec_ref, hsum_ref, pd_ref, v_ref, g_ref, bonus_ref):
    nb, tm = zr_ref.shape[0], zr_ref.shape[1]
    nbh = nb * RWKV_H
    vec = vec_ref[...]
    hsum = hsum_ref[...]
    zero = jnp.zeros((tm, RWKV_N), F32)
    for bi in range(nb):
        z = zr_ref[bi]
        zr_, zk, zv = z[:, 0:256], z[:, 256:512], z[:, 512:768]
        zw, za, zg = z[:, 768:896], z[:, 896:1024], z[:, 1024:1152]
        w_raw = _dot(jnp.tanh(zw).astype(BF16), wup_ref[...])
        a_raw = _dot(za.astype(BF16), aup_ref[...])
        g_ref[bi] = _dot(_sigmoid(zg).astype(BF16), gup_ref[...])
        kk = zk * vec[4:5]
        kk_ss = _dot(kk * kk, hsum, precision=HIGHEST)
        kk = kk * lax.rsqrt(jnp.maximum(kk_ss, 1e-12))
        k_sum = jnp.zeros_like(zk)
        for di in range(2):
            sl = slice(di * RWKV_W, (di + 1) * RWKV_W)
            decay = jnp.exp(-jnp.exp(-_softplus(-(vec[di:di + 1] + w_raw[:, sl])) - 0.5))
            a = _sigmoid(vec[2 + di:3 + di] + a_raw[:, sl])
            k_dir = zk * (1.0 + (a - 1.0) * vec[5:6])
            k_sum = k_sum + k_dir
            ka = kk * a
            for h in range(RWKV_H):
                hs = slice(h * RWKV_N, (h + 1) * RWKV_N)
                rows = pl.ds(bi * RWKV_H + h, tm, stride=nbh)
                pd_ref[di, 0, rows, :] = jnp.concatenate([decay[:, hs], ka[:, hs]], axis=1)
                pd_ref[di, 1, rows, :] = jnp.concatenate([k_dir[:, hs], kk[:, hs]], axis=1)
                pd_ref[di, 2, rows, :] = jnp.concatenate([zr_[:, hs], zero], axis=1)
        v_ref[bi] = zv
        bonus_ref[bi] = _dot(zr_ * vec[6:7] * k_sum, hsum, precision=HIGHEST) * zv


def _rwkv_prep(zr, wup, aup, gup, vec, hsum, *, tm):
    b, ta, _ = zr.shape
    nbh = b * RWKV_H
    tok = pl.BlockSpec((b, tm, RWKV_W), lambda i: (0, i, 0))
    full = lambda a: pl.BlockSpec(a.shape, lambda i: (0,) * a.ndim)
    return pl.pallas_call(
        _rwkv_prep_kernel,
        grid=(ta // tm,),
        in_specs=[pl.BlockSpec((b, tm, N_RWKV_IN), lambda i: (0, i, 0)),
                  full(wup), full(aup), full(gup), full(vec), full(hsum)],
        out_specs=[pl.BlockSpec((2, 3, tm * nbh, 128), lambda i: (0, 0, i, 0)), tok, tok, tok],
        out_shape=[jax.ShapeDtypeStruct((2, 3, ta * nbh, 128), F32)] + [jax.ShapeDtypeStruct((b, ta, RWKV_W), F32)] * 3,
        compiler_params=_cp(("parallel",)),
        name="rwkv_prep",
    )(zr, wup, aup, gup, vec, hsum)


def _rwkv_scan_kernel(pf_ref, pb_ref, pat_ref, vf_ref, vb_ref, of_ref, ob_ref, st_ref, x_ref, tt_ref, *, ts, nvh, nbh):
    ng = ts // 8
    rows = 8 * nbh

    @pl.when(pl.program_id(0) == 0)
    def _():
        st_ref[...] = jnp.zeros_like(st_ref)

    def transpose(di, p_ref, grp):
        sl = pl.ds(pl.multiple_of(grp * rows, rows), rows)
        for j in range(3):
            tt_ref[di, j] = p_ref[0, j, sl, :].T

    def update(di, u, v_ref, o_ref, t):
        pat = pat_ref[u]
        for j, (tj, half) in enumerate(((0, 0), (0, 1), (1, 0), (1, 1), (2, 0))):
            x_ref[di,j] = jnp.take_along_axis(tt_ref[di, tj, half * RWKV_N:(half + 1) * RWKV_N, :], pat, axis=1)
        vt = v_ref[t]
        nkb = RWKV_N // 8
        kblk = lambda kb: pl.ds(kb * 8, 8)
        acc = [None] * nvh
        for kb in range(nkb):
            kk = x_ref[di,3, kblk(kb), :]
            for vh in range(nvh):
                term = st_ref[di, vh, kblk(kb), :] * kk
                acc[vh] = term if kb == 0 else acc[vh] + term
        sa = [jnp.sum(a, axis=0, keepdims=True) for a in acc]
        out = [None] * nvh
        for kb in range(nkb):
            w, ka = x_ref[di,0, kblk(kb), :], x_ref[di,1, kblk(kb), :]
            k, r = x_ref[di,2, kblk(kb), :], x_ref[di,4, kblk(kb), :]
            for vh in range(nvh):
                sv = st_ref[di, vh, kblk(kb), :] * w - sa[vh] * ka + vt[vh:vh + 1, :] * k
                st_ref[di, vh, kblk(kb), :] = sv
                out[vh] = sv * r if kb == 0 else out[vh] + sv * r
        o_ref[t] = jnp.concatenate([jnp.sum(o, axis=0, keepdims=True) for o in out], axis=0)

    def group(g, carry):
        transpose(0, pf_ref, g)
        transpose(1, pb_ref, ng - 1 - g)
        for u in range(8):
            update(0, u, vf_ref, of_ref, g * 8 + u)
            update(1, 7 - u, vb_ref, ob_ref, ts - 1 - (g * 8 + u))
        return carry

    lax.fori_loop(0, ng, group, 0)


def _rwkv_scan(pd, pats, vs, *, t_lat, ts):
    ta, nvh, nl = vs.shape
    nbh = pd.shape[2] // ta
    nt, nt_lat = ta // ts, t_lat // ts
    fwd = lambda i: (i + nt_lat) % nt
    bwd = lambda i: nt - 1 - i
    return pl.pallas_call(
        functools.partial(_rwkv_scan_kernel, ts=ts, nvh=nvh, nbh=nbh),
        grid=(nt,),
        in_specs=[pl.BlockSpec((1, 3, ts * nbh, 128), lambda i: (0, 0, fwd(i), 0)),
                  pl.BlockSpec((1, 3, ts * nbh, 128), lambda i: (1, 0, bwd(i), 0)),
                  pl.BlockSpec(pats.shape, lambda i: (0, 0, 0)),
                  pl.BlockSpec((ts, nvh, nl), lambda i: (fwd(i), 0, 0)),
                  pl.BlockSpec((ts, nvh, nl), lambda i: (bwd(i), 0, 0))],
        out_specs=[pl.BlockSpec((ts, nvh, nl), lambda i: (fwd(i), 0, 0)),
                   pl.BlockSpec((ts, nvh, nl), lambda i: (bwd(i), 0, 0))],
        out_shape=[jax.ShapeDtypeStruct((ta, nvh, nl), F32)] * 2,
        scratch_shapes=[pltpu.VMEM((2, nvh, RWKV_N, nl), F32), pltpu.VMEM((2, 5, RWKV_N, nl), F32),
                        pltpu.VMEM((2, 3, 128, 8 * nbh), F32)],
        compiler_params=_cp(("arbitrary",)),
        name="rwkv_scan",
    )(pd, pd, pats, vs, vs)


def _mla_proj_kernel(za_ref, cos_ref, sin_ref, nq_ref, nkv_ref, wq_ref, wqr_ref, wk_ref, wv_ref, q_ref, k_ref, v_ref):
    za = za_ref[0]
    cq, ckv = za[:, 0:Q_RANK], za[:, Q_RANK:Q_RANK + KV_RANK]
    kr, krr = za[:, 384:512], za[:, 512:640]
    cos, sin = cos_ref[...], sin_ref[...]
    cqn = (cq * lax.rsqrt(jnp.mean(cq * cq, axis=-1, keepdims=True) + 1e-6) * nq_ref[...]).astype(BF16)
    ckvn = (ckv * lax.rsqrt(jnp.mean(ckv * ckv, axis=-1, keepdims=True) + 1e-6) * nkv_ref[...]).astype(BF16)
    q = _dot(cqn, wq_ref[...])
    qr = _dot(cqn, wqr_ref[...])
    kn = _dot(ckvn, wk_ref[...])
    v_ref[0] = _nt_dot(wv_ref[...], ckvn).astype(BF16)
    k_rope = kr * cos + krr * sin
    scale = (MLA_NOPE + MLA_ROPE) ** -0.5 * np.log2(np.e)
    for h in range(MLA_H):
        sl = slice(h * 128, (h + 1) * 128)
        q_ref[0, :, sl] = ((q[:, sl] * cos + qr[:, sl] * sin) * scale).astype(BF16)
        k_ref[0, :, sl] = (kn[:, sl] + k_rope).astype(BF16)


def _mla_proj(za, cos, sin, nq, nkv, wq, wqr, wk, wv, *, tm):
    b, ta, _ = za.shape
    full = lambda a: pl.BlockSpec(a.shape, lambda bi, i: (0,) * a.ndim)
    return pl.pallas_call(
        _mla_proj_kernel,
        grid=(b, ta // tm),
        in_specs=[pl.BlockSpec((1, tm, N_A), lambda bi, i: (bi, i, 0)),
                  pl.BlockSpec((tm, 128), lambda bi, i: (i, 0)),
                  pl.BlockSpec((tm, 128), lambda bi, i: (i, 0)),
                  full(nq), full(nkv), full(wq), full(wqr), full(wk), full(wv)],
        out_specs=[pl.BlockSpec((1, tm, 1024), lambda bi, i: (bi, i, 0)),
                   pl.BlockSpec((1, tm, 1024), lambda bi, i: (bi, i, 0)),
                   pl.BlockSpec((1, 512, tm), lambda bi, i: (bi, 0, i))],
        out_shape=[jax.ShapeDtypeStruct((b, ta, 1024), BF16),
                   jax.ShapeDtypeStruct((b, ta, 1024), BF16),
                   jax.ShapeDtypeStruct((b, 512, ta), BF16)],
        compiler_params=_cp(("parallel", "parallel")),
        name="mla_proj",
    )(za, cos, sin, nq, nkv, wq, wqr, wk, wv)


def _attn_kernel(q_ref, k_ref, vt_ref, o_ref):
    outs = []
    for h in range(ATTN_HEADS):
        sl = slice(h * 128, (h + 1) * 128)
        pair = h // 2
        st = _nt_dot(k_ref[0, :, sl], q_ref[0, :, sl])
        m = jnp.max(st, axis=0, keepdims=True)
        p = jnp.exp2(st - m)
        l = jnp.sum(p, axis=0, keepdims=True)
        ot = _dot(vt_ref[0, pair * 128:(pair + 1) * 128, :], p.astype(BF16))
        outs.append(ot[(h % 2) * MLA_V:(h % 2 + 1) * MLA_V] / l)
    o_ref[0] = jnp.concatenate(outs, axis=0).T


ATTN_HEADS = 4


def _attention(q, k, vt, *, q_off, n_q, k_off, n_keys, tq):
    b = q.shape[0]
    qo, ko = q_off // tq, k_off // n_keys
    hq, hv = ATTN_HEADS * 128, ATTN_HEADS * MLA_V
    return pl.pallas_call(
        _attn_kernel,
        grid=(b, MLA_H // ATTN_HEADS, n_q // tq),
        in_specs=[pl.BlockSpec((1, tq, hq), lambda bi, hp, i: (bi, qo + i, hp)),
                  pl.BlockSpec((1, n_keys, hq), lambda bi, hp, i: (bi, ko, hp)),
                  pl.BlockSpec((1, hv, n_keys), lambda bi, hp, i: (bi, hp, ko))],
        out_specs=pl.BlockSpec((1, tq, hv), lambda bi, hp, i: (bi, i, hp)),
        out_shape=jax.ShapeDtypeStruct((b, n_q, 512), F32),
        compiler_params=_cp(("parallel", "parallel", "parallel")),
        name="mla_attention",
    )(q, k, vt)


def _out_kernel(x_ref, hf_ref, hb_ref, zm_ref, of_ref, ob_ref, g_ref, bonus_ref, attl_ref, attc_ref, mod_ref, vec_ref,
                hmean_ref, wo_ref, ln_ref, o_ref, *, alpha, nt_lat):
    hmean = hmean_ref[...]
    att = jnp.where(pl.program_id(1) < nt_lat, attl_ref[0], attc_ref[0])

    def head_norm(y, eps):
        mu = _dot(y, hmean, precision=HIGHEST)
        yc = y - mu
        return yc * lax.rsqrt(_dot(yc * yc, hmean, precision=HIGHEST) + eps)

    vec = vec_ref[...]
    zo = zm_ref[0, :, 768:1024]
    m_mix = _sigmoid(zo) * (head_norm(hf_ref[0, 0] + hb_ref[0, 0], 1e-6) * vec[0:1])
    r_o = head_norm(of_ref[0, 0] + ob_ref[0, 0], RWKV_GN_EPS) * vec[1:2] + vec[2:3] + bonus_ref[0]
    r_mix = r_o * g_ref[0]
    mix = jnp.concatenate([m_mix, r_mix, att], axis=1).astype(BF16)
    y = _dot(mix, wo_ref[...])
    gate = mod_ref[0, 0, 2:3, :]
    o_ref[0] = _ln(alpha * x_ref[0] + gate * y) * ln_ref[0:1] + ln_ref[1:2]


def _out_proj(xa, h_m, zm, o_r, g, bonus, att_l, att_c, modv, vec, hmean, wo, ln, *, t_lat, tm, alpha):
    b, ta, d = xa.shape
    nt_lat = t_lat // tm
    tok = lambda w: pl.BlockSpec((1, tm, w), lambda bi, i: (bi, i, 0))
    full = lambda a: pl.BlockSpec(a.shape, lambda bi, i: (0,) * a.ndim)
    return pl.pallas_call(
        functools.partial(_out_kernel, alpha=alpha, nt_lat=nt_lat),
        grid=(b, ta // tm),
        in_specs=[tok(d),
                  pl.BlockSpec((1, 1, tm, MLSTM_W), lambda bi, i: (bi, 0, i, 0)),
                  pl.BlockSpec((1, 1, tm, MLSTM_W), lambda bi, i: (bi, 1, i, 0)),
                  tok(1024),
                  pl.BlockSpec((1, 1, tm, RWKV_W), lambda bi, i: (0, bi, i, 0)),
                  pl.BlockSpec((1, 1, tm, RWKV_W), lambda bi, i: (1, bi, i, 0)),
                  tok(RWKV_W), tok(RWKV_W),
                  pl.BlockSpec((1, tm, 512), lambda bi, i: (bi, jnp.minimum(i, nt_lat - 1), 0)),
                  pl.BlockSpec((1, tm, 512), lambda bi, i: (bi, jnp.maximum(i - nt_lat, 0), 0)),
                  pl.BlockSpec((1, 1, 6, d), lambda bi, i: (bi, i // nt_lat, 0, 0)),
                  full(vec), full(hmean), full(wo), full(ln)],
        out_specs=tok(d),
        out_shape=jax.ShapeDtypeStruct((b, ta, d), F32),
        compiler_params=_cp(("parallel", "parallel")),
        name="out_proj",
    )(xa, h_m, h_m, zm, o_r, o_r, g, bonus, att_l, att_c, modv, vec, hmean, wo, ln)


def _first_max(x, idx, axis):
    mx = jnp.max(x, axis=axis, keepdims=True)
    first = jnp.min(jnp.where(x == mx, idx, 1 << 20), axis=axis, keepdims=True)
    return mx, first, idx == first


def _router_kernel(x_ref, mod_ref, w_ref, bias_ref, id_ref, gate_ref, h_ref):
    tm = x_ref.shape[1]
    sh, sc = mod_ref[0, 0, 3:4, :], mod_ref[0, 0, 4:5, :]
    h = _ln(x_ref[0]) * (1.0 + sc) + sh
    for j in range(h.shape[1] // 128):
        h_ref[0, pl.ds(j, tm, stride=h.shape[1] // 128), :] = h[:, j * 128:(j + 1) * 128]
    scores = _sigmoid(_nt_dot(w_ref[...], h, precision=HIGHEST))
    biased = scores + bias_ref[...]
    gsz = N_EXPERTS // N_GROUPS
    b3 = biased.reshape(N_GROUPS, gsz, tm)
    s3 = scores.reshape(N_GROUPS, gsz, tm)
    e_idx = lax.broadcasted_iota(jnp.int32, (N_GROUPS, gsz, tm), 1)
    m1, _, hit = _first_max(b3, e_idx, 1)
    m2 = jnp.max(jnp.where(hit, -jnp.inf, b3), axis=1, keepdims=True)
    gscore = m1 + m2
    g_idx = lax.broadcasted_iota(jnp.int32, (N_GROUPS, 1, tm), 0)
    gsel = jnp.zeros((N_GROUPS, 1, tm), F32)
    for _ in range(TOPK_GROUPS):
        _, _, hit = _first_max(jnp.where(gsel > 0, -jnp.inf, gscore), g_idx, 0)
        gsel = jnp.where(hit, 1.0, gsel)
    cand = jnp.where(jnp.broadcast_to(gsel, b3.shape) > 0, b3, -jnp.inf)
    x_idx = g_idx * gsz + e_idx
    sel = jnp.zeros(b3.shape, F32)
    ids, picked = [], []
    for _ in range(TOP_K):
        _, first, hit = _first_max(jnp.where(sel > 0, -jnp.inf, cand), x_idx, (0, 1))
        sel = jnp.where(hit, 1.0, sel)
        ids.append(first[0])
        picked.append(jnp.sum(jnp.where(hit, s3, 0.0), axis=(0, 1), keepdims=True)[0])
    picked = jnp.concatenate(picked, axis=0)
    id_ref[0] = jnp.concatenate(ids, axis=0)
    gate_ref[0] = ROUTED_SCALE * picked / jnp.sum(picked, axis=0, keepdims=True)


def _router(xa, modv, w_t, bias, *, t_lat, tm):
    b, ta, d = xa.shape
    nt_lat = t_lat // tm
    return pl.pallas_call(
        _router_kernel,
        grid=(b, ta // tm),
        in_specs=[pl.BlockSpec((1, tm, d), lambda bi, i: (bi, i, 0)),
                  pl.BlockSpec((1, 1, 6, d), lambda bi, i: (bi, i // nt_lat, 0, 0)),
                  pl.BlockSpec((N_EXPERTS, d), lambda bi, i: (0, 0)),
                  pl.BlockSpec((N_EXPERTS, 1), lambda bi, i: (0, 0))],
        out_specs=[pl.BlockSpec((1, TOP_K, tm), lambda bi, i: (bi, 0, i)),
                   pl.BlockSpec((1, TOP_K, tm), lambda bi, i: (bi, 0, i)),
                   pl.BlockSpec((1, tm * (d // 128), 128), lambda bi, i: (bi, i, 0))],
        out_shape=[jax.ShapeDtypeStruct((b, TOP_K, ta), jnp.int32),
                   jax.ShapeDtypeStruct((b, TOP_K, ta), F32),
                   jax.ShapeDtypeStruct((b, ta * (d // 128), 128), F32)],
        compiler_params=_cp(("parallel", "parallel")),
        name="router",
    )(xa, modv, w_t, bias)


MOE_RB = 256
MOE_PITCH = MOE_RB + 8
MOE_U = 8


def _moe_routed_kernel(vblk_ref, vexp_ref, vlo_ref, vhi_ref, vvalid_ref, vfirst_ref,
                       tok_ref, dst_ref, gate_ref, src_ref, wg_ref, wu_ref, wd_ref, acc_ref, xt_ref, yt_ref, xb_ref,
                       gc_ref, *, nv, nchunk):
    bi, v = pl.program_id(0), pl.program_id(1)
    pos = bi * nv + v
    lo, hi = vlo_ref[pos], vhi_ref[pos]

    @pl.when(v == 0)
    def _():
        acc_ref[...] = jnp.zeros_like(acc_ref)

    @pl.when(vfirst_ref[pos] == 1)
    def _():
        for r in range(MOE_RB):
            off = pl.multiple_of(tok_ref[0, 0, 0, r], nchunk)
            xt_ref[pl.ds(r, nchunk, stride=MOE_PITCH), :] = src_ref[0, pl.ds(off, nchunk), :]
        for j in range(nchunk):
            xb_ref[:, j * 128:(j + 1) * 128] = xt_ref[j * MOE_PITCH:j * MOE_PITCH + MOE_RB, :].astype(BF16)
        ii = lax.broadcasted_iota(jnp.int32, (MOE_RB, MOE_RB), 0)
        jj = lax.broadcasted_iota(jnp.int32, (MOE_RB, MOE_RB), 1)
        gc_ref[...] = jnp.sum(jnp.where(ii == jj, gate_ref[0, 0], 0.0), axis=1, keepdims=True)

    @pl.when(vvalid_ref[pos] == 1)
    def _():
        x = xb_ref[...]
        row = lax.broadcasted_iota(jnp.int32, (MOE_RB, 1), 0)
        gcol = jnp.where(jnp.logical_and(row >= lo, row < hi), gc_ref[...], 0.0)
        a = _silu(_dot(x, wg_ref[0])) * _dot(x, wu_ref[0]) * gcol
        y = _dot(a.astype(BF16), wd_ref[0])
        for j in range(nchunk):
            yt_ref[j * MOE_PITCH:j * MOE_PITCH + MOE_RB, :] = y[:, j * 128:(j + 1) * 128]
        for g in range(MOE_RB // MOE_U):
            pending = []
            for u in range(MOE_U):
                r = g * MOE_U + u
                off = pl.multiple_of(dst_ref[0, 0, 0, r], nchunk)
                pending.append((off, acc_ref[0, pl.ds(off, nchunk), :] + yt_ref[pl.ds(r, nchunk, stride=MOE_PITCH), :]))
            for off, val in pending:
                acc_ref[0, pl.ds(off, nchunk), :] = val


def _moe_routed(h_rows, tok, dst, gate, tables, wg, wu, wd, *, nv, layer):
    b, rows, _ = h_rows.shape
    d, de = wg.shape[1], wg.shape[2]
    nchunk = d // 128
    nblk = tok.shape[1] // MOE_RB
    blk = lambda bi, v, vblk, *_: (bi, vblk[bi * nv + v], 0, 0)
    exp = lambda bi, v, vblk, vexp, *_: (layer * N_EXPERTS + vexp[bi * nv + v], 0, 0)
    grid_spec = pltpu.PrefetchScalarGridSpec(
        num_scalar_prefetch=6,
        grid=(b, nv),
        in_specs=[pl.BlockSpec((1, 1, 1, MOE_RB), blk, memory_space=pltpu.SMEM),
                  pl.BlockSpec((1, 1, 1, MOE_RB), lambda bi, v, *_: (bi, v, 0, 0), memory_space=pltpu.SMEM),
                  pl.BlockSpec((1, 1, 1, MOE_RB), blk),
                  pl.BlockSpec((1, rows, 128), lambda bi, v, *_: (bi, 0, 0), pipeline_mode=pl.Buffered(1)),
                  pl.BlockSpec((1, d, de), exp),
                  pl.BlockSpec((1, d, de), exp),
                  pl.BlockSpec((1, de, d), exp)],
        out_specs=pl.BlockSpec((1, rows + MOE_U * nchunk, 128), lambda bi, v, *_: (bi, 0, 0),
                               pipeline_mode=pl.Buffered(1)),
        scratch_shapes=[pltpu.VMEM((nchunk * MOE_PITCH, 128), F32), pltpu.VMEM((nchunk * MOE_PITCH, 128), F32),
                        pltpu.VMEM((MOE_RB, d), BF16), pltpu.VMEM((MOE_RB, 1), F32)],
    )
    return pl.pallas_call(
        functools.partial(_moe_routed_kernel, nv=nv, nchunk=nchunk),
        grid_spec=grid_spec,
        out_shape=jax.ShapeDtypeStruct((b, rows + MOE_U * nchunk, 128), F32),
        compiler_params=_cp(("parallel", "arbitrary")),
        name="moe_routed",
    )(*tables, tok.reshape(b, nblk, 1, MOE_RB), dst.reshape(b, nv, 1, MOE_RB), gate.reshape(b, nblk, 1, MOE_RB),
      h_rows, wg, wu, wd)


def _moe_tables(ids, gates, *, ta, nv, nchunk):
    b = ids.shape[0]
    n = TOP_K * ta
    nblk = n // MOE_RB
    t_idx = jnp.broadcast_to(jnp.arange(ta, dtype=jnp.int32), (b, TOP_K, ta))
    keys, g_sorted = lax.sort(((ids * ta + t_idx).reshape(b, n), gates.reshape(b, n)), dimension=1, num_keys=1)
    tok, e_sorted = keys % ta, keys // ta
    ends = jnp.sum(e_sorted[:, None, :] <= jnp.arange(N_EXPERTS, dtype=jnp.int32)[None, :, None], axis=2)
    starts = ends - jnp.sum(e_sorted[:, None, :] == jnp.arange(N_EXPERTS, dtype=jnp.int32)[None, :, None], axis=2)
    e_lo, e_hi = e_sorted[:, ::MOE_RB], e_sorted[:, MOE_RB - 1::MOE_RB]
    n_vis = e_hi - e_lo + 1
    v_end = jnp.cumsum(n_vis, axis=1)
    v_start = v_end - n_vis
    v = jnp.arange(nv, dtype=jnp.int32)
    blk = jnp.minimum(jnp.sum(v_end[:, None, :] <= v[None, :, None], axis=2), nblk - 1).astype(jnp.int32)
    take = lambda a, i: jnp.take_along_axis(a, i, axis=1)
    valid = v[None, :] < v_end[:, -1:]
    exp = jnp.where(valid, take(e_lo, blk) + v[None, :] - take(v_start, blk), take(e_hi, blk)).astype(jnp.int32)
    lo = jnp.clip(take(starts, exp) - blk * MOE_RB, 0, MOE_RB)
    hi = jnp.where(valid, jnp.clip(take(ends, exp) - blk * MOE_RB, 0, MOE_RB), 0)
    first = jnp.logical_and(valid, v[None, :] == take(v_start, blk))
    flat = lambda a: a.astype(jnp.int32).reshape(-1)
    r = jnp.arange(MOE_RB, dtype=jnp.int32)
    tok_v = jnp.take_along_axis(tok.reshape(b, nblk, MOE_RB), blk[:, :, None], axis=1)
    mine = jnp.logical_and(r >= lo[:, :, None], r < hi[:, :, None])
    dst = jnp.where(mine, tok_v, ta + r % MOE_U)
    rows = nchunk
    return ((tok * rows).astype(jnp.int32), (dst * rows).astype(jnp.int32), g_sorted,
            tuple(flat(a) for a in (blk, exp, lo, hi, valid, first)))


def _moe_finish_kernel(x_ref, r_ref, mod_ref, sg_ref, su_ref, sd_ref, ln_ref, o_ref, *, alpha):
    tm, d = x_ref.shape[1], x_ref.shape[2]
    nchunk = d // 128
    x = x_ref[0]
    h = (_ln(x) * (1.0 + mod_ref[0, 0, 4:5, :]) + mod_ref[0, 0, 3:4, :]).astype(BF16)
    shared = _dot((_silu(_dot(h, sg_ref[...])) * _dot(h, su_ref[...])).astype(BF16), sd_ref[...])
    routed = jnp.concatenate([r_ref[0, pl.ds(j, tm, stride=nchunk), :] for j in range(nchunk)], axis=1)
    o_ref[0] = _ln(alpha * x + mod_ref[0, 0, 5:6, :] * (routed + shared)) * ln_ref[0:1] + ln_ref[1:2]


def _moe_finish(xa, routed, modv, sg, su, sd, ln, *, t_lat, tm, alpha, n_rows):
    b, ta, d = xa.shape
    nt_lat = t_lat // tm
    full = lambda a: pl.BlockSpec(a.shape, lambda bi, i: (0,) * a.ndim)
    return pl.pallas_call(
        functools.partial(_moe_finish_kernel, alpha=alpha),
        grid=(b, n_rows // tm),
        in_specs=[pl.BlockSpec((1, tm, d), lambda bi, i: (bi, i, 0)),
                  pl.BlockSpec((1, tm * (d // 128), 128), lambda bi, i: (bi, i, 0)),
                  pl.BlockSpec((1, 1, 6, d), lambda bi, i: (bi, i // nt_lat, 0, 0)),
                  full(sg), full(su), full(sd), full(ln)],
        out_specs=pl.BlockSpec((1, tm, d), lambda bi, i: (bi, i, 0)),
        out_shape=jax.ShapeDtypeStruct((b, n_rows, d), F32),
        compiler_params=_cp(("parallel", "parallel")),
        name="moe_finish",
    )(xa, routed, modv, sg, su, sd, ln)


def _rope_blocks(w_rope):
    k = w_rope.shape[0]
    ev, od = w_rope[:, 0::2], w_rope[:, 1::2]
    z64, z32 = jnp.zeros((k, 64), w_rope.dtype), jnp.zeros((k, 32), w_rope.dtype)
    return (jnp.concatenate([z64, ev, od, z32], 1), jnp.concatenate([z64, -od, ev, z32], 1))


def _layer_weights(l, w_in, mla_w_uq, mla_w_ukv):
    w = w_in[l]
    d = w.shape[0]
    wm, wr, wa = w[:, :N_MLSTM_IN], w[:, N_MLSTM_IN:N_MLSTM_IN + N_RWKV_IN], w[:, N_MLSTM_IN + N_RWKV_IN:]
    kr, krr = _rope_blocks(wa[:, Q_RANK + KV_RANK:])
    w_ext = jnp.concatenate([
        wm[:, :2 * MLSTM_W], wr, wm[:, 2 * MLSTM_W:4 * MLSTM_W],
        wm[:, 4 * MLSTM_W:], jnp.zeros((d, 128 - 4 * MLSTM_H), w.dtype),
        wa[:, :Q_RANK + KV_RANK], kr, krr], axis=1).astype(BF16)
    uq = mla_w_uq[l].reshape(Q_RANK, MLA_H, MLA_NOPE + MLA_ROPE)
    nope, ev, od = uq[:, :, :MLA_NOPE], uq[:, :, MLA_NOPE::2], uq[:, :, MLA_NOPE + 1::2]
    z64, z32 = jnp.zeros((Q_RANK, MLA_H, 64), F32), jnp.zeros((Q_RANK, MLA_H, 32), F32)
    wq = jnp.concatenate([nope, ev, od, z32], axis=2).reshape(Q_RANK, MLA_H * 128)
    wqr = jnp.concatenate([z64, -od, ev, z32], axis=2).reshape(Q_RANK, MLA_H * 128)
    ukv = mla_w_ukv[l].reshape(KV_RANK, MLA_H, MLA_NOPE + MLA_V)
    wk = jnp.concatenate([ukv[:, :, :MLA_NOPE], jnp.zeros((KV_RANK, MLA_H, 64), F32)], axis=2).reshape(KV_RANK, MLA_H * 128)
    wv = ukv[:, :, MLA_NOPE:].reshape(KV_RANK, MLA_H * MLA_V)
    return w_ext, wq.astype(BF16), wqr.astype(BF16), wk.astype(BF16), wv.T.astype(BF16)


def _rope_tables(t_lat, n_ctx):
    rows = t_lat // GRID_W
    row = jnp.repeat(jnp.arange(rows), GRID_W).astype(F32)
    col = jnp.tile(jnp.arange(GRID_W), rows).astype(F32)
    n_freq = MLA_ROPE // 4
    freq = ROPE_THETA ** (-jnp.arange(n_freq, dtype=F32) / n_freq)
    ang = jnp.concatenate([row[:, None] * freq, col[:, None] * freq], -1)
    cos, sin = jnp.cos(ang), jnp.sin(ang)
    one, zero = jnp.ones((t_lat, 64), F32), jnp.zeros((t_lat, 32), F32)
    cos_l = jnp.concatenate([one, cos, cos, zero], 1)
    sin_l = jnp.concatenate([0 * one, sin, sin, zero], 1)
    cos_c = jnp.concatenate([jnp.ones((n_ctx, 96), F32), jnp.zeros((n_ctx, 32), F32)], 1)
    return jnp.concatenate([cos_l, cos_c], 0), jnp.concatenate([sin_l, jnp.zeros((n_ctx, 128), F32)], 0)


def kernel(x, c, ctx, c_ctx, w_mod, b_mod, w_in, mlstm_conv, mlstm_gate_bias, mlstm_norm_w, rwkv_mu, rwkv_w0, rwkv_w_up, rwkv_a0, rwkv_a_up, rwkv_g_up, rwkv_k_k, rwkv_k_a, rwkv_r_k, rwkv_ln_w, rwkv_ln_b, mla_q_norm, mla_kv_norm, mla_w_uq, mla_w_ukv, w_out, ln1_w, ln1_b, router_w, router_bias, exp_w_gate, exp_w_up, exp_w_down, sh_w_gate, sh_w_up, sh_w_down, ln2_w, ln2_b):
    b, t_lat, d = x.shape
    n_ctx = ctx.shape[1]
    ta = t_lat + n_ctx
    depth = w_in.shape[0]
    alpha = (2 * depth) ** 0.25
    tm = 256
    cl = 256
    nv_moe = TOP_K * ta // MOE_RB + N_EXPERTS
    assert b + 1 <= 8 and n_ctx % tm == 0 and t_lat % tm == 0 and (TOP_K * ta) % MOE_RB == 0

    cc = jnp.concatenate([c, c_ctx[None], jnp.zeros((8 - b - 1, d), F32)], 0)
    mods = _modulation(cc, w_mod, b_mod)
    cos, sin = _rope_tables(t_lat, n_ctx)
    hsum = jnp.kron(jnp.eye(RWKV_H, dtype=F32), jnp.ones((RWKV_N, RWKV_N), F32))
    hmean = hsum / RWKV_N
    lane = jnp.arange(128, dtype=jnp.int32) // (128 // (b * RWKV_H))
    pats = jnp.broadcast_to((jnp.arange(8, dtype=jnp.int32) * (b * RWKV_H))[:, None, None] + lane, (8, RWKV_N, 128))

    exp_w = [w.astype(BF16).reshape((depth * N_EXPERTS,) + w.shape[2:]) for w in (exp_w_gate, exp_w_up, exp_w_down)]
    xa = jnp.concatenate([x, ctx], axis=1)
    for l in range(depth):
        m_lat = mods[l, :b].reshape(b, 1, 6, d)
        m_ctx = jnp.broadcast_to(mods[l, b].reshape(1, 1, 6, d), (b, 1, 6, d))
        modv = jnp.concatenate([m_lat, m_ctx], axis=1)
        w_ext, wq, wqr, wk, wv = _layer_weights(l, w_in, mla_w_uq, mla_w_ukv)
        mu = rwkv_mu[l].reshape(1, N_RWKV_IN)
        zm, zg, zr, za = _in_proj(xa, modv, w_ext, mlstm_conv[l], mu, t_lat=t_lat, tm=tm)

        gates = zg[:, :, :4 * MLSTM_H].reshape(b, ta, 2, 8)
        g_col = gates.transpose(0, 2, 1, 3)
        g_row = gates.transpose(0, 2, 3, 1)
        gb = mlstm_gate_bias[l].reshape(2, 8)
        h_m = _mlstm(zm, g_col, g_row, gb.reshape(2, 1, 8), gb.reshape(2, 8, 1), t_lat=t_lat, cl=cl)

        zeros = jnp.zeros((DECAY_LORA, RWKV_W), F32)
        wup = jnp.concatenate([jnp.concatenate([rwkv_w_up[l, 0], zeros], 1),
                               jnp.concatenate([zeros, rwkv_w_up[l, 1]], 1)], 0).astype(BF16)
        aup = jnp.concatenate([jnp.concatenate([rwkv_a_up[l, 0], zeros], 1),
                               jnp.concatenate([zeros, rwkv_a_up[l, 1]], 1)], 0).astype(BF16)
        vec = jnp.concatenate([rwkv_w0[l], rwkv_a0[l], rwkv_k_k[l][None], rwkv_k_a[l][None], rwkv_r_k[l][None],
                               jnp.zeros((1, RWKV_W), F32)], 0)
        pd, v_tok, g_tok, bonus = _rwkv_prep(zr, wup, aup, rwkv_g_up[l].astype(BF16), vec, hsum, tm=128)
        rep = 128 // (b * RWKV_H)
        nvh = RWKV_N // rep
        vs = v_tok.reshape(b, ta, RWKV_H, nvh, rep).transpose(1, 3, 0, 2, 4).reshape(ta, nvh, 128)
        o_f, o_b = _rwkv_scan(pd, pats, vs, t_lat=t_lat, ts=32)
        o_s = jnp.stack([o_f, o_b], 0).reshape(2, ta, nvh, b, RWKV_H, rep)
        o_s = o_s.transpose(0, 3, 1, 4, 2, 5).reshape(2, b, ta, RWKV_W)

        q, k, vt = _mla_proj(za, cos, sin, mla_q_norm[l][None], mla_kv_norm[l][None], wq, wqr, wk, wv, tm=tm)
        att_l = _attention(q, k, vt, q_off=0, n_q=t_lat, k_off=0, n_keys=ta, tq=tm)
        att_c = _attention(q, k, vt, q_off=t_lat, n_q=n_ctx, k_off=t_lat, n_keys=n_ctx, tq=tm)

        vec_o = jnp.concatenate([mlstm_norm_w[l][None], rwkv_ln_w[l][None], rwkv_ln_b[l][None],
                                 jnp.zeros((5, RWKV_W), F32)], 0)
        ln1 = jnp.stack([ln1_w[l], ln1_b[l]], 0)
        xa = _out_proj(xa, h_m, zm, o_s, g_tok, bonus, att_l, att_c, modv, vec_o, hmean, w_out[l].astype(BF16), ln1,
                       t_lat=t_lat, tm=tm, alpha=alpha)

        ids, gts, h_rows = _router(xa, modv, router_w[l].T, router_bias[l].reshape(N_EXPERTS, 1), t_lat=t_lat, tm=tm)
        tok, dst, g_sorted, tables = _moe_tables(ids, gts, ta=ta, nv=nv_moe, nchunk=d // 128)
        routed = _moe_routed(h_rows, tok, dst, g_sorted, tables, *exp_w, nv=nv_moe, layer=l)
        ln2 = jnp.stack([ln2_w[l], ln2_b[l]], 0)
        xa = _moe_finish(xa, routed, modv, sh_w_gate[l].astype(BF16), sh_w_up[l].astype(BF16),
                         sh_w_down[l].astype(BF16), ln2, t_lat=t_lat, tm=tm, alpha=alpha,
                         n_rows=t_lat if l == depth - 1 else ta)
    return xa
```

```python
import functools

import jax
import jax.numpy as jnp
import numpy as np
from jax import lax
from jax.experimental import pallas as pl
from jax.experimental.pallas import tpu as pltpu

F32 = jnp.float32
BF16 = jnp.bfloat16
HIGHEST = lax.Precision.HIGHEST

GRID_W = 64
MLSTM_H, MLSTM_DH = 4, 64
MLSTM_W = MLSTM_H * MLSTM_DH
RWKV_H, RWKV_N = 4, 64
RWKV_W = RWKV_H * RWKV_N
DECAY_LORA, AAA_LORA, GATE_LORA = 64, 64, 128
RWKV_GN_EPS = 64e-5
MLA_H, MLA_NOPE, MLA_ROPE, MLA_V = 8, 64, 32, 64
Q_RANK, KV_RANK = 256, 128
ROPE_THETA = 10000.0
N_MLSTM_IN = 4 * MLSTM_W + 4 * MLSTM_H
N_RWKV_IN = 3 * RWKV_W + 2 * DECAY_LORA + 2 * AAA_LORA + GATE_LORA
N_EXPERTS, TOP_K, N_GROUPS, TOPK_GROUPS = 64, 8, 8, 4
ROUTED_SCALE = 2.5
LN_EPS = 1e-6
NEG = -1e30

C_QK, C_R, C_V, C_O, C_G, C_A = 0, 512, 1664, 1920, 2176, 2304
N_SHIFT = 1664
N_A = 640
N_EXT = C_A + N_A
HALO = 8
VMEM_LIMIT = 56 * 1024 * 1024


def _cp(sem):
    return pltpu.CompilerParams(dimension_semantics=sem, vmem_limit_bytes=VMEM_LIMIT)


def _ln(x):
    mu = jnp.mean(x, axis=-1, keepdims=True)
    xc = x - mu
    return xc * lax.rsqrt(jnp.mean(xc * xc, axis=-1, keepdims=True) + LN_EPS)


def _sigmoid(x):
    return 1.0 / (1.0 + jnp.exp(-x))


def _silu(x):
    return x * _sigmoid(x)


def _log_sigmoid(x):
    return jnp.minimum(x, 0.0) - jnp.log(1.0 + jnp.exp(-jnp.abs(x)))


def _softplus(x):
    return jnp.maximum(x, 0.0) + jnp.log(1.0 + jnp.exp(-jnp.abs(x)))


def _nt_dot(a, b, **kw):
    return lax.dot_general(a, b, (((1,), (1,)), ((), ())), preferred_element_type=F32, **kw)


def _tn_dot(a, b, **kw):
    return lax.dot_general(a, b, (((0,), (0,)), ((), ())), preferred_element_type=F32, **kw)


def _dot(a, b, **kw):
    return jnp.dot(a, b, preferred_element_type=F32, **kw)


def _mod_kernel(c_ref, w_ref, b_ref, o_ref):
    o_ref[0] = _dot(_silu(c_ref[...]), w_ref[0], precision=HIGHEST) + b_ref[0]


def _modulation(cc, w_mod, b_mod):
    n_layers, d, n = w_mod.shape
    tn = 1536
    return pl.pallas_call(
        _mod_kernel,
        grid=(n_layers, n // tn),
        in_specs=[pl.BlockSpec((8, d), lambda l, j: (0, 0)),
                  pl.BlockSpec((1, d, tn), lambda l, j: (l, 0, j)),
                  pl.BlockSpec((1, 1, tn), lambda l, j: (l, 0, j))],
        out_specs=pl.BlockSpec((1, 8, tn), lambda l, j: (l, 0, j)),
        out_shape=jax.ShapeDtypeStruct((n_layers, 8, n), F32),
        compiler_params=_cp(("parallel", "parallel")),
        name="modulation",
    )(cc, w_mod, b_mod.reshape(n_layers, 1, n))


def _in_kernel(xp_ref, x_ref, xn_ref, mod_ref, w_ref, conv_ref, mu_ref, cos_ref, sin_ref, nq_ref, nkv_ref,
               wq_ref, wqr_ref, wk_ref, wv_ref, zm_ref, zg_ref, zr_ref, q_ref, k_ref, vt_ref, *, tm, nt_lat, nt):
    i = pl.program_id(1)
    has_prev = jnp.logical_and(i != 0, i != nt_lat)
    has_next = jnp.logical_and(i != nt_lat - 1, i != nt - 1)
    xt = jnp.concatenate([xp_ref[0], x_ref[0], xn_ref[0]], axis=0)
    sh = mod_ref[0, 0, 0:1, :]
    sc = mod_ref[0, 0, 1:2, :]
    h = (_ln(xt) * (1.0 + sc) + sh).astype(BF16)
    z = _dot(h, w_ref[...])
    rows = lax.broadcasted_iota(jnp.int32, (tm + 2 * HALO, 1), 0)
    lo = jnp.where(has_prev, 0, HALO)
    hi = jnp.where(has_next, tm + 2 * HALO, tm + HALO)
    keep = jnp.logical_and(rows >= lo, rows < hi)
    zs = jnp.where(keep, z[:, :N_SHIFT], 0.0)
    zc = zs[HALO:HALO + tm]
    zprev = pltpu.roll(zs, 1, axis=0)[HALO:HALO + tm]
    znext = pltpu.roll(zs, tm + 2 * HALO - 1, axis=0)[HALO:HALO + tm]
    cw = conv_ref[...]
    qk = (cw[0:1] * zprev[:, :C_R] + cw[1:2] * zc[:, :C_R] + cw[2:3] * znext[:, :C_R])
    qk = _silu(qk)
    lane = lax.broadcasted_iota(jnp.int32, (1, C_R), 1)
    qk = qk * jnp.where(lane < MLSTM_W, MLSTM_DH ** -0.5, 1.0)
    zm_ref[0, :, 0:512] = qk
    zm_ref[0, :, 512:1024] = z[HALO:HALO + tm, C_V:C_G]
    zg_ref[0] = z[HALO:HALO + tm, C_G:C_A]
    zr = zc[:, C_R:]
    zr_ref[0] = zr + mu_ref[...] * (0.5 * (zprev[:, C_R:] + znext[:, C_R:]) - zr)
    _mla_project(z[HALO:HALO + tm, C_A:], cos_ref, sin_ref, nq_ref, nkv_ref, wq_ref, wqr_ref, wk_ref, wv_ref,
                 q_ref, k_ref, vt_ref)


def _in_proj(xa, modv, w_ext, conv, mu, mla, *, t_lat, tm):
    b, ta, d = xa.shape
    full = lambda a: pl.BlockSpec(a.shape, lambda bi, i: (0,) * a.ndim)
    rope = pl.BlockSpec((tm, 128), lambda bi, i: (i, 0))
    nt, nt_lat = ta // tm, t_lat // tm
    nh = tm // HALO
    last = ta // HALO - 1
    kern = functools.partial(_in_kernel, tm=tm, nt_lat=nt_lat, nt=nt)
    return pl.pallas_call(
        kern,
        grid=(b, nt),
        in_specs=[pl.BlockSpec((1, HALO, d), lambda bi, i: (bi, jnp.maximum(i * nh - 1, 0), 0)),
                  pl.BlockSpec((1, tm, d), lambda bi, i: (bi, i, 0)),
                  pl.BlockSpec((1, HALO, d), lambda bi, i: (bi, jnp.minimum((i + 1) * nh, last), 0)),
                  pl.BlockSpec((1, 1, 6, d), lambda bi, i: (bi, i // nt_lat, 0, 0)),
                  pl.BlockSpec((d, N_EXT), lambda bi, i: (0, 0)),
                  pl.BlockSpec((3, C_R), lambda bi, i: (0, 0)),
                  pl.BlockSpec((1, N_RWKV_IN), lambda bi, i: (0, 0)),
                  rope, rope] + [full(a) for a in mla[2:]],
        out_specs=[pl.BlockSpec((1, tm, 1024), lambda bi, i: (bi, i, 0)),
                   pl.BlockSpec((1, tm, 128), lambda bi, i: (bi, i, 0)),
                   pl.BlockSpec((1, tm, N_RWKV_IN), lambda bi, i: (bi, i, 0)),
                   pl.BlockSpec((1, tm, 1024), lambda bi, i: (bi, i, 0)),
                   pl.BlockSpec((1, tm, 1024), lambda bi, i: (bi, i, 0)),
                   pl.BlockSpec((1, 512, tm), lambda bi, i: (bi, 0, i))],
        out_shape=[jax.ShapeDtypeStruct((b, ta, 1024), F32),
                   jax.ShapeDtypeStruct((b, ta, 128), F32),
                   jax.ShapeDtypeStruct((b, ta, N_RWKV_IN), F32),
                   jax.ShapeDtypeStruct((b, ta, 1024), BF16),
                   jax.ShapeDtypeStruct((b, ta, 1024), BF16),
                   jax.ShapeDtypeStruct((b, 512, ta), BF16)],
        compiler_params=_cp(("parallel", "parallel")),
        name="in_proj",
    )(xa, xa, xa, modv, w_ext, conv, mu, *mla)


def _mlstm_kernel(zm_ref, gc_ref, gr_ref, bc_ref, br_ref, o_ref, ct_ref, n_ref, m_ref, *, cl):
    d = pl.program_id(1)

    @pl.when(pl.program_id(2) == 0)
    def _():
        ct_ref[...] = jnp.zeros_like(ct_ref)
        n_ref[...] = jnp.zeros_like(n_ref)
        m_ref[...] = jnp.zeros_like(m_ref)

    sgn = 1 - 2 * d
    gc = gc_ref[0, 0] + bc_ref[0]
    gr = gr_ref[0, 0] + br_ref[0]
    lf_c = _log_sigmoid(gc)
    lf_r = _log_sigmoid(gr)
    ii = lax.broadcasted_iota(jnp.int32, (cl, cl), 0)
    jj = lax.broadcasted_iota(jnp.int32, (cl, cl), 1)
    mask = (jj - ii) * sgn <= 0
    mask_t = (ii - jj) * sgn <= 0
    cum_c = _dot(mask.astype(F32), lf_c, precision=HIGHEST)
    cum_r = _dot(lf_r, mask_t.astype(F32), precision=HIGHEST)
    outs = []
    for h in range(MLSTM_H):
        sl = slice(h * MLSTM_DH, (h + 1) * MLSTM_DH)
        q = zm_ref[0, :, sl]
        k = zm_ref[0, :, MLSTM_W + h * MLSTM_DH:MLSTM_W + (h + 1) * MLSTM_DH]
        v = zm_ref[0, :, 2 * MLSTM_W + h * MLSTM_DH:2 * MLSTM_W + (h + 1) * MLSTM_DH]
        qb, kb = q.astype(BF16), k.astype(BF16)
        li_c, li_r = gc[:, h:h + 1], gr[h:h + 1, :]
        cu_c, cu_r = cum_c[:, 4 + h:5 + h], cum_r[4 + h:5 + h, :]
        b_end = jnp.sum(lf_r[4 + h:5 + h, :], axis=1, keepdims=True)
        m_prev = m_ref[h:h + 1, 0:1]
        ct = ct_ref[h]
        nv = n_ref[h:h + 1, 0:MLSTM_DH]
        d_log = jnp.where(mask, cu_c - cu_r + li_r, NEG)
        g_log = cu_c + m_prev
        m_row = jnp.maximum(g_log, jnp.max(d_log, axis=1, keepdims=True))
        w_intra = jnp.exp(d_log - m_row) * _nt_dot(qb, kb)
        e_inter = jnp.exp(g_log - m_row)
        num = _dot(w_intra.astype(BF16), v.astype(BF16)) + e_inter * _dot(qb, ct.astype(BF16))
        den = jnp.sum(w_intra, axis=1, keepdims=True) + e_inter * jnp.sum(q * nv, axis=1, keepdims=True)
        outs.append(num / jnp.maximum(jnp.abs(den), jnp.exp(-m_row)))
        w_end = b_end - cu_c + li_c
        m_loc = jnp.max(w_end, axis=0, keepdims=True)
        e_end = jnp.exp(w_end - m_loc)
        m_new = jnp.maximum(b_end + m_prev, m_loc)
        a_old = jnp.exp(b_end + m_prev - m_new)
        a_loc = jnp.exp(m_loc - m_new)
        ct_ref[h] = a_old * ct + a_loc * _tn_dot(kb, (v * e_end).astype(BF16))
        n_ref[h:h + 1, 0:MLSTM_DH] = a_old * nv + a_loc * jnp.sum(k * e_end, axis=0, keepdims=True)
        m_ref[h:h + 1, :] = jnp.broadcast_to(m_new, (1, 128))
    o_ref[0, 0] = jnp.concatenate(outs, axis=1)


def _mlstm(zm, g_col, g_row, b_col, b_row, *, t_lat, cl):
    b, ta, _ = zm.shape
    nc, nc_lat = ta // cl, t_lat // cl
    nc_ctx = nc - nc_lat

    def chunk(di, c):
        fwd_idx = jnp.where(c < nc_ctx, nc_lat + c, c - nc_ctx)
        return jnp.where(di == 0, fwd_idx, nc - 1 - c)

    return pl.pallas_call(
        functools.partial(_mlstm_kernel, cl=cl),
        grid=(b, 2, nc),
        in_specs=[pl.BlockSpec((1, cl, 1024), lambda bi, di, c: (bi, chunk(di, c), 0)),
                  pl.BlockSpec((1, 1, cl, 8), lambda bi, di, c: (bi, di, chunk(di, c), 0)),
                  pl.BlockSpec((1, 1, 8, cl), lambda bi, di, c: (bi, di, 0, chunk(di, c))),
                  pl.BlockSpec((1, 1, 8), lambda bi, di, c: (di, 0, 0)),
                  pl.BlockSpec((1, 8, 1), lambda bi, di, c: (di, 0, 0))],
        out_specs=pl.BlockSpec((1, 1, cl, MLSTM_W), lambda bi, di, c: (bi, di, chunk(di, c), 0)),
        out_shape=jax.ShapeDtypeStruct((b, 2, ta, MLSTM_W), F32),
        scratch_shapes=[pltpu.VMEM((MLSTM_H, MLSTM_DH, MLSTM_DH), F32),
                        pltpu.VMEM((8, 128), F32),
                        pltpu.VMEM((8, 128), F32)],
        compiler_params=_cp(("parallel", "parallel", "arbitrary")),
        name="mlstm",
    )(zm, g_col, g_row, b_col, b_row)


def _rwkv_prep_kernel(zr_ref, wup_ref, aup_ref, gup_ref, vec_ref, hsum_ref, pd_ref, v_ref, g_ref, bonus_ref):
    nb, tm = zr_ref.shape[0], zr_ref.shape[1]
    nbh = nb * RWKV_H
    vec = vec_ref[...]
    hsum = hsum_ref[...]
    zero = jnp.zeros((tm, RWKV_N), F32)
    for bi in range(nb):
        z = zr_ref[bi]
        zr_, zk, zv = z[:, 0:256], z[:, 256:512], z[:, 512:768]
        zw, za, zg = z[:, 768:896], z[:, 896:1024], z[:, 1024:1152]
        w_raw = _dot(jnp.tanh(zw).astype(BF16), wup_ref[...])
        a_raw = _dot(za.astype(BF16), aup_ref[...])
        g_ref[bi] = _dot(_sigmoid(zg).astype(BF16), gup_ref[...])
        kk = zk * vec[4:5]
        kk_ss = _dot(kk * kk, hsum, precision=HIGHEST)
        kk = kk * lax.rsqrt(jnp.maximum(kk_ss, 1e-12))
        k_sum = jnp.zeros_like(zk)
        for di in range(2):
            sl = slice(di * RWKV_W, (di + 1) * RWKV_W)
            decay = jnp.exp(-jnp.exp(-_softplus(-(vec[di:di + 1] + w_raw[:, sl])) - 0.5))
            a = _sigmoid(vec[2 + di:3 + di] + a_raw[:, sl])
            k_dir = zk * (1.0 + (a - 1.0) * vec[5:6])
            k_sum = k_sum + k_dir
            ka = kk * a
            for h in range(RWKV_H):
                hs = slice(h * RWKV_N, (h + 1) * RWKV_N)
                rows = pl.ds(bi * RWKV_H + h, tm, stride=nbh)
                pd_ref[di, 0, rows, :] = jnp.concatenate([decay[:, hs], ka[:, hs]], axis=1)
                pd_ref[di, 1, rows, :] = jnp.concatenate([k_dir[:, hs], kk[:, hs]], axis=1)
                pd_ref[di, 2, rows, :] = jnp.concatenate([zr_[:, hs], zero], axis=1)
        v_ref[bi] = zv
        bonus_ref[bi] = _dot(zr_ * vec[6:7] * k_sum, hsum, precision=HIGHEST) * zv


def _rwkv_prep(zr, wup, aup, gup, vec, hsum, *, tm):
    b, ta, _ = zr.shape
    nbh = b * RWKV_H
    tok = pl.BlockSpec((b, tm, RWKV_W), lambda i: (0, i, 0))
    full = lambda a: pl.BlockSpec(a.shape, lambda i: (0,) * a.ndim)
    return pl.pallas_call(
        _rwkv_prep_kernel,
        grid=(ta // tm,),
        in_specs=[pl.BlockSpec((b, tm, N_RWKV_IN), lambda i: (0, i, 0)),
                  full(wup), full(aup), full(gup), full(vec), full(hsum)],
        out_specs=[pl.BlockSpec((2, 3, tm * nbh, 128), lambda i: (0, 0, i, 0)), tok, tok, tok],
        out_shape=[jax.ShapeDtypeStruct((2, 3, ta * nbh, 128), F32)] + [jax.ShapeDtypeStruct((b, ta, RWKV_W), F32)] * 3,
        compiler_params=_cp(("parallel",)),
        name="rwkv_prep",
    )(zr, wup, aup, gup, vec, hsum)


def _rwkv_scan_kernel(pf_ref, pb_ref, pat_ref, vf_ref, vb_ref, of_ref, ob_ref, st_ref, x_ref, tt_ref, *, ts, nvh, nbh):
    ng = ts // 8
    rows = 8 * nbh

    @pl.when(pl.program_id(0) == 0)
    def _():
        st_ref[...] = jnp.zeros_like(st_ref)

    def transpose(di, p_ref, grp):
        sl = pl.ds(pl.multiple_of(grp * rows, rows), rows)
        for j in range(3):
            tt_ref[di, j] = p_ref[0, j, sl, :].T

    def update(di, u, v_ref, o_ref, t):
        pat = pat_ref[u]
        for j, (tj, half) in enumerate(((0, 0), (0, 1), (1, 0), (1, 1), (2, 0))):
            x_ref[di,j] = jnp.take_along_axis(tt_ref[di, tj, half * RWKV_N:(half + 1) * RWKV_N, :], pat, axis=1)
        vt = v_ref[t]
        nkb = RWKV_N // 8
        kblk = lambda kb: pl.ds(kb * 8, 8)
        acc = [None] * nvh
        for kb in range(nkb):
            kk = x_ref[di,3, kblk(kb), :]
            for vh in range(nvh):
                term = st_ref[di, vh, kblk(kb), :] * kk
                acc[vh] = term if kb == 0 else acc[vh] + term
        sa = [jnp.sum(a, axis=0, keepdims=True) for a in acc]
        out = [None] * nvh
        for kb in range(nkb):
            w, ka = x_ref[di,0, kblk(kb), :], x_ref[di,1, kblk(kb), :]
            k, r = x_ref[di,2, kblk(kb), :], x_ref[di,4, kblk(kb), :]
            for vh in range(nvh):
                sv = st_ref[di, vh, kblk(kb), :] * w - sa[vh] * ka + vt[vh:vh + 1, :] * k
                st_ref[di, vh, kblk(kb), :] = sv
                out[vh] = sv * r if kb == 0 else out[vh] + sv * r
        o_ref[t] = jnp.concatenate([jnp.sum(o, axis=0, keepdims=True) for o in out], axis=0)

    def group(g, carry):
        transpose(0, pf_ref, g)
        transpose(1, pb_ref, ng - 1 - g)
        for u in range(8):
            update(0, u, vf_ref, of_ref, g * 8 + u)
            update(1, 7 - u, vb_ref, ob_ref, ts - 1 - (g * 8 + u))
        return carry

    lax.fori_loop(0, ng, group, 0)


def _rwkv_scan(pd, pats, vs, *, t_lat, ts):
    ta, nvh, nl = vs.shape
    nbh = pd.shape[2] // ta
    nt, nt_lat = ta // ts, t_lat // ts
    fwd = lambda i: (i + nt_lat) % nt
    bwd = lambda i: nt - 1 - i
    return pl.pallas_call(
        functools.partial(_rwkv_scan_kernel, ts=ts, nvh=nvh, nbh=nbh),
        grid=(nt,),
        in_specs=[pl.BlockSpec((1, 3, ts * nbh, 128), lambda i: (0, 0, fwd(i), 0)),
                  pl.BlockSpec((1, 3, ts * nbh, 128), lambda i: (1, 0, bwd(i), 0)),
                  pl.BlockSpec(pats.shape, lambda i: (0, 0, 0)),
                  pl.BlockSpec((ts, nvh, nl), lambda i: (fwd(i), 0, 0)),
                  pl.BlockSpec((ts, nvh, nl), lambda i: (bwd(i), 0, 0))],
        out_specs=[pl.BlockSpec((ts, nvh, nl), lambda i: (fwd(i), 0, 0)),
                   pl.BlockSpec((ts, nvh, nl), lambda i: (bwd(i), 0, 0))],
        out_shape=[jax.ShapeDtypeStruct((ta, nvh, nl), F32)] * 2,
        scratch_shapes=[pltpu.VMEM((2, nvh, RWKV_N, nl), F32), pltpu.VMEM((2, 5, RWKV_N, nl), F32),
                        pltpu.VMEM((2, 3, 128, 8 * nbh), F32)],
        compiler_params=_cp(("arbitrary",)),
        name="rwkv_scan",
    )(pd, pd, pats, vs, vs)


def _mla_project(za, cos_ref, sin_ref, nq_ref, nkv_ref, wq_ref, wqr_ref, wk_ref, wv_ref, q_ref, k_ref, v_ref):
    cq, ckv = za[:, 0:Q_RANK], za[:, Q_RANK:Q_RANK + KV_RANK]
    kr, krr = za[:, 384:512], za[:, 512:640]
    cos, sin = cos_ref[...], sin_ref[...]
    cqn = (cq * lax.rsqrt(jnp.mean(cq * cq, axis=-1, keepdims=True) + 1e-6) * nq_ref[...]).astype(BF16)
    ckvn = (ckv * lax.rsqrt(jnp.mean(ckv * ckv, axis=-1, keepdims=True) + 1e-6) * nkv_ref[...]).astype(BF16)
    q = _dot(cqn, wq_ref[...])
    qr = _dot(cqn, wqr_ref[...])
    kn = _dot(ckvn, wk_ref[...])
    v_ref[0] = _nt_dot(wv_ref[...], ckvn).astype(BF16)
    k_rope = kr * cos + krr * sin
    scale = (MLA_NOPE + MLA_ROPE) ** -0.5 * np.log2(np.e)
    for h in range(MLA_H):
        sl = slice(h * 128, (h + 1) * 128)
        q_ref[0, :, sl] = ((q[:, sl] * cos + qr[:, sl] * sin) * scale).astype(BF16)
        k_ref[0, :, sl] = (kn[:, sl] + k_rope).astype(BF16)


def _attn_kernel(q_ref, k_ref, vt_ref, o_ref):
    outs = []
    for h in range(ATTN_HEADS):
        sl = slice(h * 128, (h + 1) * 128)
        pair = h // 2
        st = _nt_dot(k_ref[0, :, sl], q_ref[0, :, sl])
        m = jnp.max(st, axis=0, keepdims=True)
        p = jnp.exp2(st - m)
        l = jnp.sum(p, axis=0, keepdims=True)
        ot = _dot(vt_ref[0, pair * 128:(pair + 1) * 128, :], p.astype(BF16))
        outs.append(ot[(h % 2) * MLA_V:(h % 2 + 1) * MLA_V] / l)
    o_ref[0] = jnp.concatenate(outs, axis=0).T


ATTN_HEADS = 4


def _attention(q, k, vt, *, q_off, n_q, k_off, n_keys, tq):
    b = q.shape[0]
    qo, ko = q_off // tq, k_off // n_keys
    hq, hv = ATTN_HEADS * 128, ATTN_HEADS * MLA_V
    return pl.pallas_call(
        _attn_kernel,
        grid=(b, MLA_H // ATTN_HEADS, n_q // tq),
        in_specs=[pl.BlockSpec((1, tq, hq), lambda bi, hp, i: (bi, qo + i, hp)),
                  pl.BlockSpec((1, n_keys, hq), lambda bi, hp, i: (bi, ko, hp)),
                  pl.BlockSpec((1, hv, n_keys), lambda bi, hp, i: (bi, hp, ko))],
        out_specs=pl.BlockSpec((1, tq, hv), lambda bi, hp, i: (bi, i, hp)),
        out_shape=jax.ShapeDtypeStruct((b, n_q, 512), F32),
        compiler_params=_cp(("parallel", "parallel", "parallel")),
        name="mla_attention",
    )(q, k, vt)


def _out_kernel(x_ref, hf_ref, hb_ref, zm_ref, of_ref, ob_ref, g_ref, bonus_ref, attl_ref, attc_ref, mod_ref, vec_ref,
                hmean_ref, wo_ref, ln_ref, rw_ref, rb_ref, o_ref, id_ref, gate_ref, h_ref, *, alpha, nt_lat):
    hmean = hmean_ref[...]
    att = jnp.where(pl.program_id(1) < nt_lat, attl_ref[0], attc_ref[0])

    def head_norm(y, eps):
        mu = _dot(y, hmean, precision=HIGHEST)
        yc = y - mu
        return yc * lax.rsqrt(_dot(yc * yc, hmean, precision=HIGHEST) + eps)

    vec = vec_ref[...]
    zo = zm_ref[0]
    m_mix = _sigmoid(zo) * (head_norm(hf_ref[0, 0] + hb_ref[0, 0], 1e-6) * vec[0:1])
    r_o = head_norm(of_ref[0, 0] + ob_ref[0, 0], RWKV_GN_EPS) * vec[1:2] + vec[2:3] + bonus_ref[0]
    r_mix = r_o * g_ref[0]
    mix = jnp.concatenate([m_mix, r_mix, att], axis=1).astype(BF16)
    y = _dot(mix, wo_ref[...])
    gate = mod_ref[0, 0, 2:3, :]
    x_new = _ln(alpha * x_ref[0] + gate * y) * ln_ref[0:1] + ln_ref[1:2]
    o_ref[0] = x_new
    _route(x_new, mod_ref, rw_ref, rb_ref, id_ref, gate_ref, h_ref)


def _out_proj(xa, h_m, zm, o_r, g, bonus, att_l, att_c, modv, vec, hmean, wo, ln, rw_t, rbias, *, t_lat, tm, alpha):
    b, ta, d = xa.shape
    nt_lat = t_lat // tm
    tok = lambda w: pl.BlockSpec((1, tm, w), lambda bi, i: (bi, i, 0))
    full = lambda a: pl.BlockSpec(a.shape, lambda bi, i: (0,) * a.ndim)
    return pl.pallas_call(
        functools.partial(_out_kernel, alpha=alpha, nt_lat=nt_lat),
        grid=(b, ta // tm),
        in_specs=[tok(d),
                  pl.BlockSpec((1, 1, tm, MLSTM_W), lambda bi, i: (bi, 0, i, 0)),
                  pl.BlockSpec((1, 1, tm, MLSTM_W), lambda bi, i: (bi, 1, i, 0)),
                  pl.BlockSpec((1, tm, MLSTM_W), lambda bi, i: (bi, i, 3)),
                  pl.BlockSpec((1, 1, tm, RWKV_W), lambda bi, i: (0, bi, i, 0)),
                  pl.BlockSpec((1, 1, tm, RWKV_W), lambda bi, i: (1, bi, i, 0)),
                  tok(RWKV_W), tok(RWKV_W),
                  pl.BlockSpec((1, tm, 512), lambda bi, i: (bi, jnp.minimum(i, nt_lat - 1), 0)),
                  pl.BlockSpec((1, tm, 512), lambda bi, i: (bi, jnp.maximum(i - nt_lat, 0), 0)),
                  pl.BlockSpec((1, 1, 6, d), lambda bi, i: (bi, i // nt_lat, 0, 0)),
                  full(vec), full(hmean), full(wo), full(ln), full(rw_t), full(rbias)],
        out_specs=[tok(d),
                   pl.BlockSpec((1, TOP_K, tm), lambda bi, i: (bi, 0, i)),
                   pl.BlockSpec((1, TOP_K, tm), lambda bi, i: (bi, 0, i)),
                   pl.BlockSpec((1, tm * (d // 128), 128), lambda bi, i: (bi, i, 0))],
        out_shape=[jax.ShapeDtypeStruct((b, ta, d), F32),
                   jax.ShapeDtypeStruct((b, TOP_K, ta), jnp.int32),
                   jax.ShapeDtypeStruct((b, TOP_K, ta), F32),
                   jax.ShapeDtypeStruct((b, ta * (d // 128), 128), F32)],
        compiler_params=_cp(("parallel", "parallel")),
        name="out_proj",
    )(xa, h_m, h_m, zm, o_r, o_r, g, bonus, att_l, att_c, modv, vec, hmean, wo, ln, rw_t, rbias)


def _first_max(x, idx, axis):
    mx = jnp.max(x, axis=axis, keepdims=True)
    first = jnp.min(jnp.where(x == mx, idx, 1 << 20), axis=axis, keepdims=True)
    return mx, first, idx == first


def _route(x, mod_ref, w_ref, bias_ref, id_ref, gate_ref, h_ref):
    tm = x.shape[0]
    sh, sc = mod_ref[0, 0, 3:4, :], mod_ref[0, 0, 4:5, :]
    h = _ln(x) * (1.0 + sc) + sh
    for j in range(h.shape[1] // 128):
        h_ref[0, pl.ds(j, tm, stride=h.shape[1] // 128), :] = h[:, j * 128:(j + 1) * 128]
    scores = _sigmoid(_nt_dot(w_ref[...], h, precision=HIGHEST))
    biased = scores + bias_ref[...]
    gsz = N_EXPERTS // N_GROUPS
    b3 = biased.reshape(N_GROUPS, gsz, tm)
    s3 = scores.reshape(N_GROUPS, gsz, tm)
    e_idx = lax.broadcasted_iota(jnp.int32, (N_GROUPS, gsz, tm), 1)
    m1, _, hit = _first_max(b3, e_idx, 1)
    m2 = jnp.max(jnp.where(hit, -jnp.inf, b3), axis=1, keepdims=True)
    gscore = m1 + m2
    g_idx = lax.broadcasted_iota(jnp.int32, (N_GROUPS, 1, tm), 0)
    gsel = jnp.zeros((N_GROUPS, 1, tm), F32)
    for _ in range(TOPK_GROUPS):
        _, _, hit = _first_max(jnp.where(gsel > 0, -jnp.inf, gscore), g_idx, 0)
        gsel = jnp.where(hit, 1.0, gsel)
    cand = jnp.where(jnp.broadcast_to(gsel, b3.shape) > 0, b3, -jnp.inf)
    x_idx = g_idx * gsz + e_idx
    sel = jnp.zeros(b3.shape, F32)
    ids, picked = [], []
    for _ in range(TOP_K):
        _, first, hit = _first_max(jnp.where(sel > 0, -jnp.inf, cand), x_idx, (0, 1))
        sel = jnp.where(hit, 1.0, sel)
        ids.append(first[0])
        picked.append(jnp.sum(jnp.where(hit, s3, 0.0), axis=(0, 1), keepdims=True)[0])
    picked = jnp.concatenate(picked, axis=0)
    id_ref[0] = jnp.concatenate(ids, axis=0)
    gate_ref[0] = ROUTED_SCALE * picked / jnp.sum(picked, axis=0, keepdims=True)


MOE_RB = 256
MOE_PITCH = MOE_RB + 8
MOE_U = 8


def _moe_routed_kernel(vblk_ref, vexp_ref, vlo_ref, vhi_ref, vvalid_ref, vfirst_ref,
                       tok_ref, dst_ref, gate_ref, src_ref, wg_ref, wu_ref, wd_ref, acc_ref, xt_ref, yt_ref, xb_ref,
                       gc_ref, *, nv, nchunk):
    bi, v = pl.program_id(0), pl.program_id(1)
    pos = bi * nv + v
    lo, hi = vlo_ref[pos], vhi_ref[pos]

    @pl.when(v == 0)
    def _():
        acc_ref[...] = jnp.zeros_like(acc_ref)

    @pl.when(vfirst_ref[pos] == 1)
    def _():
        for r in range(MOE_RB):
            off = pl.multiple_of(tok_ref[0, 0, 0, r], nchunk)
            xt_ref[pl.ds(r, nchunk, stride=MOE_PITCH), :] = src_ref[0, pl.ds(off, nchunk), :]
        for j in range(nchunk):
            xb_ref[:, j * 128:(j + 1) * 128] = xt_ref[j * MOE_PITCH:j * MOE_PITCH + MOE_RB, :].astype(BF16)
        ii = lax.broadcasted_iota(jnp.int32, (MOE_RB, MOE_RB), 0)
        jj = lax.broadcasted_iota(jnp.int32, (MOE_RB, MOE_RB), 1)
        gc_ref[...] = jnp.sum(jnp.where(ii == jj, gate_ref[0, 0], 0.0), axis=1, keepdims=True)

    @pl.when(vvalid_ref[pos] == 1)
    def _():
        x = xb_ref[...]
        row = lax.broadcasted_iota(jnp.int32, (MOE_RB, 1), 0)
        gcol = jnp.where(jnp.logical_and(row >= lo, row < hi), gc_ref[...], 0.0)
        a = _silu(_dot(x, wg_ref[0])) * _dot(x, wu_ref[0]) * gcol
        y = _dot(a.astype(BF16), wd_ref[0])
        for j in range(nchunk):
            yt_ref[j * MOE_PITCH:j * MOE_PITCH + MOE_RB, :] = y[:, j * 128:(j + 1) * 128]
        for g in range(MOE_RB // MOE_U):
            pending = []
            for u in range(MOE_U):
                r = g * MOE_U + u
                off = pl.multiple_of(dst_ref[0, 0, 0, r], nchunk)
                pending.append((off, acc_ref[0, pl.ds(off, nchunk), :] + yt_ref[pl.ds(r, nchunk, stride=MOE_PITCH), :]))
            for off, val in pending:
                acc_ref[0, pl.ds(off, nchunk), :] = val


def _moe_routed(h_rows, tok, dst, gate, tables, wg, wu, wd, *, nv, layer):
    b, rows, _ = h_rows.shape
    d, de = wg.shape[1], wg.shape[2]
    nchunk = d // 128
    nblk = tok.shape[1] // MOE_RB
    blk = lambda bi, v, vblk, *_: (bi, vblk[bi * nv + v], 0, 0)
    exp = lambda bi, v, vblk, vexp, *_: (layer * N_EXPERTS + vexp[bi * nv + v], 0, 0)
    grid_spec = pltpu.PrefetchScalarGridSpec(
        num_scalar_prefetch=6,
        grid=(b, nv),
        in_specs=[pl.BlockSpec((1, 1, 1, MOE_RB), blk, memory_space=pltpu.SMEM),
                  pl.BlockSpec((1, 1, 1, MOE_RB), lambda bi, v, *_: (bi, v, 0, 0), memory_space=pltpu.SMEM),
                  pl.BlockSpec((1, 1, 1, MOE_RB), blk),
                  pl.BlockSpec((1, rows, 128), lambda bi, v, *_: (bi, 0, 0), pipeline_mode=pl.Buffered(1)),
                  pl.BlockSpec((1, d, de), exp),
                  pl.BlockSpec((1, d, de), exp),
                  pl.BlockSpec((1, de, d), exp)],
        out_specs=pl.BlockSpec((1, rows + MOE_U * nchunk, 128), lambda bi, v, *_: (bi, 0, 0),
                               pipeline_mode=pl.Buffered(1)),
        scratch_shapes=[pltpu.VMEM((nchunk * MOE_PITCH, 128), F32), pltpu.VMEM((nchunk * MOE_PITCH, 128), F32),
                        pltpu.VMEM((MOE_RB, d), BF16), pltpu.VMEM((MOE_RB, 1), F32)],
    )
    return pl.pallas_call(
        functools.partial(_moe_routed_kernel, nv=nv, nchunk=nchunk),
        grid_spec=grid_spec,
        out_shape=jax.ShapeDtypeStruct((b, rows + MOE_U * nchunk, 128), F32),
        compiler_params=_cp(("parallel", "arbitrary")),
        name="moe_routed",
    )(*tables, tok.reshape(b, nblk, 1, MOE_RB), dst.reshape(b, nv, 1, MOE_RB), gate.reshape(b, nblk, 1, MOE_RB),
      h_rows, wg, wu, wd)


def _moe_tables(ids, gates, *, ta, nv, nchunk):
    b = ids.shape[0]
    n = TOP_K * ta
    nblk = n // MOE_RB
    t_idx = jnp.broadcast_to(jnp.arange(ta, dtype=jnp.int32), (b, TOP_K, ta))
    keys, g_sorted = lax.sort(((ids * ta + t_idx).reshape(b, n), gates.reshape(b, n)), dimension=1, num_keys=1)
    tok, e_sorted = keys % ta, keys // ta
    ends = jnp.sum(e_sorted[:, None, :] <= jnp.arange(N_EXPERTS, dtype=jnp.int32)[None, :, None], axis=2)
    starts = ends - jnp.sum(e_sorted[:, None, :] == jnp.arange(N_EXPERTS, dtype=jnp.int32)[None, :, None], axis=2)
    e_lo, e_hi = e_sorted[:, ::MOE_RB], e_sorted[:, MOE_RB - 1::MOE_RB]
    n_vis = e_hi - e_lo + 1
    v_end = jnp.cumsum(n_vis, axis=1)
    v_start = v_end - n_vis
    v = jnp.arange(nv, dtype=jnp.int32)
    blk = jnp.minimum(jnp.sum(v_end[:, None, :] <= v[None, :, None], axis=2), nblk - 1).astype(jnp.int32)
    take = lambda a, i: jnp.take_along_axis(a, i, axis=1)
    valid = v[None, :] < v_end[:, -1:]
    exp = jnp.where(valid, take(e_lo, blk) + v[None, :] - take(v_start, blk), take(e_hi, blk)).astype(jnp.int32)
    lo = jnp.clip(take(starts, exp) - blk * MOE_RB, 0, MOE_RB)
    hi = jnp.where(valid, jnp.clip(take(ends, exp) - blk * MOE_RB, 0, MOE_RB), 0)
    first = jnp.logical_and(valid, v[None, :] == take(v_start, blk))
    flat = lambda a: a.astype(jnp.int32).reshape(-1)
    r = jnp.arange(MOE_RB, dtype=jnp.int32)
    tok_v = jnp.take_along_axis(tok.reshape(b, nblk, MOE_RB), blk[:, :, None], axis=1)
    mine = jnp.logical_and(r >= lo[:, :, None], r < hi[:, :, None])
    dst = jnp.where(mine, tok_v, ta + r % MOE_U)
    rows = nchunk
    return ((tok * rows).astype(jnp.int32), (dst * rows).astype(jnp.int32), g_sorted,
            tuple(flat(a) for a in (blk, exp, lo, hi, valid, first)))


def _moe_finish_kernel(x_ref, r_ref, mod_ref, sg_ref, su_ref, sd_ref, ln_ref, o_ref, *, alpha):
    tm, d = x_ref.shape[1], x_ref.shape[2]
    nchunk = d // 128
    x = x_ref[0]
    h = (_ln(x) * (1.0 + mod_ref[0, 0, 4:5, :]) + mod_ref[0, 0, 3:4, :]).astype(BF16)
    shared = _dot((_silu(_dot(h, sg_ref[...])) * _dot(h, su_ref[...])).astype(BF16), sd_ref[...])
    routed = jnp.concatenate([r_ref[0, pl.ds(j, tm, stride=nchunk), :] for j in range(nchunk)], axis=1)
    o_ref[0] = _ln(alpha * x + mod_ref[0, 0, 5:6, :] * (routed + shared)) * ln_ref[0:1] + ln_ref[1:2]


def _moe_finish(xa, routed, modv, sg, su, sd, ln, *, t_lat, tm, alpha, n_rows):
    b, ta, d = xa.shape
    nt_lat = t_lat // tm
    full = lambda a: pl.BlockSpec(a.shape, lambda bi, i: (0,) * a.ndim)
    return pl.pallas_call(
        functools.partial(_moe_finish_kernel, alpha=alpha),
        grid=(b, n_rows // tm),
        in_specs=[pl.BlockSpec((1, tm, d), lambda bi, i: (bi, i, 0)),
                  pl.BlockSpec((1, tm * (d // 128), 128), lambda bi, i: (bi, i, 0)),
                  pl.BlockSpec((1, 1, 6, d), lambda bi, i: (bi, i // nt_lat, 0, 0)),
                  full(sg), full(su), full(sd), full(ln)],
        out_specs=pl.BlockSpec((1, tm, d), lambda bi, i: (bi, i, 0)),
        out_shape=jax.ShapeDtypeStruct((b, n_rows, d), F32),
        compiler_params=_cp(("parallel", "parallel")),
        name="moe_finish",
    )(xa, routed, modv, sg, su, sd, ln)


def _rope_blocks(w_rope):
    k = w_rope.shape[0]
    ev, od = w_rope[:, 0::2], w_rope[:, 1::2]
    z64, z32 = jnp.zeros((k, 64), w_rope.dtype), jnp.zeros((k, 32), w_rope.dtype)
    return (jnp.concatenate([z64, ev, od, z32], 1), jnp.concatenate([z64, -od, ev, z32], 1))


def _layer_weights(l, w_in, mla_w_uq, mla_w_ukv):
    w = w_in[l]
    d = w.shape[0]
    wm, wr, wa = w[:, :N_MLSTM_IN], w[:, N_MLSTM_IN:N_MLSTM_IN + N_RWKV_IN], w[:, N_MLSTM_IN + N_RWKV_IN:]
    kr, krr = _rope_blocks(wa[:, Q_RANK + KV_RANK:])
    w_ext = jnp.concatenate([
        wm[:, :2 * MLSTM_W], wr, wm[:, 2 * MLSTM_W:4 * MLSTM_W],
        wm[:, 4 * MLSTM_W:], jnp.zeros((d, 128 - 4 * MLSTM_H), w.dtype),
        wa[:, :Q_RANK + KV_RANK], kr, krr], axis=1).astype(BF16)
    uq = mla_w_uq[l].reshape(Q_RANK, MLA_H, MLA_NOPE + MLA_ROPE)
    nope, ev, od = uq[:, :, :MLA_NOPE], uq[:, :, MLA_NOPE::2], uq[:, :, MLA_NOPE + 1::2]
    z64, z32 = jnp.zeros((Q_RANK, MLA_H, 64), F32), jnp.zeros((Q_RANK, MLA_H, 32), F32)
    wq = jnp.concatenate([nope, ev, od, z32], axis=2).reshape(Q_RANK, MLA_H * 128)
    wqr = jnp.concatenate([z64, -od, ev, z32], axis=2).reshape(Q_RANK, MLA_H * 128)
    ukv = mla_w_ukv[l].reshape(KV_RANK, MLA_H, MLA_NOPE + MLA_V)
    wk = jnp.concatenate([ukv[:, :, :MLA_NOPE], jnp.zeros((KV_RANK, MLA_H, 64), F32)], axis=2).reshape(KV_RANK, MLA_H * 128)
    wv = ukv[:, :, MLA_NOPE:].reshape(KV_RANK, MLA_H * MLA_V)
    return w_ext, wq.astype(BF16), wqr.astype(BF16), wk.astype(BF16), wv.T.astype(BF16)


def _rope_tables(t_lat, n_ctx):
    rows = t_lat // GRID_W
    row = jnp.repeat(jnp.arange(rows), GRID_W).astype(F32)
    col = jnp.tile(jnp.arange(GRID_W), rows).astype(F32)
    n_freq = MLA_ROPE // 4
    freq = ROPE_THETA ** (-jnp.arange(n_freq, dtype=F32) / n_freq)
    ang = jnp.concatenate([row[:, None] * freq, col[:, None] * freq], -1)
    cos, sin = jnp.cos(ang), jnp.sin(ang)
    one, zero = jnp.ones((t_lat, 64), F32), jnp.zeros((t_lat, 32), F32)
    cos_l = jnp.concatenate([one, cos, cos, zero], 1)
    sin_l = jnp.concatenate([0 * one, sin, sin, zero], 1)
    cos_c = jnp.concatenate([jnp.ones((n_ctx, 96), F32), jnp.zeros((n_ctx, 32), F32)], 1)
    return jnp.concatenate([cos_l, cos_c], 0), jnp.concatenate([sin_l, jnp.zeros((n_ctx, 128), F32)], 0)


def kernel(x, c, ctx, c_ctx, w_mod, b_mod, w_in, mlstm_conv, mlstm_gate_bias, mlstm_norm_w, rwkv_mu, rwkv_w0, rwkv_w_up, rwkv_a0, rwkv_a_up, rwkv_g_up, rwkv_k_k, rwkv_k_a, rwkv_r_k, rwkv_ln_w, rwkv_ln_b, mla_q_norm, mla_kv_norm, mla_w_uq, mla_w_ukv, w_out, ln1_w, ln1_b, router_w, router_bias, exp_w_gate, exp_w_up, exp_w_down, sh_w_gate, sh_w_up, sh_w_down, ln2_w, ln2_b):
    b, t_lat, d = x.shape
    n_ctx = ctx.shape[1]
    ta = t_lat + n_ctx
    depth = w_in.shape[0]
    alpha = (2 * depth) ** 0.25
    tm = 256
    cl = 256
    nv_moe = TOP_K * ta // MOE_RB + N_EXPERTS
    assert b + 1 <= 8 and n_ctx % tm == 0 and t_lat % tm == 0 and (TOP_K * ta) % MOE_RB == 0

    cc = jnp.concatenate([c, c_ctx[None], jnp.zeros((8 - b - 1, d), F32)], 0)
    mods = _modulation(cc, w_mod, b_mod)
    cos, sin = _rope_tables(t_lat, n_ctx)
    hsum = jnp.kron(jnp.eye(RWKV_H, dtype=F32), jnp.ones((RWKV_N, RWKV_N), F32))
    hmean = hsum / RWKV_N
    lane = jnp.arange(128, dtype=jnp.int32) // (128 // (b * RWKV_H))
    pats = jnp.broadcast_to((jnp.arange(8, dtype=jnp.int32) * (b * RWKV_H))[:, None, None] + lane, (8, RWKV_N, 128))

    exp_w = [w.astype(BF16).reshape((depth * N_EXPERTS,) + w.shape[2:]) for w in (exp_w_gate, exp_w_up, exp_w_down)]
    xa = jnp.concatenate([x, ctx], axis=1)
    for l in range(depth):
        m_lat = mods[l, :b].reshape(b, 1, 6, d)
        m_ctx = jnp.broadcast_to(mods[l, b].reshape(1, 1, 6, d), (b, 1, 6, d))
        modv = jnp.concatenate([m_lat, m_ctx], axis=1)
        w_ext, wq, wqr, wk, wv = _layer_weights(l, w_in, mla_w_uq, mla_w_ukv)
        mu = rwkv_mu[l].reshape(1, N_RWKV_IN)
        mla = (cos, sin, mla_q_norm[l][None], mla_kv_norm[l][None], wq, wqr, wk, wv)
        zm, zg, zr, q, k, vt = _in_proj(xa, modv, w_ext, mlstm_conv[l], mu, mla, t_lat=t_lat, tm=tm)

        gates = zg[:, :, :4 * MLSTM_H].reshape(b, ta, 2, 8)
        g_col = gates.transpose(0, 2, 1, 3)
        g_row = gates.transpose(0, 2, 3, 1)
        gb = mlstm_gate_bias[l].reshape(2, 8)
        h_m = _mlstm(zm, g_col, g_row, gb.reshape(2, 1, 8), gb.reshape(2, 8, 1), t_lat=t_lat, cl=cl)

        zeros = jnp.zeros((DECAY_LORA, RWKV_W), F32)
        wup = jnp.concatenate([jnp.concatenate([rwkv_w_up[l, 0], zeros], 1),
                               jnp.concatenate([zeros, rwkv_w_up[l, 1]], 1)], 0).astype(BF16)
        aup = jnp.concatenate([jnp.concatenate([rwkv_a_up[l, 0], zeros], 1),
                               jnp.concatenate([zeros, rwkv_a_up[l, 1]], 1)], 0).astype(BF16)
        vec = jnp.concatenate([rwkv_w0[l], rwkv_a0[l], rwkv_k_k[l][None], rwkv_k_a[l][None], rwkv_r_k[l][None],
                               jnp.zeros((1, RWKV_W), F32)], 0)
        pd, v_tok, g_tok, bonus = _rwkv_prep(zr, wup, aup, rwkv_g_up[l].astype(BF16), vec, hsum, tm=128)
        rep = 128 // (b * RWKV_H)
        nvh = RWKV_N // rep
        vs = v_tok.reshape(b, ta, RWKV_H, nvh, rep).transpose(1, 3, 0, 2, 4).reshape(ta, nvh, 128)
        o_f, o_b = _rwkv_scan(pd, pats, vs, t_lat=t_lat, ts=32)
        o_s = jnp.stack([o_f, o_b], 0).reshape(2, ta, nvh, b, RWKV_H, rep)
        o_s = o_s.transpose(0, 3, 1, 4, 2, 5).reshape(2, b, ta, RWKV_W)

        att_l = _attention(q, k, vt, q_off=0, n_q=t_lat, k_off=0, n_keys=ta, tq=tm)
        att_c = _attention(q, k, vt, q_off=t_lat, n_q=n_ctx, k_off=t_lat, n_keys=n_ctx, tq=tm)

        vec_o = jnp.concatenate([mlstm_norm_w[l][None], rwkv_ln_w[l][None], rwkv_ln_b[l][None],
                                 jnp.zeros((5, RWKV_W), F32)], 0)
        ln1 = jnp.stack([ln1_w[l], ln1_b[l]], 0)
        xa, ids, gts, h_rows = _out_proj(xa, h_m, zm, o_s, g_tok, bonus, att_l, att_c, modv, vec_o, hmean,
                                         w_out[l].astype(BF16), ln1, router_w[l].T,
                                         router_bias[l].reshape(N_EXPERTS, 1), t_lat=t_lat, tm=tm, alpha=alpha)

        tok, dst, g_sorted, tables = _moe_tables(ids, gts, ta=ta, nv=nv_moe, nchunk=d // 128)
        routed = _moe_routed(h_rows, tok, dst, g_sorted, tables, *exp_w, nv=nv_moe, layer=l)
        ln2 = jnp.stack([ln2_w[l], ln2_b[l]], 0)
        xa = _moe_finish(xa, routed, modv, sh_w_gate[l].astype(BF16), sh_w_up[l].astype(BF16),
                         sh_w_down[l].astype(BF16), ln2, t_lat=t_lat, tm=tm, alpha=alpha,
                         n_rows=t_lat if l == depth - 1 else ta)
    return xa
```

```python
import functools

import jax
import jax.numpy as jnp
import numpy as np
from jax import lax
from jax.experimental import pallas as pl
from jax.experimental.pallas import tpu as pltpu

F32 = jnp.float32
BF16 = jnp.bfloat16
HIGHEST = lax.Precision.HIGHEST

GRID_W = 64
MLSTM_H, MLSTM_DH = 4, 64
MLSTM_W = MLSTM_H * MLSTM_DH
RWKV_H, RWKV_N = 4, 64
RWKV_W = RWKV_H * RWKV_N
DECAY_LORA, AAA_LORA, GATE_LORA = 64, 64, 128
RWKV_GN_EPS = 64e-5
MLA_H, MLA_NOPE, MLA_ROPE, MLA_V = 8, 64, 32, 64
Q_RANK, KV_RANK = 256, 128
ROPE_THETA = 10000.0
N_MLSTM_IN = 4 * MLSTM_W + 4 * MLSTM_H
N_RWKV_IN = 3 * RWKV_W + 2 * DECAY_LORA + 2 * AAA_LORA + GATE_LORA
N_EXPERTS, TOP_K, N_GROUPS, TOPK_GROUPS = 64, 8, 8, 4
ROUTED_SCALE = 2.5
LN_EPS = 1e-6
NEG = -1e30

C_QK, C_R, C_V, C_O, C_G, C_A = 0, 512, 1664, 1920, 2176, 2304
N_SHIFT = 1664
N_A = 640
N_EXT = C_A + N_A
HALO = 8
VMEM_LIMIT = 56 * 1024 * 1024


def _cp(sem):
    return pltpu.CompilerParams(dimension_semantics=sem, vmem_limit_bytes=VMEM_LIMIT)


def _ln(x):
    mu = jnp.mean(x, axis=-1, keepdims=True)
    xc = x - mu
    return xc * lax.rsqrt(jnp.mean(xc * xc, axis=-1, keepdims=True) + LN_EPS)


def _sigmoid(x):
    return 1.0 / (1.0 + jnp.exp(-x))


def _silu(x):
    return x * _sigmoid(x)


def _log_sigmoid(x):
    return jnp.minimum(x, 0.0) - jnp.log(1.0 + jnp.exp(-jnp.abs(x)))


def _softplus(x):
    return jnp.maximum(x, 0.0) + jnp.log(1.0 + jnp.exp(-jnp.abs(x)))


def _nt_dot(a, b, **kw):
    return lax.dot_general(a, b, (((1,), (1,)), ((), ())), preferred_element_type=F32, **kw)


def _tn_dot(a, b, **kw):
    return lax.dot_general(a, b, (((0,), (0,)), ((), ())), preferred_element_type=F32, **kw)


def _dot(a, b, **kw):
    return jnp.dot(a, b, preferred_element_type=F32, **kw)


def _mod_kernel(c_ref, w_ref, b_ref, o_ref):
    o_ref[0] = _dot(_silu(c_ref[...]), w_ref[0], precision=HIGHEST) + b_ref[0]


def _modulation(cc, w_mod, b_mod):
    n_layers, d, n = w_mod.shape
    tn = 1536
    return pl.pallas_call(
        _mod_kernel,
        grid=(n_layers, n // tn),
        in_specs=[pl.BlockSpec((8, d), lambda l, j: (0, 0)),
                  pl.BlockSpec((1, d, tn), lambda l, j: (l, 0, j)),
                  pl.BlockSpec((1, 1, tn), lambda l, j: (l, 0, j))],
        out_specs=pl.BlockSpec((1, 8, tn), lambda l, j: (l, 0, j)),
        out_shape=jax.ShapeDtypeStruct((n_layers, 8, n), F32),
        compiler_params=_cp(("parallel", "parallel")),
        name="modulation",
    )(cc, w_mod, b_mod.reshape(n_layers, 1, n))


def _in_kernel(xp_ref, x_ref, xn_ref, mod_ref, w_ref, conv_ref, mu_ref, cos_ref, sin_ref, nq_ref, nkv_ref,
               wq_ref, wqr_ref, wk_ref, wv_ref, zm_ref, zg_ref, zr_ref, q_ref, k_ref, vt_ref, *, tm, nt_lat, nt):
    i = pl.program_id(1)
    has_prev = jnp.logical_and(i != 0, i != nt_lat)
    has_next = jnp.logical_and(i != nt_lat - 1, i != nt - 1)
    xt = jnp.concatenate([xp_ref[0], x_ref[0], xn_ref[0]], axis=0)
    sh = mod_ref[0, 0, 0:1, :]
    sc = mod_ref[0, 0, 1:2, :]
    h = (_ln(xt) * (1.0 + sc) + sh).astype(BF16)
    z = _dot(h, w_ref[...])
    rows = lax.broadcasted_iota(jnp.int32, (tm + 2 * HALO, 1), 0)
    lo = jnp.where(has_prev, 0, HALO)
    hi = jnp.where(has_next, tm + 2 * HALO, tm + HALO)
    keep = jnp.logical_and(rows >= lo, rows < hi)
    zs = jnp.where(keep, z[:, :N_SHIFT], 0.0)
    zc = zs[HALO:HALO + tm]
    zprev = pltpu.roll(zs, 1, axis=0)[HALO:HALO + tm]
    znext = pltpu.roll(zs, tm + 2 * HALO - 1, axis=0)[HALO:HALO + tm]
    cw = conv_ref[...]
    qk = (cw[0:1] * zprev[:, :C_R] + cw[1:2] * zc[:, :C_R] + cw[2:3] * znext[:, :C_R])
    qk = _silu(qk)
    lane = lax.broadcasted_iota(jnp.int32, (1, C_R), 1)
    qk = qk * jnp.where(lane < MLSTM_W, MLSTM_DH ** -0.5, 1.0)
    zm_ref[0, :, 0:512] = qk
    zm_ref[0, :, 512:1024] = z[HALO:HALO + tm, C_V:C_G]
    zg_ref[0] = z[HALO:HALO + tm, C_G:C_A]
    zr = zc[:, C_R:]
    zr_ref[0] = zr + mu_ref[...] * (0.5 * (zprev[:, C_R:] + znext[:, C_R:]) - zr)
    _mla_project(z[HALO:HALO + tm, C_A:], cos_ref, sin_ref, nq_ref, nkv_ref, wq_ref, wqr_ref, wk_ref, wv_ref,
                 q_ref, k_ref, vt_ref)


def _in_proj(xa, modv, w_ext, conv, mu, mla, *, t_lat, tm):
    b, ta, d = xa.shape
    full = lambda a: pl.BlockSpec(a.shape, lambda bi, i: (0,) * a.ndim)
    rope = pl.BlockSpec((tm, 128), lambda bi, i: (i, 0))
    nt, nt_lat = ta // tm, t_lat // tm
    nh = tm // HALO
    last = ta // HALO - 1
    kern = functools.partial(_in_kernel, tm=tm, nt_lat=nt_lat, nt=nt)
    return pl.pallas_call(
        kern,
        grid=(b, nt),
        in_specs=[pl.BlockSpec((1, HALO, d), lambda bi, i: (bi, jnp.maximum(i * nh - 1, 0), 0)),
                  pl.BlockSpec((1, tm, d), lambda bi, i: (bi, i, 0)),
                  pl.BlockSpec((1, HALO, d), lambda bi, i: (bi, jnp.minimum((i + 1) * nh, last), 0)),
                  pl.BlockSpec((1, 1, 6, d), lambda bi, i: (bi, i // nt_lat, 0, 0)),
                  pl.BlockSpec((d, N_EXT), lambda bi, i: (0, 0)),
                  pl.BlockSpec((3, C_R), lambda bi, i: (0, 0)),
                  pl.BlockSpec((1, N_RWKV_IN), lambda bi, i: (0, 0)),
                  rope, rope] + [full(a) for a in mla[2:]],
        out_specs=[pl.BlockSpec((1, tm, 1024), lambda bi, i: (bi, i, 0)),
                   pl.BlockSpec((1, tm, 128), lambda bi, i: (bi, i, 0)),
                   pl.BlockSpec((1, tm, N_RWKV_IN), lambda bi, i: (bi, i, 0)),
                   pl.BlockSpec((1, tm, 1024), lambda bi, i: (bi, i, 0)),
                   pl.BlockSpec((1, tm, 1024), lambda bi, i: (bi, i, 0)),
                   pl.BlockSpec((1, 512, tm), lambda bi, i: (bi, 0, i))],
        out_shape=[jax.ShapeDtypeStruct((b, ta, 1024), F32),
                   jax.ShapeDtypeStruct((b, ta, 128), F32),
                   jax.ShapeDtypeStruct((b, ta, N_RWKV_IN), F32),
                   jax.ShapeDtypeStruct((b, ta, 1024), BF16),
                   jax.ShapeDtypeStruct((b, ta, 1024), BF16),
                   jax.ShapeDtypeStruct((b, 512, ta), BF16)],
        compiler_params=_cp(("parallel", "parallel")),
        name="in_proj",
    )(xa, xa, xa, modv, w_ext, conv, mu, *mla)


def _mlstm_kernel(zm_ref, gc_ref, gr_ref, bc_ref, br_ref, o_ref, ct_ref, n_ref, m_ref, *, cl):
    d = pl.program_id(1)

    @pl.when(pl.program_id(2) == 0)
    def _():
        ct_ref[...] = jnp.zeros_like(ct_ref)
        n_ref[...] = jnp.zeros_like(n_ref)
        m_ref[...] = jnp.zeros_like(m_ref)

    sgn = 1 - 2 * d
    gc = gc_ref[0, 0] + bc_ref[0]
    gr = gr_ref[0, 0] + br_ref[0]
    lf_c = _log_sigmoid(gc)
    lf_r = _log_sigmoid(gr)
    ii = lax.broadcasted_iota(jnp.int32, (cl, cl), 0)
    jj = lax.broadcasted_iota(jnp.int32, (cl, cl), 1)
    mask = (jj - ii) * sgn <= 0
    mask_t = (ii - jj) * sgn <= 0
    cum_c = _dot(mask.astype(F32), lf_c, precision=HIGHEST)
    cum_r = _dot(lf_r, mask_t.astype(F32), precision=HIGHEST)
    outs = []
    for h in range(MLSTM_H):
        sl = slice(h * MLSTM_DH, (h + 1) * MLSTM_DH)
        q = zm_ref[0, :, sl]
        k = zm_ref[0, :, MLSTM_W + h * MLSTM_DH:MLSTM_W + (h + 1) * MLSTM_DH]
        v = zm_ref[0, :, 2 * MLSTM_W + h * MLSTM_DH:2 * MLSTM_W + (h + 1) * MLSTM_DH]
        qb, kb = q.astype(BF16), k.astype(BF16)
        li_c, li_r = gc[:, h:h + 1], gr[h:h + 1, :]
        cu_c, cu_r = cum_c[:, 4 + h:5 + h], cum_r[4 + h:5 + h, :]
        b_end = jnp.sum(lf_r[4 + h:5 + h, :], axis=1, keepdims=True)
        m_prev = m_ref[h:h + 1, 0:1]
        ct = ct_ref[h]
        nv = n_ref[h:h + 1, 0:MLSTM_DH]
        d_log = jnp.where(mask, cu_c - cu_r + li_r, NEG)
        g_log = cu_c + m_prev
        m_row = jnp.maximum(g_log, jnp.max(d_log, axis=1, keepdims=True))
        w_intra = jnp.exp(d_log - m_row) * _nt_dot(qb, kb)
        e_inter = jnp.exp(g_log - m_row)
        num = _dot(w_intra.astype(BF16), v.astype(BF16)) + e_inter * _dot(qb, ct.astype(BF16))
        den = jnp.sum(w_intra, axis=1, keepdims=True) + e_inter * jnp.sum(q * nv, axis=1, keepdims=True)
        outs.append(num / jnp.maximum(jnp.abs(den), jnp.exp(-m_row)))
        w_end = b_end - cu_c + li_c
        m_loc = jnp.max(w_end, axis=0, keepdims=True)
        e_end = jnp.exp(w_end - m_loc)
        m_new = jnp.maximum(b_end + m_prev, m_loc)
        a_old = jnp.exp(b_end + m_prev - m_new)
        a_loc = jnp.exp(m_loc - m_new)
        ct_ref[h] = a_old * ct + a_loc * _tn_dot(kb, (v * e_end).astype(BF16))
        n_ref[h:h + 1, 0:MLSTM_DH] = a_old * nv + a_loc * jnp.sum(k * e_end, axis=0, keepdims=True)
        m_ref[h:h + 1, :] = jnp.broadcast_to(m_new, (1, 128))
    o_ref[0, 0] = jnp.concatenate(outs, axis=1)


def _mlstm(zm, g_col, g_row, b_col, b_row, *, t_lat, cl):
    b, ta, _ = zm.shape
    nc, nc_lat = ta // cl, t_lat // cl
    nc_ctx = nc - nc_lat

    def chunk(di, c):
        fwd_idx = jnp.where(c < nc_ctx, nc_lat + c, c - nc_ctx)
        return jnp.where(di == 0, fwd_idx, nc - 1 - c)

    return pl.pallas_call(
        functools.partial(_mlstm_kernel, cl=cl),
        grid=(b, 2, nc),
        in_specs=[pl.BlockSpec((1, cl, 1024), lambda bi, di, c: (bi, chunk(di, c), 0)),
                  pl.BlockSpec((1, 1, cl, 8), lambda bi, di, c: (bi, di, chunk(di, c), 0)),
                  pl.BlockSpec((1, 1, 8, cl), lambda bi, di, c: (bi, di, 0, chunk(di, c))),
                  pl.BlockSpec((1, 1, 8), lambda bi, di, c: (di, 0, 0)),
                  pl.BlockSpec((1, 8, 1), lambda bi, di, c: (di, 0, 0))],
        out_specs=pl.BlockSpec((1, 1, cl, MLSTM_W), lambda bi, di, c: (bi, di, chunk(di, c), 0)),
        out_shape=jax.ShapeDtypeStruct((b, 2, ta, MLSTM_W), F32),
        scratch_shapes=[pltpu.VMEM((MLSTM_H, MLSTM_DH, MLSTM_DH), F32),
                        pltpu.VMEM((8, 128), F32),
                        pltpu.VMEM((8, 128), F32)],
        compiler_params=_cp(("parallel", "parallel", "arbitrary")),
        name="mlstm",
    )(zm, g_col, g_row, b_col, b_row)


def _rwkv_prep_kernel(zr_ref, wup_ref, aup_ref, gup_ref, vec_ref, hsum_ref, pd_ref, v_ref, g_ref, bonus_ref):
    nb, tm = zr_ref.shape[0], zr_ref.shape[1]
    nbh = nb * RWKV_H
    vec = vec_ref[...]
    hsum = hsum_ref[...]
    zero = jnp.zeros((tm, RWKV_N), F32)
    for bi in range(nb):
        z = zr_ref[bi]
        zr_, zk, zv = z[:, 0:256], z[:, 256:512], z[:, 512:768]
        zw, za, zg = z[:, 768:896], z[:, 896:1024], z[:, 1024:1152]
        w_raw = _dot(jnp.tanh(zw).astype(BF16), wup_ref[...])
        a_raw = _dot(za.astype(BF16), aup_ref[...])
        g_ref[bi] = _dot(_sigmoid(zg).astype(BF16), gup_ref[...])
        kk = zk * vec[4:5]
        kk_ss = _dot(kk * kk, hsum, precision=HIGHEST)
        kk = kk * lax.rsqrt(jnp.maximum(kk_ss, 1e-12))
        k_sum = jnp.zeros_like(zk)
        for di in range(2):
            sl = slice(di * RWKV_W, (di + 1) * RWKV_W)
            decay = jnp.exp(-jnp.exp(-_softplus(-(vec[di:di + 1] + w_raw[:, sl])) - 0.5))
            a = _sigmoid(vec[2 + di:3 + di] + a_raw[:, sl])
            k_dir = zk * (1.0 + (a - 1.0) * vec[5:6])
            k_sum = k_sum + k_dir
            ka = kk * a
            for h in range(RWKV_H):
                hs = slice(h * RWKV_N, (h + 1) * RWKV_N)
                rows = pl.ds(bi * RWKV_H + h, tm, stride=nbh)
                pd_ref[di, 0, rows, :] = jnp.concatenate([decay[:, hs], ka[:, hs]], axis=1)
                pd_ref[di, 1, rows, :] = jnp.concatenate([k_dir[:, hs], kk[:, hs]], axis=1)
                pd_ref[di, 2, rows, :] = jnp.concatenate([zr_[:, hs], zero], axis=1)
        v_ref[bi] = zv
        bonus_ref[bi] = _dot(zr_ * vec[6:7] * k_sum, hsum, precision=HIGHEST) * zv


def _rwkv_prep(zr, wup, aup, gup, vec, hsum, *, tm):
    b, ta, _ = zr.shape
    nbh = b * RWKV_H
    tok = pl.BlockSpec((b, tm, RWKV_W), lambda i: (0, i, 0))
    full = lambda a: pl.BlockSpec(a.shape, lambda i: (0,) * a.ndim)
    return pl.pallas_call(
        _rwkv_prep_kernel,
        grid=(ta // tm,),
        in_specs=[pl.BlockSpec((b, tm, N_RWKV_IN), lambda i: (0, i, 0)),
                  full(wup), full(aup), full(gup), full(vec), full(hsum)],
        out_specs=[pl.BlockSpec((2, 3, tm * nbh, 128), lambda i: (0, 0, i, 0)), tok, tok, tok],
        out_shape=[jax.ShapeDtypeStruct((2, 3, ta * nbh, 128), F32)] + [jax.ShapeDtypeStruct((b, ta, RWKV_W), F32)] * 3,
        compiler_params=_cp(("parallel",)),
        name="rwkv_prep",
    )(zr, wup, aup, gup, vec, hsum)


def _rwkv_scan_kernel(pf_ref, pb_ref, pat_ref, vf_ref, vb_ref, of_ref, ob_ref, st_ref, x_ref, tt_ref, *, ts, nvh, nbh):
    ng = ts // 8
    rows = 8 * nbh

    @pl.when(pl.program_id(0) == 0)
    def _():
        st_ref[...] = jnp.zeros_like(st_ref)

    def transpose(di, p_ref, grp):
        sl = pl.ds(pl.multiple_of(grp * rows, rows), rows)
        for j in range(3):
            tt_ref[di, j] = p_ref[0, j, sl, :].T

    def update(di, u, v_ref, o_ref, t):
        pat = pat_ref[u]
        for j, (tj, half) in enumerate(((0, 0), (0, 1), (1, 0), (1, 1), (2, 0))):
            x_ref[di,j] = jnp.take_along_axis(tt_ref[di, tj, half * RWKV_N:(half + 1) * RWKV_N, :], pat, axis=1)
        vt = v_ref[t]
        nkb = RWKV_N // 8
        kblk = lambda kb: pl.ds(kb * 8, 8)
        acc = [None] * nvh
        for kb in range(nkb):
            kk = x_ref[di,3, kblk(kb), :]
            for vh in range(nvh):
                term = st_ref[di, vh, kblk(kb), :] * kk
                acc[vh] = term if kb == 0 else acc[vh] + term
        sa = [jnp.sum(a, axis=0, keepdims=True) for a in acc]
        out = [None] * nvh
        for kb in range(nkb):
            w, ka = x_ref[di,0, kblk(kb), :], x_ref[di,1, kblk(kb), :]
            k, r = x_ref[di,2, kblk(kb), :], x_ref[di,4, kblk(kb), :]
            for vh in range(nvh):
                sv = st_ref[di, vh, kblk(kb), :] * w - sa[vh] * ka + vt[vh:vh + 1, :] * k
                st_ref[di, vh, kblk(kb), :] = sv
                out[vh] = sv * r if kb == 0 else out[vh] + sv * r
        o_ref[t] = jnp.concatenate([jnp.sum(o, axis=0, keepdims=True) for o in out], axis=0)

    def group(g, carry):
        transpose(0, pf_ref, g)
        transpose(1, pb_ref, ng - 1 - g)
        for u in range(8):
            update(0, u, vf_ref, of_ref, g * 8 + u)
            update(1, 7 - u, vb_ref, ob_ref, ts - 1 - (g * 8 + u))
        return carry

    lax.fori_loop(0, ng, group, 0)


def _rwkv_scan(pd, pats, vs, *, t_lat, ts):
    ta, nvh, nl = vs.shape
    nbh = pd.shape[2] // ta
    nt, nt_lat = ta // ts, t_lat // ts
    fwd = lambda i: (i + nt_lat) % nt
    bwd = lambda i: nt - 1 - i
    return pl.pallas_call(
        functools.partial(_rwkv_scan_kernel, ts=ts, nvh=nvh, nbh=nbh),
        grid=(nt,),
        in_specs=[pl.BlockSpec((1, 3, ts * nbh, 128), lambda i: (0, 0, fwd(i), 0)),
                  pl.BlockSpec((1, 3, ts * nbh, 128), lambda i: (1, 0, bwd(i), 0)),
                  pl.BlockSpec(pats.shape, lambda i: (0, 0, 0)),
                  pl.BlockSpec((ts, nvh, nl), lambda i: (fwd(i), 0, 0)),
                  pl.BlockSpec((ts, nvh, nl), lambda i: (bwd(i), 0, 0))],
        out_specs=[pl.BlockSpec((ts, nvh, nl), lambda i: (fwd(i), 0, 0)),
                   pl.BlockSpec((ts, nvh, nl), lambda i: (bwd(i), 0, 0))],
        out_shape=[jax.ShapeDtypeStruct((ta, nvh, nl), F32)] * 2,
        scratch_shapes=[pltpu.VMEM((2, nvh, RWKV_N, nl), F32), pltpu.VMEM((2, 5, RWKV_N, nl), F32),
                        pltpu.VMEM((2, 3, 128, 8 * nbh), F32)],
        compiler_params=_cp(("arbitrary",)),
        name="rwkv_scan",
    )(pd, pd, pats, vs, vs)


def _mla_project(za, cos_ref, sin_ref, nq_ref, nkv_ref, wq_ref, wqr_ref, wk_ref, wv_ref, q_ref, k_ref, v_ref):
    cq, ckv = za[:, 0:Q_RANK], za[:, Q_RANK:Q_RANK + KV_RANK]
    kr, krr = za[:, 384:512], za[:, 512:640]
    cos, sin = cos_ref[...], sin_ref[...]
    cqn = (cq * lax.rsqrt(jnp.mean(cq * cq, axis=-1, keepdims=True) + 1e-6) * nq_ref[...]).astype(BF16)
    ckvn = (ckv * lax.rsqrt(jnp.mean(ckv * ckv, axis=-1, keepdims=True) + 1e-6) * nkv_ref[...]).astype(BF16)
    q = _dot(cqn, wq_ref[...])
    qr = _dot(cqn, wqr_ref[...])
    kn = _dot(ckvn, wk_ref[...])
    v_ref[0] = _nt_dot(wv_ref[...], ckvn).astype(BF16)
    k_rope = kr * cos + krr * sin
    scale = (MLA_NOPE + MLA_ROPE) ** -0.5 * np.log2(np.e)
    for h in range(MLA_H):
        sl = slice(h * 128, (h + 1) * 128)
        q_ref[0, :, sl] = ((q[:, sl] * cos + qr[:, sl] * sin) * scale).astype(BF16)
        k_ref[0, :, sl] = (kn[:, sl] + k_rope).astype(BF16)


def _attn_kernel(q_ref, k_ref, vt_ref, o_ref):
    outs = []
    for h in range(ATTN_HEADS):
        sl = slice(h * 128, (h + 1) * 128)
        pair = h // 2
        st = _nt_dot(k_ref[0, :, sl], q_ref[0, :, sl])
        m = jnp.max(st, axis=0, keepdims=True)
        p = jnp.exp2(st - m)
        l = jnp.sum(p, axis=0, keepdims=True)
        ot = _dot(vt_ref[0, pair * 128:(pair + 1) * 128, :], p.astype(BF16))
        outs.append(ot[(h % 2) * MLA_V:(h % 2 + 1) * MLA_V] / l)
    o_ref[0] = jnp.concatenate(outs, axis=0).T


ATTN_HEADS = 4


def _attention(q, k, vt, *, q_off, n_q, k_off, n_keys, tq):
    b = q.shape[0]
    qo, ko = q_off // tq, k_off // n_keys
    hq, hv = ATTN_HEADS * 128, ATTN_HEADS * MLA_V
    return pl.pallas_call(
        _attn_kernel,
        grid=(b, MLA_H // ATTN_HEADS, n_q // tq),
        in_specs=[pl.BlockSpec((1, tq, hq), lambda bi, hp, i: (bi, qo + i, hp)),
                  pl.BlockSpec((1, n_keys, hq), lambda bi, hp, i: (bi, ko, hp)),
                  pl.BlockSpec((1, hv, n_keys), lambda bi, hp, i: (bi, hp, ko))],
        out_specs=pl.BlockSpec((1, tq, hv), lambda bi, hp, i: (bi, i, hp)),
        out_shape=jax.ShapeDtypeStruct((b, n_q, 512), F32),
        compiler_params=_cp(("parallel", "parallel", "parallel")),
        name="mla_attention",
    )(q, k, vt)


def _out_kernel(x_ref, hf_ref, hb_ref, zm_ref, of_ref, ob_ref, g_ref, bonus_ref, attl_ref, attc_ref, mod_ref, vec_ref,
                hmean_ref, wo_ref, ln_ref, o_ref, *, alpha, nt_lat):
    hmean = hmean_ref[...]
    att = jnp.where(pl.program_id(1) < nt_lat, attl_ref[0], attc_ref[0])

    def head_norm(y, eps):
        mu = _dot(y, hmean, precision=HIGHEST)
        yc = y - mu
        return yc * lax.rsqrt(_dot(yc * yc, hmean, precision=HIGHEST) + eps)

    vec = vec_ref[...]
    zo = zm_ref[0]
    m_mix = _sigmoid(zo) * (head_norm(hf_ref[0, 0] + hb_ref[0, 0], 1e-6) * vec[0:1])
    r_o = head_norm(of_ref[0, 0] + ob_ref[0, 0], RWKV_GN_EPS) * vec[1:2] + vec[2:3] + bonus_ref[0]
    r_mix = r_o * g_ref[0]
    mix = jnp.concatenate([m_mix, r_mix, att], axis=1).astype(BF16)
    y = _dot(mix, wo_ref[...])
    gate = mod_ref[0, 0, 2:3, :]
    o_ref[0] = _ln(alpha * x_ref[0] + gate * y) * ln_ref[0:1] + ln_ref[1:2]


def _out_proj(xa, h_m, zm, o_r, g, bonus, att_l, att_c, modv, vec, hmean, wo, ln, *, t_lat, tm, alpha):
    b, ta, d = xa.shape
    nt_lat = t_lat // tm
    tok = lambda w: pl.BlockSpec((1, tm, w), lambda bi, i: (bi, i, 0))
    full = lambda a: pl.BlockSpec(a.shape, lambda bi, i: (0,) * a.ndim)
    return pl.pallas_call(
        functools.partial(_out_kernel, alpha=alpha, nt_lat=nt_lat),
        grid=(b, ta // tm),
        in_specs=[tok(d),
                  pl.BlockSpec((1, 1, tm, MLSTM_W), lambda bi, i: (bi, 0, i, 0)),
                  pl.BlockSpec((1, 1, tm, MLSTM_W), lambda bi, i: (bi, 1, i, 0)),
                  pl.BlockSpec((1, tm, MLSTM_W), lambda bi, i: (bi, i, 3)),
                  pl.BlockSpec((1, 1, tm, RWKV_W), lambda bi, i: (0, bi, i, 0)),
                  pl.BlockSpec((1, 1, tm, RWKV_W), lambda bi, i: (1, bi, i, 0)),
                  tok(RWKV_W), tok(RWKV_W),
                  pl.BlockSpec((1, tm, 512), lambda bi, i: (bi, jnp.minimum(i, nt_lat - 1), 0)),
                  pl.BlockSpec((1, tm, 512), lambda bi, i: (bi, jnp.maximum(i - nt_lat, 0), 0)),
                  pl.BlockSpec((1, 1, 6, d), lambda bi, i: (bi, i // nt_lat, 0, 0)),
                  full(vec), full(hmean), full(wo), full(ln)],
        out_specs=tok(d),
        out_shape=jax.ShapeDtypeStruct((b, ta, d), F32),
        compiler_params=_cp(("parallel", "parallel")),
        name="out_proj",
    )(xa, h_m, h_m, zm, o_r, o_r, g, bonus, att_l, att_c, modv, vec, hmean, wo, ln)


def _first_max(x, idx, axis):
    mx = jnp.max(x, axis=axis, keepdims=True)
    first = jnp.min(jnp.where(x == mx, idx, 1 << 20), axis=axis, keepdims=True)
    return mx, first, idx == first


def _router_kernel(x_ref, mod_ref, w_ref, bias_ref, id_ref, gate_ref, h_ref):
    tm = x_ref.shape[1]
    sh, sc = mod_ref[0, 0, 3:4, :], mod_ref[0, 0, 4:5, :]
    h = _ln(x_ref[0]) * (1.0 + sc) + sh
    for j in range(h.shape[1] // 128):
        h_ref[0, pl.ds(j, tm, stride=h.shape[1] // 128), :] = h[:, j * 128:(j + 1) * 128]
    scores = _sigmoid(_nt_dot(w_ref[...], h, precision=HIGHEST))
    biased = scores + bias_ref[...]
    gsz = N_EXPERTS // N_GROUPS
    b3 = biased.reshape(N_GROUPS, gsz, tm)
    s3 = scores.reshape(N_GROUPS, gsz, tm)
    e_idx = lax.broadcasted_iota(jnp.int32, (N_GROUPS, gsz, tm), 1)
    m1, _, hit = _first_max(b3, e_idx, 1)
    m2 = jnp.max(jnp.where(hit, -jnp.inf, b3), axis=1, keepdims=True)
    gscore = m1 + m2
    g_idx = lax.broadcasted_iota(jnp.int32, (N_GROUPS, 1, tm), 0)
    gsel = jnp.zeros((N_GROUPS, 1, tm), F32)
    for _ in range(TOPK_GROUPS):
        _, _, hit = _first_max(jnp.where(gsel > 0, -jnp.inf, gscore), g_idx, 0)
        gsel = jnp.where(hit, 1.0, gsel)
    cand = jnp.where(jnp.broadcast_to(gsel, b3.shape) > 0, b3, -jnp.inf)
    x_idx = g_idx * gsz + e_idx
    sel = jnp.zeros(b3.shape, F32)
    ids, picked = [], []
    for _ in range(TOP_K):
        _, first, hit = _first_max(jnp.where(sel > 0, -jnp.inf, cand), x_idx, (0, 1))
        sel = jnp.where(hit, 1.0, sel)
        ids.append(first[0])
        picked.append(jnp.sum(jnp.where(hit, s3, 0.0), axis=(0, 1), keepdims=True)[0])
    picked = jnp.concatenate(picked, axis=0)
    id_ref[0] = jnp.concatenate(ids, axis=0)
    gate_ref[0] = ROUTED_SCALE * picked / jnp.sum(picked, axis=0, keepdims=True)


def _router(xa, modv, w_t, bias, *, t_lat, tm):
    b, ta, d = xa.shape
    nt_lat = t_lat // tm
    return pl.pallas_call(
        _router_kernel,
        grid=(b, ta // tm),
        in_specs=[pl.BlockSpec((1, tm, d), lambda bi, i: (bi, i, 0)),
                  pl.BlockSpec((1, 1, 6, d), lambda bi, i: (bi, i // nt_lat, 0, 0)),
                  pl.BlockSpec((N_EXPERTS, d), lambda bi, i: (0, 0)),
                  pl.BlockSpec((N_EXPERTS, 1), lambda bi, i: (0, 0))],
        out_specs=[pl.BlockSpec((1, TOP_K, tm), lambda bi, i: (bi, 0, i)),
                   pl.BlockSpec((1, TOP_K, tm), lambda bi, i: (bi, 0, i)),
                   pl.BlockSpec((1, tm * (d // 128), 128), lambda bi, i: (bi, i, 0))],
        out_shape=[jax.ShapeDtypeStruct((b, TOP_K, ta), jnp.int32),
                   jax.ShapeDtypeStruct((b, TOP_K, ta), F32),
                   jax.ShapeDtypeStruct((b, ta * (d // 128), 128), F32)],
        compiler_params=_cp(("parallel", "parallel")),
        name="router",
    )(xa, modv, w_t, bias)


MOE_RB = 256
MOE_PITCH = MOE_RB + 8
MOE_U = 8


def _moe_routed_kernel(vblk_ref, vexp_ref, vlo_ref, vhi_ref, vvalid_ref, vfirst_ref,
                       tok_ref, dst_ref, gate_ref, src_ref, wg_ref, wu_ref, wd_ref, acc_ref, xt_ref, yt_ref, xb_ref,
                       gc_ref, *, nv, nchunk):
    bi, v = pl.program_id(0), pl.program_id(1)
    pos = bi * nv + v
    lo, hi = vlo_ref[pos], vhi_ref[pos]

    @pl.when(v == 0)
    def _():
        acc_ref[...] = jnp.zeros_like(acc_ref)

    @pl.when(vfirst_ref[pos] == 1)
    def _():
        for r in range(MOE_RB):
            off = pl.multiple_of(tok_ref[0, 0, 0, r], nchunk)
            xt_ref[pl.ds(r, nchunk, stride=MOE_PITCH), :] = src_ref[0, pl.ds(off, nchunk), :]
        for j in range(nchunk):
            xb_ref[:, j * 128:(j + 1) * 128] = xt_ref[j * MOE_PITCH:j * MOE_PITCH + MOE_RB, :].astype(BF16)
        ii = lax.broadcasted_iota(jnp.int32, (MOE_RB, MOE_RB), 0)
        jj = lax.broadcasted_iota(jnp.int32, (MOE_RB, MOE_RB), 1)
        gc_ref[...] = jnp.sum(jnp.where(ii == jj, gate_ref[0, 0], 0.0), axis=1, keepdims=True)

    @pl.when(vvalid_ref[pos] == 1)
    def _():
        x = xb_ref[...]
        row = lax.broadcasted_iota(jnp.int32, (MOE_RB, 1), 0)
        gcol = jnp.where(jnp.logical_and(row >= lo, row < hi), gc_ref[...], 0.0)
        a = _silu(_dot(x, wg_ref[0])) * _dot(x, wu_ref[0]) * gcol
        y = _dot(a.astype(BF16), wd_ref[0])
        for j in range(nchunk):
            yt_ref[j * MOE_PITCH:j * MOE_PITCH + MOE_RB, :] = y[:, j * 128:(j + 1) * 128]
        for g in range(MOE_RB // MOE_U):
            pending = []
            for u in range(MOE_U):
                r = g * MOE_U + u
                off = pl.multiple_of(dst_ref[0, 0, 0, r], nchunk)
                pending.append((off, acc_ref[0, pl.ds(off, nchunk), :] + yt_ref[pl.ds(r, nchunk, stride=MOE_PITCH), :]))
            for off, val in pending:
                acc_ref[0, pl.ds(off, nchunk), :] = val


def _moe_routed(h_rows, tok, dst, gate, tables, wg, wu, wd, *, nv, layer):
    b, rows, _ = h_rows.shape
    d, de = wg.shape[1], wg.shape[2]
    nchunk = d // 128
    nblk = tok.shape[1] // MOE_RB
    blk = lambda bi, v, vblk, *_: (bi, vblk[bi * nv + v], 0, 0)
    exp = lambda bi, v, vblk, vexp, *_: (layer * N_EXPERTS + vexp[bi * nv + v], 0, 0)
    grid_spec = pltpu.PrefetchScalarGridSpec(
        num_scalar_prefetch=6,
        grid=(b, nv),
        in_specs=[pl.BlockSpec((1, 1, 1, MOE_RB), blk, memory_space=pltpu.SMEM),
                  pl.BlockSpec((1, 1, 1, MOE_RB), lambda bi, v, *_: (bi, v, 0, 0), memory_space=pltpu.SMEM),
                  pl.BlockSpec((1, 1, 1, MOE_RB), blk),
                  pl.BlockSpec((1, rows, 128), lambda bi, v, *_: (bi, 0, 0), pipeline_mode=pl.Buffered(1)),
                  pl.BlockSpec((1, d, de), exp),
                  pl.BlockSpec((1, d, de), exp),
                  pl.BlockSpec((1, de, d), exp)],
        out_specs=pl.BlockSpec((1, rows + MOE_U * nchunk, 128), lambda bi, v, *_: (bi, 0, 0),
                               pipeline_mode=pl.Buffered(1)),
        scratch_shapes=[pltpu.VMEM((nchunk * MOE_PITCH, 128), F32), pltpu.VMEM((nchunk * MOE_PITCH, 128), F32),
                        pltpu.VMEM((MOE_RB, d), BF16), pltpu.VMEM((MOE_RB, 1), F32)],
    )
    return pl.pallas_call(
        functools.partial(_moe_routed_kernel, nv=nv, nchunk=nchunk),
        grid_spec=grid_spec,
        out_shape=jax.ShapeDtypeStruct((b, rows + MOE_U * nchunk, 128), F32),
        compiler_params=_cp(("parallel", "arbitrary")),
        name="moe_routed",
    )(*tables, tok.reshape(b, nblk, 1, MOE_RB), dst.reshape(b, nv, 1, MOE_RB), gate.reshape(b, nblk, 1, MOE_RB),
      h_rows, wg, wu, wd)


def _moe_tables(ids, gates, *, ta, nv, nchunk):
    b = ids.shape[0]
    n = TOP_K * ta
    nblk = n // MOE_RB
    t_idx = jnp.broadcast_to(jnp.arange(ta, dtype=jnp.int32), (b, TOP_K, ta))
    keys, g_sorted = lax.sort(((ids * ta + t_idx).reshape(b, n), gates.reshape(b, n)), dimension=1, num_keys=1)
    tok, e_sorted = keys % ta, keys // ta
    ends = jnp.sum(e_sorted[:, None, :] <= jnp.arange(N_EXPERTS, dtype=jnp.int32)[None, :, None], axis=2)
    starts = ends - jnp.sum(e_sorted[:, None, :] == jnp.arange(N_EXPERTS, dtype=jnp.int32)[None, :, None], axis=2)
    e_lo, e_hi = e_sorted[:, ::MOE_RB], e_sorted[:, MOE_RB - 1::MOE_RB]
    n_vis = e_hi - e_lo + 1
    v_end = jnp.cumsum(n_vis, axis=1)
    v_start = v_end - n_vis
    v = jnp.arange(nv, dtype=jnp.int32)
    blk = jnp.minimum(jnp.sum(v_end[:, None, :] <= v[None, :, None], axis=2), nblk - 1).astype(jnp.int32)
    take = lambda a, i: jnp.take_along_axis(a, i, axis=1)
    valid = v[None, :] < v_end[:, -1:]
    exp = jnp.where(valid, take(e_lo, blk) + v[None, :] - take(v_start, blk), take(e_hi, blk)).astype(jnp.int32)
    lo = jnp.clip(take(starts, exp) - blk * MOE_RB, 0, MOE_RB)
    hi = jnp.where(valid, jnp.clip(take(ends, exp) - blk * MOE_RB, 0, MOE_RB), 0)
    first = jnp.logical_and(valid, v[None, :] == take(v_start, blk))
    flat = lambda a: a.astype(jnp.int32).reshape(-1)
    r = jnp.arange(MOE_RB, dtype=jnp.int32)
    tok_v = jnp.take_along_axis(tok.reshape(b, nblk, MOE_RB), blk[:, :, None], axis=1)
    mine = jnp.logical_and(r >= lo[:, :, None], r < hi[:, :, None])
    dst = jnp.where(mine, tok_v, ta + r % MOE_U)
    rows = nchunk
    return ((tok * rows).astype(jnp.int32), (dst * rows).astype(jnp.int32), g_sorted,
            tuple(flat(a) for a in (blk, exp, lo, hi, valid, first)))


def _moe_finish_kernel(x_ref, r_ref, mod_ref, sg_ref, su_ref, sd_ref, ln_ref, o_ref, *, alpha):
    tm, d = x_ref.shape[1], x_ref.shape[2]
    nchunk = d // 128
    x = x_ref[0]
    h = (_ln(x) * (1.0 + mod_ref[0, 0, 4:5, :]) + mod_ref[0, 0, 3:4, :]).astype(BF16)
    shared = _dot((_silu(_dot(h, sg_ref[...])) * _dot(h, su_ref[...])).astype(BF16), sd_ref[...])
    routed = jnp.concatenate([r_ref[0, pl.ds(j, tm, stride=nchunk), :] for j in range(nchunk)], axis=1)
    o_ref[0] = _ln(alpha * x + mod_ref[0, 0, 5:6, :] * (routed + shared)) * ln_ref[0:1] + ln_ref[1:2]


def _moe_finish(xa, routed, modv, sg, su, sd, ln, *, t_lat, tm, alpha, n_rows):
    b, ta, d = xa.shape
    nt_lat = t_lat // tm
    full = lambda a: pl.BlockSpec(a.shape, lambda bi, i: (0,) * a.ndim)
    return pl.pallas_call(
        functools.partial(_moe_finish_kernel, alpha=alpha),
        grid=(b, n_rows // tm),
        in_specs=[pl.BlockSpec((1, tm, d), lambda bi, i: (bi, i, 0)),
                  pl.BlockSpec((1, tm * (d // 128), 128), lambda bi, i: (bi, i, 0)),
                  pl.BlockSpec((1, 1, 6, d), lambda bi, i: (bi, i // nt_lat, 0, 0)),
                  full(sg), full(su), full(sd), full(ln)],
        out_specs=pl.BlockSpec((1, tm, d), lambda bi, i: (bi, i, 0)),
        out_shape=jax.ShapeDtypeStruct((b, n_rows, d), F32),
        compiler_params=_cp(("parallel", "parallel")),
        name="moe_finish",
    )(xa, routed, modv, sg, su, sd, ln)


def _rope_blocks(w_rope):
    k = w_rope.shape[0]
    ev, od = w_rope[:, 0::2], w_rope[:, 1::2]
    z64, z32 = jnp.zeros((k, 64), w_rope.dtype), jnp.zeros((k, 32), w_rope.dtype)
    return (jnp.concatenate([z64, ev, od, z32], 1), jnp.concatenate([z64, -od, ev, z32], 1))


def _layer_weights(l, w_in, mla_w_uq, mla_w_ukv):
    w = w_in[l]
    d = w.shape[0]
    wm, wr, wa = w[:, :N_MLSTM_IN], w[:, N_MLSTM_IN:N_MLSTM_IN + N_RWKV_IN], w[:, N_MLSTM_IN + N_RWKV_IN:]
    kr, krr = _rope_blocks(wa[:, Q_RANK + KV_RANK:])
    w_ext = jnp.concatenate([
        wm[:, :2 * MLSTM_W], wr, wm[:, 2 * MLSTM_W:4 * MLSTM_W],
        wm[:, 4 * MLSTM_W:], jnp.zeros((d, 128 - 4 * MLSTM_H), w.dtype),
        wa[:, :Q_RANK + KV_RANK], kr, krr], axis=1).astype(BF16)
    uq = mla_w_uq[l].reshape(Q_RANK, MLA_H, MLA_NOPE + MLA_ROPE)
    nope, ev, od = uq[:, :, :MLA_NOPE], uq[:, :, MLA_NOPE::2], uq[:, :, MLA_NOPE + 1::2]
    z64, z32 = jnp.zeros((Q_RANK, MLA_H, 64), F32), jnp.zeros((Q_RANK, MLA_H, 32), F32)
    wq = jnp.concatenate([nope, ev, od, z32], axis=2).reshape(Q_RANK, MLA_H * 128)
    wqr = jnp.concatenate([z64, -od, ev, z32], axis=2).reshape(Q_RANK, MLA_H * 128)
    ukv = mla_w_ukv[l].reshape(KV_RANK, MLA_H, MLA_NOPE + MLA_V)
    wk = jnp.concatenate([ukv[:, :, :MLA_NOPE], jnp.zeros((KV_RANK, MLA_H, 64), F32)], axis=2).reshape(KV_RANK, MLA_H * 128)
    wv = ukv[:, :, MLA_NOPE:].reshape(KV_RANK, MLA_H * MLA_V)
    return w_ext, wq.astype(BF16), wqr.astype(BF16), wk.astype(BF16), wv.T.astype(BF16)


def _rope_tables(t_lat, n_ctx):
    rows = t_lat // GRID_W
    row = jnp.repeat(jnp.arange(rows), GRID_W).astype(F32)
    col = jnp.tile(jnp.arange(GRID_W), rows).astype(F32)
    n_freq = MLA_ROPE // 4
    freq = ROPE_THETA ** (-jnp.arange(n_freq, dtype=F32) / n_freq)
    ang = jnp.concatenate([row[:, None] * freq, col[:, None] * freq], -1)
    cos, sin = jnp.cos(ang), jnp.sin(ang)
    one, zero = jnp.ones((t_lat, 64), F32), jnp.zeros((t_lat, 32), F32)
    cos_l = jnp.concatenate([one, cos, cos, zero], 1)
    sin_l = jnp.concatenate([0 * one, sin, sin, zero], 1)
    cos_c = jnp.concatenate([jnp.ones((n_ctx, 96), F32), jnp.zeros((n_ctx, 32), F32)], 1)
    return jnp.concatenate([cos_l, cos_c], 0), jnp.concatenate([sin_l, jnp.zeros((n_ctx, 128), F32)], 0)


def kernel(x, c, ctx, c_ctx, w_mod, b_mod, w_in, mlstm_conv, mlstm_gate_bias, mlstm_norm_w, rwkv_mu, rwkv_w0, rwkv_w_up, rwkv_a0, rwkv_a_up, rwkv_g_up, rwkv_k_k, rwkv_k_a, rwkv_r_k, rwkv_ln_w, rwkv_ln_b, mla_q_norm, mla_kv_norm, mla_w_uq, mla_w_ukv, w_out, ln1_w, ln1_b, router_w, router_bias, exp_w_gate, exp_w_up, exp_w_down, sh_w_gate, sh_w_up, sh_w_down, ln2_w, ln2_b):
    b, t_lat, d = x.shape
    n_ctx = ctx.shape[1]
    ta = t_lat + n_ctx
    depth = w_in.shape[0]
    alpha = (2 * depth) ** 0.25
    tm = 256
    cl = 256
    nv_moe = TOP_K * ta // MOE_RB + N_EXPERTS
    assert b + 1 <= 8 and n_ctx % tm == 0 and t_lat % tm == 0 and (TOP_K * ta) % MOE_RB == 0

    cc = jnp.concatenate([c, c_ctx[None], jnp.zeros((8 - b - 1, d), F32)], 0)
    mods = _modulation(cc, w_mod, b_mod)
    cos, sin = _rope_tables(t_lat, n_ctx)
    hsum = jnp.kron(jnp.eye(RWKV_H, dtype=F32), jnp.ones((RWKV_N, RWKV_N), F32))
    hmean = hsum / RWKV_N
    lane = jnp.arange(128, dtype=jnp.int32) // (128 // (b * RWKV_H))
    pats = jnp.broadcast_to((jnp.arange(8, dtype=jnp.int32) * (b * RWKV_H))[:, None, None] + lane, (8, RWKV_N, 128))

    exp_w = [w.astype(BF16).reshape((depth * N_EXPERTS,) + w.shape[2:]) for w in (exp_w_gate, exp_w_up, exp_w_down)]
    xa = jnp.concatenate([x, ctx], axis=1)
    for l in range(depth):
        m_lat = mods[l, :b].reshape(b, 1, 6, d)
        m_ctx = jnp.broadcast_to(mods[l, b].reshape(1, 1, 6, d), (b, 1, 6, d))
        modv = jnp.concatenate([m_lat, m_ctx], axis=1)
        w_ext, wq, wqr, wk, wv = _layer_weights(l, w_in, mla_w_uq, mla_w_ukv)
        mu = rwkv_mu[l].reshape(1, N_RWKV_IN)
        mla = (cos, sin, mla_q_norm[l][None], mla_kv_norm[l][None], wq, wqr, wk, wv)
        zm, zg, zr, q, k, vt = _in_proj(xa, modv, w_ext, mlstm_conv[l], mu, mla, t_lat=t_lat, tm=tm)

        gates = zg[:, :, :4 * MLSTM_H].reshape(b, ta, 2, 8)
        g_col = gates.transpose(0, 2, 1, 3)
        g_row = gates.transpose(0, 2, 3, 1)
        gb = mlstm_gate_bias[l].reshape(2, 8)
        h_m = _mlstm(zm, g_col, g_row, gb.reshape(2, 1, 8), gb.reshape(2, 8, 1), t_lat=t_lat, cl=cl)

        zeros = jnp.zeros((DECAY_LORA, RWKV_W), F32)
        wup = jnp.concatenate([jnp.concatenate([rwkv_w_up[l, 0], zeros], 1),
                               jnp.concatenate([zeros, rwkv_w_up[l, 1]], 1)], 0).astype(BF16)
        aup = jnp.concatenate([jnp.concatenate([rwkv_a_up[l, 0], zeros], 1),
                               jnp.concatenate([zeros, rwkv_a_up[l, 1]], 1)], 0).astype(BF16)
        vec = jnp.concatenate([rwkv_w0[l], rwkv_a0[l], rwkv_k_k[l][None], rwkv_k_a[l][None], rwkv_r_k[l][None],
                               jnp.zeros((1, RWKV_W), F32)], 0)
        pd, v_tok, g_tok, bonus = _rwkv_prep(zr, wup, aup, rwkv_g_up[l].astype(BF16), vec, hsum, tm=128)
        rep = 128 // (b * RWKV_H)
        nvh = RWKV_N // rep
        vs = v_tok.reshape(b, ta, RWKV_H, nvh, rep).transpose(1, 3, 0, 2, 4).reshape(ta, nvh, 128)
        o_f, o_b = _rwkv_scan(pd, pats, vs, t_lat=t_lat, ts=32)
        o_s = jnp.stack([o_f, o_b], 0).reshape(2, ta, nvh, b, RWKV_H, rep)
        o_s = o_s.transpose(0, 3, 1, 4, 2, 5).reshape(2, b, ta, RWKV_W)

        att_l = _attention(q, k, vt, q_off=0, n_q=t_lat, k_off=0, n_keys=ta, tq=tm)
        att_c = _attention(q, k, vt, q_off=t_lat, n_q=n_ctx, k_off=t_lat, n_keys=n_ctx, tq=tm)

        vec_o = jnp.concatenate([mlstm_norm_w[l][None], rwkv_ln_w[l][None], rwkv_ln_b[l][None],
                                 jnp.zeros((5, RWKV_W), F32)], 0)
        ln1 = jnp.stack([ln1_w[l], ln1_b[l]], 0)
        xa = _out_proj(xa, h_m, zm, o_s, g_tok, bonus, att_l, att_c, modv, vec_o, hmean, w_out[l].astype(BF16), ln1,
                       t_lat=t_lat, tm=tm, alpha=alpha)

        ids, gts, h_rows = _router(xa, modv, router_w[l].T, router_bias[l].reshape(N_EXPERTS, 1), t_lat=t_lat, tm=tm)
        tok, dst, g_sorted, tables = _moe_tables(ids, gts, ta=ta, nv=nv_moe, nchunk=d // 128)
        routed = _moe_routed(h_rows, tok, dst, g_sorted, tables, *exp_w, nv=nv_moe, layer=l)
        ln2 = jnp.stack([ln2_w[l], ln2_b[l]], 0)
        xa = _moe_finish(xa, routed, modv, sh_w_gate[l].astype(BF16), sh_w_up[l].astype(BF16),
                         sh_w_down[l].astype(BF16), ln2, t_lat=t_lat, tm=tm, alpha=alpha,
                         n_rows=t_lat if l == depth - 1 else ta)
    return xa
```

```python
import functools

import jax
import jax.numpy as jnp
import numpy as np
from jax import lax
from jax.experimental import pallas as pl
from jax.experimental.pallas import tpu as pltpu

F32 = jnp.float32
BF16 = jnp.bfloat16
HIGHEST = lax.Precision.HIGHEST

GRID_W = 64
MLSTM_H, MLSTM_DH = 4, 64
MLSTM_W = MLSTM_H * MLSTM_DH
RWKV_H, RWKV_N = 4, 64
RWKV_W = RWKV_H * RWKV_N
DECAY_LORA, AAA_LORA, GATE_LORA = 64, 64, 128
RWKV_GN_EPS = 64e-5
MLA_H, MLA_NOPE, MLA_ROPE, MLA_V = 8, 64, 32, 64
Q_RANK, KV_RANK = 256, 128
ROPE_THETA = 10000.0
N_MLSTM_IN = 4 * MLSTM_W + 4 * MLSTM_H
N_RWKV_IN = 3 * RWKV_W + 2 * DECAY_LORA + 2 * AAA_LORA + GATE_LORA
N_EXPERTS, TOP_K, N_GROUPS, TOPK_GROUPS = 64, 8, 8, 4
ROUTED_SCALE = 2.5
LN_EPS = 1e-6
NEG = -1e30

C_QK, C_R, C_V, C_O, C_G, C_A = 0, 512, 1664, 1920, 2176, 2304
N_SHIFT = 1664
N_A = 640
N_EXT = C_A + N_A
HALO = 8
VMEM_LIMIT = 56 * 1024 * 1024


def _cp(sem):
    return pltpu.CompilerParams(dimension_semantics=sem, vmem_limit_bytes=VMEM_LIMIT)


def _ln(x):
    mu = jnp.mean(x, axis=-1, keepdims=True)
    xc = x - mu
    return xc * lax.rsqrt(jnp.mean(xc * xc, axis=-1, keepdims=True) + LN_EPS)


def _sigmoid(x):
    return 1.0 / (1.0 + jnp.exp(-x))


def _silu(x):
    return x * _sigmoid(x)


def _log_sigmoid(x):
    return jnp.minimum(x, 0.0) - jnp.log(1.0 + jnp.exp(-jnp.abs(x)))


def _softplus(x):
    return jnp.maximum(x, 0.0) + jnp.log(1.0 + jnp.exp(-jnp.abs(x)))


def _nt_dot(a, b, **kw):
    return lax.dot_general(a, b, (((1,), (1,)), ((), ())), preferred_element_type=F32, **kw)


def _tn_dot(a, b, **kw):
    return lax.dot_general(a, b, (((0,), (0,)), ((), ())), preferred_element_type=F32, **kw)


def _dot(a, b, **kw):
    return jnp.dot(a, b, preferred_element_type=F32, **kw)


def _split3(a):
    hi = a.astype(BF16)
    r = a - hi.astype(F32)
    mid = r.astype(BF16)
    return hi, mid, (r - mid.astype(F32)).astype(BF16)


def _dot_f32_by_exact(a, b_exact):
    b = b_exact.astype(BF16)
    hi, mid, lo = _split3(a)
    return _dot(hi, b) + _dot(mid, b) + _dot(lo, b)


def _mod_kernel(c_ref, w_ref, b_ref, o_ref):
    o_ref[0] = _dot(_silu(c_ref[...]), w_ref[0], precision=HIGHEST) + b_ref[0]


def _modulation(cc, w_mod, b_mod):
    n_layers, d, n = w_mod.shape
    tn = 1536
    return pl.pallas_call(
        _mod_kernel,
        grid=(n_layers, n // tn),
        in_specs=[pl.BlockSpec((8, d), lambda l, j: (0, 0)),
                  pl.BlockSpec((1, d, tn), lambda l, j: (l, 0, j)),
                  pl.BlockSpec((1, 1, tn), lambda l, j: (l, 0, j))],
        out_specs=pl.BlockSpec((1, 8, tn), lambda l, j: (l, 0, j)),
        out_shape=jax.ShapeDtypeStruct((n_layers, 8, n), F32),
        compiler_params=_cp(("parallel", "parallel")),
        name="modulation",
    )(cc, w_mod, b_mod.reshape(n_layers, 1, n))


def _in_kernel(xp_ref, x_ref, xn_ref, mod_ref, w_ref, conv_ref, mu_ref, cos_ref, sin_ref, nq_ref, nkv_ref,
               wq_ref, wqr_ref, wk_ref, wv_ref, zm_ref, zg_ref, zr_ref, q_ref, k_ref, vt_ref, *, tm, nt_lat, nt):
    i = pl.program_id(1)
    has_prev = jnp.logical_and(i != 0, i != nt_lat)
    has_next = jnp.logical_and(i != nt_lat - 1, i != nt - 1)
    xt = jnp.concatenate([xp_ref[0], x_ref[0], xn_ref[0]], axis=0)
    sh = mod_ref[0, 0, 0:1, :]
    sc = mod_ref[0, 0, 1:2, :]
    h = (_ln(xt) * (1.0 + sc) + sh).astype(BF16)
    z = _dot(h, w_ref[...])
    rows = lax.broadcasted_iota(jnp.int32, (tm + 2 * HALO, 1), 0)
    lo = jnp.where(has_prev, 0, HALO)
    hi = jnp.where(has_next, tm + 2 * HALO, tm + HALO)
    keep = jnp.logical_and(rows >= lo, rows < hi)
    zs = jnp.where(keep, z[:, :N_SHIFT], 0.0)
    zc = zs[HALO:HALO + tm]
    zprev = pltpu.roll(zs, 1, axis=0)[HALO:HALO + tm]
    znext = pltpu.roll(zs, tm + 2 * HALO - 1, axis=0)[HALO:HALO + tm]
    cw = conv_ref[...]
    qk = (cw[0:1] * zprev[:, :C_R] + cw[1:2] * zc[:, :C_R] + cw[2:3] * znext[:, :C_R])
    qk = _silu(qk)
    lane = lax.broadcasted_iota(jnp.int32, (1, C_R), 1)
    qk = qk * jnp.where(lane < MLSTM_W, MLSTM_DH ** -0.5, 1.0)
    zm_ref[0, :, 0:512] = qk
    zm_ref[0, :, 512:1024] = z[HALO:HALO + tm, C_V:C_G]
    zg_ref[0] = z[HALO:HALO + tm, C_G:C_A]
    zr = zc[:, C_R:]
    zr_ref[0] = zr + mu_ref[...] * (0.5 * (zprev[:, C_R:] + znext[:, C_R:]) - zr)
    _mla_project(z[HALO:HALO + tm, C_A:], cos_ref, sin_ref, nq_ref, nkv_ref, wq_ref, wqr_ref, wk_ref, wv_ref,
                 q_ref, k_ref, vt_ref)


def _in_proj(xa, modv, w_ext, conv, mu, mla, *, t_lat, tm):
    b, ta, d = xa.shape
    full = lambda a: pl.BlockSpec(a.shape, lambda bi, i: (0,) * a.ndim)
    rope = pl.BlockSpec((tm, 128), lambda bi, i: (i, 0))
    nt, nt_lat = ta // tm, t_lat // tm
    nh = tm // HALO
    last = ta // HALO - 1
    kern = functools.partial(_in_kernel, tm=tm, nt_lat=nt_lat, nt=nt)
    return pl.pallas_call(
        kern,
        grid=(b, nt),
        in_specs=[pl.BlockSpec((1, HALO, d), lambda bi, i: (bi, jnp.maximum(i * nh - 1, 0), 0)),
                  pl.BlockSpec((1, tm, d), lambda bi, i: (bi, i, 0)),
                  pl.BlockSpec((1, HALO, d), lambda bi, i: (bi, jnp.minimum((i + 1) * nh, last), 0)),
                  pl.BlockSpec((1, 1, 6, d), lambda bi, i: (bi, i // nt_lat, 0, 0)),
                  pl.BlockSpec((d, N_EXT), lambda bi, i: (0, 0)),
                  pl.BlockSpec((3, C_R), lambda bi, i: (0, 0)),
                  pl.BlockSpec((1, N_RWKV_IN), lambda bi, i: (0, 0)),
                  rope, rope] + [full(a) for a in mla[2:]],
        out_specs=[pl.BlockSpec((1, tm, 1024), lambda bi, i: (bi, i, 0)),
                   pl.BlockSpec((1, tm, 128), lambda bi, i: (bi, i, 0)),
                   pl.BlockSpec((1, tm, N_RWKV_IN), lambda bi, i: (bi, i, 0)),
                   pl.BlockSpec((1, tm, 1024), lambda bi, i: (bi, i, 0)),
                   pl.BlockSpec((1, tm, 1024), lambda bi, i: (bi, i, 0)),
                   pl.BlockSpec((1, 512, tm), lambda bi, i: (bi, 0, i))],
        out_shape=[jax.ShapeDtypeStruct((b, ta, 1024), F32),
                   jax.ShapeDtypeStruct((b, ta, 128), F32),
                   jax.ShapeDtypeStruct((b, ta, N_RWKV_IN), F32),
                   jax.ShapeDtypeStruct((b, ta, 1024), BF16),
                   jax.ShapeDtypeStruct((b, ta, 1024), BF16),
                   jax.ShapeDtypeStruct((b, 512, ta), BF16)],
        compiler_params=_cp(("parallel", "parallel")),
        name="in_proj",
    )(xa, xa, xa, modv, w_ext, conv, mu, *mla)


def _mlstm_kernel(zm_ref, gc_ref, gr_ref, bc_ref, br_ref, o_ref, ct_ref, n_ref, m_ref, *, cl):
    d = pl.program_id(1)

    @pl.when(pl.program_id(2) == 0)
    def _():
        ct_ref[...] = jnp.zeros_like(ct_ref)
        n_ref[...] = jnp.zeros_like(n_ref)
        m_ref[...] = jnp.zeros_like(m_ref)

    sgn = 1 - 2 * d
    gc = gc_ref[0, 0] + bc_ref[0]
    gr = gr_ref[0, 0] + br_ref[0]
    lf_c = _log_sigmoid(gc)
    lf_r = _log_sigmoid(gr)
    ii = lax.broadcasted_iota(jnp.int32, (cl, cl), 0)
    jj = lax.broadcasted_iota(jnp.int32, (cl, cl), 1)
    mask = (jj - ii) * sgn <= 0
    mask_t = (ii - jj) * sgn <= 0
    cum_c = _dot(mask.astype(F32), lf_c, precision=HIGHEST)
    cum_r = _dot(lf_r, mask_t.astype(F32), precision=HIGHEST)
    outs = []
    for h in range(MLSTM_H):
        sl = slice(h * MLSTM_DH, (h + 1) * MLSTM_DH)
        q = zm_ref[0, :, sl]
        k = zm_ref[0, :, MLSTM_W + h * MLSTM_DH:MLSTM_W + (h + 1) * MLSTM_DH]
        v = zm_ref[0, :, 2 * MLSTM_W + h * MLSTM_DH:2 * MLSTM_W + (h + 1) * MLSTM_DH]
        qb, kb = q.astype(BF16), k.astype(BF16)
        li_c, li_r = gc[:, h:h + 1], gr[h:h + 1, :]
        cu_c, cu_r = cum_c[:, 4 + h:5 + h], cum_r[4 + h:5 + h, :]
        b_end = jnp.sum(lf_r[4 + h:5 + h, :], axis=1, keepdims=True)
        m_prev = m_ref[h:h + 1, 0:1]
        ct = ct_ref[h]
        nv = n_ref[h:h + 1, 0:MLSTM_DH]
        d_log = jnp.where(mask, cu_c - cu_r + li_r, NEG)
        g_log = cu_c + m_prev
        m_row = jnp.maximum(g_log, jnp.max(d_log, axis=1, keepdims=True))
        w_intra = jnp.exp(d_log - m_row) * _nt_dot(qb, kb)
        e_inter = jnp.exp(g_log - m_row)
        num = _dot(w_intra.astype(BF16), v.astype(BF16)) + e_inter * _dot(qb, ct.astype(BF16))
        den = jnp.sum(w_intra, axis=1, keepdims=True) + e_inter * jnp.sum(q * nv, axis=1, keepdims=True)
        outs.append(num / jnp.maximum(jnp.abs(den), jnp.exp(-m_row)))
        w_end = b_end - cu_c + li_c
        m_loc = jnp.max(w_end, axis=0, keepdims=True)
        e_end = jnp.exp(w_end - m_loc)
        m_new = jnp.maximum(b_end + m_prev, m_loc)
        a_old = jnp.exp(b_end + m_prev - m_new)
        a_loc = jnp.exp(m_loc - m_new)
        ct_ref[h] = a_old * ct + a_loc * _tn_dot(kb, (v * e_end).astype(BF16))
        n_ref[h:h + 1, 0:MLSTM_DH] = a_old * nv + a_loc * jnp.sum(k * e_end, axis=0, keepdims=True)
        m_ref[h:h + 1, :] = jnp.broadcast_to(m_new, (1, 128))
    o_ref[0, 0] = jnp.concatenate(outs, axis=1)


def _mlstm(zm, g_col, g_row, b_col, b_row, *, t_lat, cl):
    b, ta, _ = zm.shape
    nc, nc_lat = ta // cl, t_lat // cl
    nc_ctx = nc - nc_lat

    def chunk(di, c):
        fwd_idx = jnp.where(c < nc_ctx, nc_lat + c, c - nc_ctx)
        return jnp.where(di == 0, fwd_idx, nc - 1 - c)

    return pl.pallas_call(
        functools.partial(_mlstm_kernel, cl=cl),
        grid=(b, 2, nc),
        in_specs=[pl.BlockSpec((1, cl, 1024), lambda bi, di, c: (bi, chunk(di, c), 0)),
                  pl.BlockSpec((1, 1, cl, 8), lambda bi, di, c: (bi, di, chunk(di, c), 0)),
                  pl.BlockSpec((1, 1, 8, cl), lambda bi, di, c: (bi, di, 0, chunk(di, c))),
                  pl.BlockSpec((1, 1, 8), lambda bi, di, c: (di, 0, 0)),
                  pl.BlockSpec((1, 8, 1), lambda bi, di, c: (di, 0, 0))],
        out_specs=pl.BlockSpec((1, 1, cl, MLSTM_W), lambda bi, di, c: (bi, di, chunk(di, c), 0)),
        out_shape=jax.ShapeDtypeStruct((b, 2, ta, MLSTM_W), F32),
        scratch_shapes=[pltpu.VMEM((MLSTM_H, MLSTM_DH, MLSTM_DH), F32),
                        pltpu.VMEM((8, 128), F32),
                        pltpu.VMEM((8, 128), F32)],
        compiler_params=_cp(("parallel", "parallel", "arbitrary")),
        name="mlstm",
    )(zm, g_col, g_row, b_col, b_row)


def _rwkv_prep_kernel(zr_ref, wup_ref, aup_ref, gup_ref, vec_ref, hsum_ref, pd_ref, v_ref, g_ref, bonus_ref):
    nb, tm = zr_ref.shape[0], zr_ref.shape[1]
    nbh = nb * RWKV_H
    vec = vec_ref[...]
    hsum = hsum_ref[...]
    zero = jnp.zeros((tm, RWKV_N), F32)
    for bi in range(nb):
        z = zr_ref[bi]
        zr_, zk, zv = z[:, 0:256], z[:, 256:512], z[:, 512:768]
        zw, za, zg = z[:, 768:896], z[:, 896:1024], z[:, 1024:1152]
        w_raw = _dot(jnp.tanh(zw).astype(BF16), wup_ref[...])
        a_raw = _dot(za.astype(BF16), aup_ref[...])
        g_ref[bi] = _dot(_sigmoid(zg).astype(BF16), gup_ref[...])
        kk = zk * vec[4:5]
        kk_ss = _dot_f32_by_exact(kk * kk, hsum)
        kk = kk * lax.rsqrt(jnp.maximum(kk_ss, 1e-12))
        k_sum = jnp.zeros_like(zk)
        for di in range(2):
            sl = slice(di * RWKV_W, (di + 1) * RWKV_W)
            decay = jnp.exp(-jnp.exp(-_softplus(-(vec[di:di + 1] + w_raw[:, sl])) - 0.5))
            a = _sigmoid(vec[2 + di:3 + di] + a_raw[:, sl])
            k_dir = zk * (1.0 + (a - 1.0) * vec[5:6])
            k_sum = k_sum + k_dir
            ka = kk * a
            for h in range(RWKV_H):
                hs = slice(h * RWKV_N, (h + 1) * RWKV_N)
                rows = pl.ds(bi * RWKV_H + h, tm, stride=nbh)
                pd_ref[di, 0, rows, :] = jnp.concatenate([decay[:, hs], ka[:, hs]], axis=1)
                pd_ref[di, 1, rows, :] = jnp.concatenate([k_dir[:, hs], kk[:, hs]], axis=1)
                pd_ref[di, 2, rows, :] = jnp.concatenate([zr_[:, hs], zero], axis=1)
        v_ref[bi] = zv
        bonus_ref[bi] = _dot_f32_by_exact(zr_ * vec[6:7] * k_sum, hsum) * zv


def _rwkv_prep(zr, wup, aup, gup, vec, hsum, *, tm):
    b, ta, _ = zr.shape
    nbh = b * RWKV_H
    tok = pl.BlockSpec((b, tm, RWKV_W), lambda i: (0, i, 0))
    full = lambda a: pl.BlockSpec(a.shape, lambda i: (0,) * a.ndim)
    return pl.pallas_call(
        _rwkv_prep_kernel,
        grid=(ta // tm,),
        in_specs=[pl.BlockSpec((b, tm, N_RWKV_IN), lambda i: (0, i, 0)),
                  full(wup), full(aup), full(gup), full(vec), full(hsum)],
        out_specs=[pl.BlockSpec((2, 3, tm * nbh, 128), lambda i: (0, 0, i, 0)), tok, tok, tok],
        out_shape=[jax.ShapeDtypeStruct((2, 3, ta * nbh, 128), F32)] + [jax.ShapeDtypeStruct((b, ta, RWKV_W), F32)] * 3,
        compiler_params=_cp(("parallel",)),
        name="rwkv_prep",
    )(zr, wup, aup, gup, vec, hsum)


def _rwkv_scan_kernel(pf_ref, pb_ref, pat_ref, vf_ref, vb_ref, of_ref, ob_ref, st_ref, x_ref, tt_ref, *, ts, nvh, nbh):
    ng = ts // 8
    rows = 8 * nbh

    @pl.when(pl.program_id(0) == 0)
    def _():
        st_ref[...] = jnp.zeros_like(st_ref)

    def transpose(di, p_ref, grp):
        sl = pl.ds(pl.multiple_of(grp * rows, rows), rows)
        for j in range(3):
            tt_ref[di, j] = p_ref[0, j, sl, :].T

    def update(di, u, v_ref, o_ref, t):
        pat = pat_ref[u]
        for j, (tj, half) in enumerate(((0, 0), (0, 1), (1, 0), (1, 1), (2, 0))):
            x_ref[di,j] = jnp.take_along_axis(tt_ref[di, tj, half * RWKV_N:(half + 1) * RWKV_N, :], pat, axis=1)
        vt = v_ref[t]
        nkb = RWKV_N // 8
        kblk = lambda kb: pl.ds(kb * 8, 8)
        acc = [None] * nvh
        for kb in range(nkb):
            kk = x_ref[di,3, kblk(kb), :]
            for vh in range(nvh):
                term = st_ref[di, vh, kblk(kb), :] * kk
                acc[vh] = term if kb == 0 else acc[vh] + term
        sa = [jnp.sum(a, axis=0, keepdims=True) for a in acc]
        out = [None] * nvh
        for kb in range(nkb):
            w, ka = x_ref[di,0, kblk(kb), :], x_ref[di,1, kblk(kb), :]
            k, r = x_ref[di,2, kblk(kb), :], x_ref[di,4, kblk(kb), :]
            for vh in range(nvh):
                sv = st_ref[di, vh, kblk(kb), :] * w - sa[vh] * ka + vt[vh:vh + 1, :] * k
                st_ref[di, vh, kblk(kb), :] = sv
                out[vh] = sv * r if kb == 0 else out[vh] + sv * r
        o_ref[t] = jnp.concatenate([jnp.sum(o, axis=0, keepdims=True) for o in out], axis=0)

    def group(g, carry):
        transpose(0, pf_ref, g)
        transpose(1, pb_ref, ng - 1 - g)
        for u in range(8):
            update(0, u, vf_ref, of_ref, g * 8 + u)
            update(1, 7 - u, vb_ref, ob_ref, ts - 1 - (g * 8 + u))
        return carry

    lax.fori_loop(0, ng, group, 0)


def _rwkv_scan(pd, pats, vs, *, t_lat, ts):
    ta, nvh, nl = vs.shape
    nbh = pd.shape[2] // ta
    nt, nt_lat = ta // ts, t_lat // ts
    fwd = lambda i: (i + nt_lat) % nt
    bwd = lambda i: nt - 1 - i
    return pl.pallas_call(
        functools.partial(_rwkv_scan_kernel, ts=ts, nvh=nvh, nbh=nbh),
        grid=(nt,),
        in_specs=[pl.BlockSpec((1, 3, ts * nbh, 128), lambda i: (0, 0, fwd(i), 0)),
                  pl.BlockSpec((1, 3, ts * nbh, 128), lambda i: (1, 0, bwd(i), 0)),
                  pl.BlockSpec(pats.shape, lambda i: (0, 0, 0)),
                  pl.BlockSpec((ts, nvh, nl), lambda i: (fwd(i), 0, 0)),
                  pl.BlockSpec((ts, nvh, nl), lambda i: (bwd(i), 0, 0))],
        out_specs=[pl.BlockSpec((ts, nvh, nl), lambda i: (fwd(i), 0, 0)),
                   pl.BlockSpec((ts, nvh, nl), lambda i: (bwd(i), 0, 0))],
        out_shape=[jax.ShapeDtypeStruct((ta, nvh, nl), F32)] * 2,
        scratch_shapes=[pltpu.VMEM((2, nvh, RWKV_N, nl), F32), pltpu.VMEM((2, 5, RWKV_N, nl), F32),
                        pltpu.VMEM((2, 3, 128, 8 * nbh), F32)],
        compiler_params=_cp(("arbitrary",)),
        name="rwkv_scan",
    )(pd, pd, pats, vs, vs)


def _mla_project(za, cos_ref, sin_ref, nq_ref, nkv_ref, wq_ref, wqr_ref, wk_ref, wv_ref, q_ref, k_ref, v_ref):
    cq, ckv = za[:, 0:Q_RANK], za[:, Q_RANK:Q_RANK + KV_RANK]
    kr, krr = za[:, 384:512], za[:, 512:640]
    cos, sin = cos_ref[...], sin_ref[...]
    cqn = (cq * lax.rsqrt(jnp.mean(cq * cq, axis=-1, keepdims=True) + 1e-6) * nq_ref[...]).astype(BF16)
    ckvn = (ckv * lax.rsqrt(jnp.mean(ckv * ckv, axis=-1, keepdims=True) + 1e-6) * nkv_ref[...]).astype(BF16)
    q = _dot(cqn, wq_ref[...])
    qr = _dot(cqn, wqr_ref[...])
    kn = _dot(ckvn, wk_ref[...])
    v_ref[0] = _nt_dot(wv_ref[...], ckvn).astype(BF16)
    k_rope = kr * cos + krr * sin
    scale = (MLA_NOPE + MLA_ROPE) ** -0.5 * np.log2(np.e)
    for h in range(MLA_H):
        sl = slice(h * 128, (h + 1) * 128)
        q_ref[0, :, sl] = ((q[:, sl] * cos + qr[:, sl] * sin) * scale).astype(BF16)
        k_ref[0, :, sl] = (kn[:, sl] + k_rope).astype(BF16)


def _attn_kernel(q_ref, k_ref, vt_ref, o_ref):
    outs = []
    for h in range(ATTN_HEADS):
        sl = slice(h * 128, (h + 1) * 128)
        pair = h // 2
        st = _nt_dot(k_ref[0, :, sl], q_ref[0, :, sl])
        m = jnp.max(st, axis=0, keepdims=True)
        p = jnp.exp2(st - m)
        l = jnp.sum(p, axis=0, keepdims=True)
        ot = _dot(vt_ref[0, pair * 128:(pair + 1) * 128, :], p.astype(BF16))
        outs.append(ot[(h % 2) * MLA_V:(h % 2 + 1) * MLA_V] / l)
    o_ref[0] = jnp.concatenate(outs, axis=0).T


ATTN_HEADS = 4


def _attention(q, k, vt, *, q_off, n_q, k_off, n_keys, tq):
    b = q.shape[0]
    qo, ko = q_off // tq, k_off // n_keys
    hq, hv = ATTN_HEADS * 128, ATTN_HEADS * MLA_V
    return pl.pallas_call(
        _attn_kernel,
        grid=(b, MLA_H // ATTN_HEADS, n_q // tq),
        in_specs=[pl.BlockSpec((1, tq, hq), lambda bi, hp, i: (bi, qo + i, hp)),
                  pl.BlockSpec((1, n_keys, hq), lambda bi, hp, i: (bi, ko, hp)),
                  pl.BlockSpec((1, hv, n_keys), lambda bi, hp, i: (bi, hp, ko))],
        out_specs=pl.BlockSpec((1, tq, hv), lambda bi, hp, i: (bi, i, hp)),
        out_shape=jax.ShapeDtypeStruct((b, n_q, 512), F32),
        compiler_params=_cp(("parallel", "parallel", "parallel")),
        name="mla_attention",
    )(q, k, vt)


def _out_kernel(x_ref, hf_ref, hb_ref, zm_ref, of_ref, ob_ref, g_ref, bonus_ref, attl_ref, attc_ref, mod_ref, vec_ref,
                hmean_ref, wo_ref, ln_ref, o_ref, *, alpha, nt_lat):
    hmean = hmean_ref[...]
    att = jnp.where(pl.program_id(1) < nt_lat, attl_ref[0], attc_ref[0])

    def head_norm(y, eps):
        mu = _dot_f32_by_exact(y, hmean)
        yc = y - mu
        return yc * lax.rsqrt(_dot_f32_by_exact(yc * yc, hmean) + eps)

    vec = vec_ref[...]
    zo = zm_ref[0]
    m_mix = _sigmoid(zo) * (head_norm(hf_ref[0, 0] + hb_ref[0, 0], 1e-6) * vec[0:1])
    r_o = head_norm(of_ref[0, 0] + ob_ref[0, 0], RWKV_GN_EPS) * vec[1:2] + vec[2:3] + bonus_ref[0]
    r_mix = r_o * g_ref[0]
    mix = jnp.concatenate([m_mix, r_mix, att], axis=1).astype(BF16)
    y = _dot(mix, wo_ref[...])
    gate = mod_ref[0, 0, 2:3, :]
    o_ref[0] = _ln(alpha * x_ref[0] + gate * y) * ln_ref[0:1] + ln_ref[1:2]


def _out_proj(xa, h_m, zm, o_r, g, bonus, att_l, att_c, modv, vec, hmean, wo, ln, *, t_lat, tm, alpha):
    b, ta, d = xa.shape
    nt_lat = t_lat // tm
    tok = lambda w: pl.BlockSpec((1, tm, w), lambda bi, i: (bi, i, 0))
    full = lambda a: pl.BlockSpec(a.shape, lambda bi, i: (0,) * a.ndim)
    return pl.pallas_call(
        functools.partial(_out_kernel, alpha=alpha, nt_lat=nt_lat),
        grid=(b, ta // tm),
        in_specs=[tok(d),
                  pl.BlockSpec((1, 1, tm, MLSTM_W), lambda bi, i: (bi, 0, i, 0)),
                  pl.BlockSpec((1, 1, tm, MLSTM_W), lambda bi, i: (bi, 1, i, 0)),
                  pl.BlockSpec((1, tm, MLSTM_W), lambda bi, i: (bi, i, 3)),
                  pl.BlockSpec((1, 1, tm, RWKV_W), lambda bi, i: (0, bi, i, 0)),
                  pl.BlockSpec((1, 1, tm, RWKV_W), lambda bi, i: (1, bi, i, 0)),
                  tok(RWKV_W), tok(RWKV_W),
                  pl.BlockSpec((1, tm, 512), lambda bi, i: (bi, jnp.minimum(i, nt_lat - 1), 0)),
                  pl.BlockSpec((1, tm, 512), lambda bi, i: (bi, jnp.maximum(i - nt_lat, 0), 0)),
                  pl.BlockSpec((1, 1, 6, d), lambda bi, i: (bi, i // nt_lat, 0, 0)),
                  full(vec), full(hmean), full(wo), full(ln)],
        out_specs=tok(d),
        out_shape=jax.ShapeDtypeStruct((b, ta, d), F32),
        compiler_params=_cp(("parallel", "parallel")),
        name="out_proj",
    )(xa, h_m, h_m, zm, o_r, o_r, g, bonus, att_l, att_c, modv, vec, hmean, wo, ln)


def _first_max(x, idx, axis):
    mx = jnp.max(x, axis=axis, keepdims=True)
    first = jnp.min(jnp.where(x == mx, idx, 1 << 20), axis=axis, keepdims=True)
    return mx, first, idx == first


def _router_kernel(x_ref, mod_ref, w_ref, bias_ref, id_ref, gate_ref, h_ref):
    tm = x_ref.shape[1]
    sh, sc = mod_ref[0, 0, 3:4, :], mod_ref[0, 0, 4:5, :]
    h = _ln(x_ref[0]) * (1.0 + sc) + sh
    for j in range(h.shape[1] // 128):
        h_ref[0, pl.ds(j, tm, stride=h.shape[1] // 128), :] = h[:, j * 128:(j + 1) * 128]
    scores = _sigmoid(_nt_dot(w_ref[...], h, precision=HIGHEST))
    biased = scores + bias_ref[...]
    gsz = N_EXPERTS // N_GROUPS
    b3 = biased.reshape(N_GROUPS, gsz, tm)
    s3 = scores.reshape(N_GROUPS, gsz, tm)
    e_idx = lax.broadcasted_iota(jnp.int32, (N_GROUPS, gsz, tm), 1)
    m1, _, hit = _first_max(b3, e_idx, 1)
    m2 = jnp.max(jnp.where(hit, -jnp.inf, b3), axis=1, keepdims=True)
    gscore = m1 + m2
    g_idx = lax.broadcasted_iota(jnp.int32, (N_GROUPS, 1, tm), 0)
    gsel = jnp.zeros((N_GROUPS, 1, tm), F32)
    for _ in range(TOPK_GROUPS):
        _, _, hit = _first_max(jnp.where(gsel > 0, -jnp.inf, gscore), g_idx, 0)
        gsel = jnp.where(hit, 1.0, gsel)
    cand = jnp.where(jnp.broadcast_to(gsel, b3.shape) > 0, b3, -jnp.inf)
    x_idx = g_idx * gsz + e_idx
    sel = jnp.zeros(b3.shape, F32)
    ids, picked = [], []
    for _ in range(TOP_K):
        _, first, hit = _first_max(jnp.where(sel > 0, -jnp.inf, cand), x_idx, (0, 1))
        sel = jnp.where(hit, 1.0, sel)
        ids.append(first[0])
        picked.append(jnp.sum(jnp.where(hit, s3, 0.0), axis=(0, 1), keepdims=True)[0])
    picked = jnp.concatenate(picked, axis=0)
    id_ref[0] = jnp.concatenate(ids, axis=0)
    gate_ref[0] = ROUTED_SCALE * picked / jnp.sum(picked, axis=0, keepdims=True)


def _router(xa, modv, w_t, bias, *, t_lat, tm):
    b, ta, d = xa.shape
    nt_lat = t_lat // tm
    return pl.pallas_call(
        _router_kernel,
        grid=(b, ta // tm),
        in_specs=[pl.BlockSpec((1, tm, d), lambda bi, i: (bi, i, 0)),
                  pl.BlockSpec((1, 1, 6, d), lambda bi, i: (bi, i // nt_lat, 0, 0)),
                  pl.BlockSpec((N_EXPERTS, d), lambda bi, i: (0, 0)),
                  pl.BlockSpec((N_EXPERTS, 1), lambda bi, i: (0, 0))],
        out_specs=[pl.BlockSpec((1, TOP_K, tm), lambda bi, i: (bi, 0, i)),
                   pl.BlockSpec((1, TOP_K, tm), lambda bi, i: (bi, 0, i)),
                   pl.BlockSpec((1, tm * (d // 128), 128), lambda bi, i: (bi, i, 0))],
        out_shape=[jax.ShapeDtypeStruct((b, TOP_K, ta), jnp.int32),
                   jax.ShapeDtypeStruct((b, TOP_K, ta), F32),
                   jax.ShapeDtypeStruct((b, ta * (d // 128), 128), F32)],
        compiler_params=_cp(("parallel", "parallel")),
        name="router",
    )(xa, modv, w_t, bias)


MOE_RB = 256
MOE_PITCH = MOE_RB + 8
MOE_U = 8


def _moe_routed_kernel(vblk_ref, vexp_ref, vlo_ref, vhi_ref, vvalid_ref, vfirst_ref,
                       tok_ref, dst_ref, gate_ref, src_ref, wg_ref, wu_ref, wd_ref, acc_ref, xt_ref, yt_ref, xb_ref,
                       gc_ref, *, nv, nchunk):
    bi, v = pl.program_id(0), pl.program_id(1)
    pos = bi * nv + v
    lo, hi = vlo_ref[pos], vhi_ref[pos]

    @pl.when(v == 0)
    def _():
        acc_ref[...] = jnp.zeros_like(acc_ref)

    @pl.when(vfirst_ref[pos] == 1)
    def _():
        for r in range(MOE_RB):
            off = pl.multiple_of(tok_ref[0, 0, 0, r], nchunk)
            xt_ref[pl.ds(r, nchunk, stride=MOE_PITCH), :] = src_ref[0, pl.ds(off, nchunk), :]
        for j in range(nchunk):
            xb_ref[:, j * 128:(j + 1) * 128] = xt_ref[j * MOE_PITCH:j * MOE_PITCH + MOE_RB, :].astype(BF16)
        ii = lax.broadcasted_iota(jnp.int32, (MOE_RB, MOE_RB), 0)
        jj = lax.broadcasted_iota(jnp.int32, (MOE_RB, MOE_RB), 1)
        gc_ref[...] = jnp.sum(jnp.where(ii == jj, gate_ref[0, 0], 0.0), axis=1, keepdims=True)

    @pl.when(vvalid_ref[pos] == 1)
    def _():
        x = xb_ref[...]
        row = lax.broadcasted_iota(jnp.int32, (MOE_RB, 1), 0)
        gcol = jnp.where(jnp.logical_and(row >= lo, row < hi), gc_ref[...], 0.0)
        a = _silu(_dot(x, wg_ref[0])) * _dot(x, wu_ref[0]) * gcol
        y = _dot(a.astype(BF16), wd_ref[0])
        for j in range(nchunk):
            yt_ref[j * MOE_PITCH:j * MOE_PITCH + MOE_RB, :] = y[:, j * 128:(j + 1) * 128]
        for g in range(MOE_RB // MOE_U):
            pending = []
            for u in range(MOE_U):
                r = g * MOE_U + u
                off = pl.multiple_of(dst_ref[0, 0, 0, r], nchunk)
                pending.append((off, acc_ref[0, pl.ds(off, nchunk), :] + yt_ref[pl.ds(r, nchunk, stride=MOE_PITCH), :]))
            for off, val in pending:
                acc_ref[0, pl.ds(off, nchunk), :] = val


def _moe_routed(h_rows, tok, dst, gate, tables, wg, wu, wd, *, nv, layer):
    b, rows, _ = h_rows.shape
    d, de = wg.shape[1], wg.shape[2]
    nchunk = d // 128
    nblk = tok.shape[1] // MOE_RB
    blk = lambda bi, v, vblk, *_: (bi, vblk[bi * nv + v], 0, 0)
    exp = lambda bi, v, vblk, vexp, *_: (layer * N_EXPERTS + vexp[bi * nv + v], 0, 0)
    grid_spec = pltpu.PrefetchScalarGridSpec(
        num_scalar_prefetch=6,
        grid=(b, nv),
        in_specs=[pl.BlockSpec((1, 1, 1, MOE_RB), blk, memory_space=pltpu.SMEM),
                  pl.BlockSpec((1, 1, 1, MOE_RB), lambda bi, v, *_: (bi, v, 0, 0), memory_space=pltpu.SMEM),
                  pl.BlockSpec((1, 1, 1, MOE_RB), blk),
                  pl.BlockSpec((1, rows, 128), lambda bi, v, *_: (bi, 0, 0), pipeline_mode=pl.Buffered(1)),
                  pl.BlockSpec((1, d, de), exp),
                  pl.BlockSpec((1, d, de), exp),
                  pl.BlockSpec((1, de, d), exp)],
        out_specs=pl.BlockSpec((1, rows + MOE_U * nchunk, 128), lambda bi, v, *_: (bi, 0, 0),
                               pipeline_mode=pl.Buffered(1)),
        scratch_shapes=[pltpu.VMEM((nchunk * MOE_PITCH, 128), F32), pltpu.VMEM((nchunk * MOE_PITCH, 128), F32),
                        pltpu.VMEM((MOE_RB, d), BF16), pltpu.VMEM((MOE_RB, 1), F32)],
    )
    return pl.pallas_call(
        functools.partial(_moe_routed_kernel, nv=nv, nchunk=nchunk),
        grid_spec=grid_spec,
        out_shape=jax.ShapeDtypeStruct((b, rows + MOE_U * nchunk, 128), F32),
        compiler_params=_cp(("parallel", "arbitrary")),
        name="moe_routed",
    )(*tables, tok.reshape(b, nblk, 1, MOE_RB), dst.reshape(b, nv, 1, MOE_RB), gate.reshape(b, nblk, 1, MOE_RB),
      h_rows, wg, wu, wd)


def _moe_tables(ids, gates, *, ta, nv, nchunk):
    b = ids.shape[0]
    n = TOP_K * ta
    nblk = n // MOE_RB
    t_idx = jnp.broadcast_to(jnp.arange(ta, dtype=jnp.int32), (b, TOP_K, ta))
    keys, g_sorted = lax.sort(((ids * ta + t_idx).reshape(b, n), gates.reshape(b, n)), dimension=1, num_keys=1)
    tok, e_sorted = keys % ta, keys // ta
    ends = jnp.sum(e_sorted[:, None, :] <= jnp.arange(N_EXPERTS, dtype=jnp.int32)[None, :, None], axis=2)
    starts = ends - jnp.sum(e_sorted[:, None, :] == jnp.arange(N_EXPERTS, dtype=jnp.int32)[None, :, None], axis=2)
    e_lo, e_hi = e_sorted[:, ::MOE_RB], e_sorted[:, MOE_RB - 1::MOE_RB]
    n_vis = e_hi - e_lo + 1
    v_end = jnp.cumsum(n_vis, axis=1)
    v_start = v_end - n_vis
    v = jnp.arange(nv, dtype=jnp.int32)
    blk = jnp.minimum(jnp.sum(v_end[:, None, :] <= v[None, :, None], axis=2), nblk - 1).astype(jnp.int32)
    take = lambda a, i: jnp.take_along_axis(a, i, axis=1)
    valid = v[None, :] < v_end[:, -1:]
    exp = jnp.where(valid, take(e_lo, blk) + v[None, :] - take(v_start, blk), take(e_hi, blk)).astype(jnp.int32)
    lo = jnp.clip(take(starts, exp) - blk * MOE_RB, 0, MOE_RB)
    hi = jnp.where(valid, jnp.clip(take(ends, exp) - blk * MOE_RB, 0, MOE_RB), 0)
    first = jnp.logical_and(valid, v[None, :] == take(v_start, blk))
    flat = lambda a: a.astype(jnp.int32).reshape(-1)
    r = jnp.arange(MOE_RB, dtype=jnp.int32)
    tok_v = jnp.take_along_axis(tok.reshape(b, nblk, MOE_RB), blk[:, :, None], axis=1)
    mine = jnp.logical_and(r >= lo[:, :, None], r < hi[:, :, None])
    dst = jnp.where(mine, tok_v, ta + r % MOE_U)
    rows = nchunk
    return ((tok * rows).astype(jnp.int32), (dst * rows).astype(jnp.int32), g_sorted,
            tuple(flat(a) for a in (blk, exp, lo, hi, valid, first)))


def _moe_finish_kernel(x_ref, r_ref, mod_ref, sg_ref, su_ref, sd_ref, ln_ref, o_ref, *, alpha):
    tm, d = x_ref.shape[1], x_ref.shape[2]
    nchunk = d // 128
    x = x_ref[0]
    h = (_ln(x) * (1.0 + mod_ref[0, 0, 4:5, :]) + mod_ref[0, 0, 3:4, :]).astype(BF16)
    shared = _dot((_silu(_dot(h, sg_ref[...])) * _dot(h, su_ref[...])).astype(BF16), sd_ref[...])
    routed = jnp.concatenate([r_ref[0, pl.ds(j, tm, stride=nchunk), :] for j in range(nchunk)], axis=1)
    o_ref[0] = _ln(alpha * x + mod_ref[0, 0, 5:6, :] * (routed + shared)) * ln_ref[0:1] + ln_ref[1:2]


def _moe_finish(xa, routed, modv, sg, su, sd, ln, *, t_lat, tm, alpha, n_rows):
    b, ta, d = xa.shape
    nt_lat = t_lat // tm
    full = lambda a: pl.BlockSpec(a.shape, lambda bi, i: (0,) * a.ndim)
    return pl.pallas_call(
        functools.partial(_moe_finish_kernel, alpha=alpha),
        grid=(b, n_rows // tm),
        in_specs=[pl.BlockSpec((1, tm, d), lambda bi, i: (bi, i, 0)),
                  pl.BlockSpec((1, tm * (d // 128), 128), lambda bi, i: (bi, i, 0)),
                  pl.BlockSpec((1, 1, 6, d), lambda bi, i: (bi, i // nt_lat, 0, 0)),
                  full(sg), full(su), full(sd), full(ln)],
        out_specs=pl.BlockSpec((1, tm, d), lambda bi, i: (bi, i, 0)),
        out_shape=jax.ShapeDtypeStruct((b, n_rows, d), F32),
        compiler_params=_cp(("parallel", "parallel")),
        name="moe_finish",
    )(xa, routed, modv, sg, su, sd, ln)


def _rope_blocks(w_rope):
    k = w_rope.shape[0]
    ev, od = w_rope[:, 0::2], w_rope[:, 1::2]
    z64, z32 = jnp.zeros((k, 64), w_rope.dtype), jnp.zeros((k, 32), w_rope.dtype)
    return (jnp.concatenate([z64, ev, od, z32], 1), jnp.concatenate([z64, -od, ev, z32], 1))


def _layer_weights(l, w_in, mla_w_uq, mla_w_ukv):
    w = w_in[l]
    d = w.shape[0]
    wm, wr, wa = w[:, :N_MLSTM_IN], w[:, N_MLSTM_IN:N_MLSTM_IN + N_RWKV_IN], w[:, N_MLSTM_IN + N_RWKV_IN:]
    kr, krr = _rope_blocks(wa[:, Q_RANK + KV_RANK:])
    w_ext = jnp.concatenate([
        wm[:, :2 * MLSTM_W], wr, wm[:, 2 * MLSTM_W:4 * MLSTM_W],
        wm[:, 4 * MLSTM_W:], jnp.zeros((d, 128 - 4 * MLSTM_H), w.dtype),
        wa[:, :Q_RANK + KV_RANK], kr, krr], axis=1).astype(BF16)
    uq = mla_w_uq[l].reshape(Q_RANK, MLA_H, MLA_NOPE + MLA_ROPE)
    nope, ev, od = uq[:, :, :MLA_NOPE], uq[:, :, MLA_NOPE::2], uq[:, :, MLA_NOPE + 1::2]
    z64, z32 = jnp.zeros((Q_RANK, MLA_H, 64), F32), jnp.zeros((Q_RANK, MLA_H, 32), F32)
    wq = jnp.concatenate([nope, ev, od, z32], axis=2).reshape(Q_RANK, MLA_H * 128)
    wqr = jnp.concatenate([z64, -od, ev, z32], axis=2).reshape(Q_RANK, MLA_H * 128)
    ukv = mla_w_ukv[l].reshape(KV_RANK, MLA_H, MLA_NOPE + MLA_V)
    wk = jnp.concatenate([ukv[:, :, :MLA_NOPE], jnp.zeros((KV_RANK, MLA_H, 64), F32)], axis=2).reshape(KV_RANK, MLA_H * 128)
    wv = ukv[:, :, MLA_NOPE:].reshape(KV_RANK, MLA_H * MLA_V)
    return w_ext, wq.astype(BF16), wqr.astype(BF16), wk.astype(BF16), wv.T.astype(BF16)


def _rope_tables(t_lat, n_ctx):
    rows = t_lat // GRID_W
    row = jnp.repeat(jnp.arange(rows), GRID_W).astype(F32)
    col = jnp.tile(jnp.arange(GRID_W), rows).astype(F32)
    n_freq = MLA_ROPE // 4
    freq = ROPE_THETA ** (-jnp.arange(n_freq, dtype=F32) / n_freq)
    ang = jnp.concatenate([row[:, None] * freq, col[:, None] * freq], -1)
    cos, sin = jnp.cos(ang), jnp.sin(ang)
    one, zero = jnp.ones((t_lat, 64), F32), jnp.zeros((t_lat, 32), F32)
    cos_l = jnp.concatenate([one, cos, cos, zero], 1)
    sin_l = jnp.concatenate([0 * one, sin, sin, zero], 1)
    cos_c = jnp.concatenate([jnp.ones((n_ctx, 96), F32), jnp.zeros((n_ctx, 32), F32)], 1)
    return jnp.concatenate([cos_l, cos_c], 0), jnp.concatenate([sin_l, jnp.zeros((n_ctx, 128), F32)], 0)


def kernel(x, c, ctx, c_ctx, w_mod, b_mod, w_in, mlstm_conv, mlstm_gate_bias, mlstm_norm_w, rwkv_mu, rwkv_w0, rwkv_w_up, rwkv_a0, rwkv_a_up, rwkv_g_up, rwkv_k_k, rwkv_k_a, rwkv_r_k, rwkv_ln_w, rwkv_ln_b, mla_q_norm, mla_kv_norm, mla_w_uq, mla_w_ukv, w_out, ln1_w, ln1_b, router_w, router_bias, exp_w_gate, exp_w_up, exp_w_down, sh_w_gate, sh_w_up, sh_w_down, ln2_w, ln2_b):
    b, t_lat, d = x.shape
    n_ctx = ctx.shape[1]
    ta = t_lat + n_ctx
    depth = w_in.shape[0]
    alpha = (2 * depth) ** 0.25
    tm = 256
    cl = 256
    nv_moe = TOP_K * ta // MOE_RB + N_EXPERTS
    assert b + 1 <= 8 and n_ctx % tm == 0 and t_lat % tm == 0 and (TOP_K * ta) % MOE_RB == 0

    cc = jnp.concatenate([c, c_ctx[None], jnp.zeros((8 - b - 1, d), F32)], 0)
    mods = _modulation(cc, w_mod, b_mod)
    cos, sin = _rope_tables(t_lat, n_ctx)
    hsum = jnp.kron(jnp.eye(RWKV_H, dtype=F32), jnp.ones((RWKV_N, RWKV_N), F32))
    hmean = hsum / RWKV_N
    lane = jnp.arange(128, dtype=jnp.int32) // (128 // (b * RWKV_H))
    pats = jnp.broadcast_to((jnp.arange(8, dtype=jnp.int32) * (b * RWKV_H))[:, None, None] + lane, (8, RWKV_N, 128))

    exp_w = [w.astype(BF16).reshape((depth * N_EXPERTS,) + w.shape[2:]) for w in (exp_w_gate, exp_w_up, exp_w_down)]
    xa = jnp.concatenate([x, ctx], axis=1)
    for l in range(depth):
        m_lat = mods[l, :b].reshape(b, 1, 6, d)
        m_ctx = jnp.broadcast_to(mods[l, b].reshape(1, 1, 6, d), (b, 1, 6, d))
        modv = jnp.concatenate([m_lat, m_ctx], axis=1)
        w_ext, wq, wqr, wk, wv = _layer_weights(l, w_in, mla_w_uq, mla_w_ukv)
        mu = rwkv_mu[l].reshape(1, N_RWKV_IN)
        mla = (cos, sin, mla_q_norm[l][None], mla_kv_norm[l][None], wq, wqr, wk, wv)
        zm, zg, zr, q, k, vt = _in_proj(xa, modv, w_ext, mlstm_conv[l], mu, mla, t_lat=t_lat, tm=tm)

        gates = zg[:, :, :4 * MLSTM_H].reshape(b, ta, 2, 8)
        g_col = gates.transpose(0, 2, 1, 3)
        g_row = gates.transpose(0, 2, 3, 1)
        gb = mlstm_gate_bias[l].reshape(2, 8)
        h_m = _mlstm(zm, g_col, g_row, gb.reshape(2, 1, 8), gb.reshape(2, 8, 1), t_lat=t_lat, cl=cl)

        zeros = jnp.zeros((DECAY_LORA, RWKV_W), F32)
        wup = jnp.concatenate([jnp.concatenate([rwkv_w_up[l, 0], zeros], 1),
                               jnp.concatenate([zeros, rwkv_w_up[l, 1]], 1)], 0).astype(BF16)
        aup = jnp.concatenate([jnp.concatenate([rwkv_a_up[l, 0], zeros], 1),
                               jnp.concatenate([zeros, rwkv_a_up[l, 1]], 1)], 0).astype(BF16)
        vec = jnp.concatenate([rwkv_w0[l], rwkv_a0[l], rwkv_k_k[l][None], rwkv_k_a[l][None], rwkv_r_k[l][None],
                               jnp.zeros((1, RWKV_W), F32)], 0)
        pd, v_tok, g_tok, bonus = _rwkv_prep(zr, wup, aup, rwkv_g_up[l].astype(BF16), vec, hsum, tm=128)
        rep = 128 // (b * RWKV_H)
        nvh = RWKV_N // rep
        vs = v_tok.reshape(b, ta, RWKV_H, nvh, rep).transpose(1, 3, 0, 2, 4).reshape(ta, nvh, 128)
        o_f, o_b = _rwkv_scan(pd, pats, vs, t_lat=t_lat, ts=32)
        o_s = jnp.stack([o_f, o_b], 0).reshape(2, ta, nvh, b, RWKV_H, rep)
        o_s = o_s.transpose(0, 3, 1, 4, 2, 5).reshape(2, b, ta, RWKV_W)

        att_l = _attention(q, k, vt, q_off=0, n_q=t_lat, k_off=0, n_keys=ta, tq=tm)
        att_c = _attention(q, k, vt, q_off=t_lat, n_q=n_ctx, k_off=t_lat, n_keys=n_ctx, tq=tm)

        vec_o = jnp.concatenate([mlstm_norm_w[l][None], rwkv_ln_w[l][None], rwkv_ln_b[l][None],
                                 jnp.zeros((5, RWKV_W), F32)], 0)
        ln1 = jnp.stack([ln1_w[l], ln1_b[l]], 0)
        xa = _out_proj(xa, h_m, zm, o_s, g_tok, bonus, att_l, att_c, modv, vec_o, hmean, w_out[l].astype(BF16), ln1,
                       t_lat=t_lat, tm=tm, alpha=alpha)

        ids, gts, h_rows = _router(xa, modv, router_w[l].T, router_bias[l].reshape(N_EXPERTS, 1), t_lat=t_lat, tm=tm)
        tok, dst, g_sorted, tables = _moe_tables(ids, gts, ta=ta, nv=nv_moe, nchunk=d // 128)
        routed = _moe_routed(h_rows, tok, dst, g_sorted, tables, *exp_w, nv=nv_moe, layer=l)
        ln2 = jnp.stack([ln2_w[l], ln2_b[l]], 0)
        xa = _moe_finish(xa, routed, modv, sh_w_gate[l].astype(BF16), sh_w_up[l].astype(BF16),
                         sh_w_down[l].astype(BF16), ln2, t_lat=t_lat, tm=tm, alpha=alpha,
                         n_rows=t_lat if l == depth - 1 else ta)
    return xa
```

```python
import functools

import jax
import jax.numpy as jnp
import numpy as np
from jax import lax
from jax.experimental import pallas as pl
from jax.experimental.pallas import tpu as pltpu

F32 = jnp.float32
BF16 = jnp.bfloat16
HIGHEST = lax.Precision.HIGHEST

GRID_W = 64
MLSTM_H, MLSTM_DH = 4, 64
MLSTM_W = MLSTM_H * MLSTM_DH
RWKV_H, RWKV_N = 4, 64
RWKV_W = RWKV_H * RWKV_N
DECAY_LORA, AAA_LORA, GATE_LORA = 64, 64, 128
RWKV_GN_EPS = 64e-5
MLA_H, MLA_NOPE, MLA_ROPE, MLA_V = 8, 64, 32, 64
Q_RANK, KV_RANK = 256, 128
ROPE_THETA = 10000.0
N_MLSTM_IN = 4 * MLSTM_W + 4 * MLSTM_H
N_RWKV_IN = 3 * RWKV_W + 2 * DECAY_LORA + 2 * AAA_LORA + GATE_LORA
N_EXPERTS, TOP_K, N_GROUPS, TOPK_GROUPS = 64, 8, 8, 4
ROUTED_SCALE = 2.5
LN_EPS = 1e-6
NEG = -1e30

C_QK, C_R, C_V, C_O, C_G, C_A = 0, 512, 1664, 1920, 2176, 2304
N_SHIFT = 1664
N_A = 640
N_EXT = C_A + N_A
HALO = 8
VMEM_LIMIT = 56 * 1024 * 1024


def _cp(sem):
    return pltpu.CompilerParams(dimension_semantics=sem, vmem_limit_bytes=VMEM_LIMIT)


def _ln(x):
    mu = jnp.mean(x, axis=-1, keepdims=True)
    xc = x - mu
    return xc * lax.rsqrt(jnp.mean(xc * xc, axis=-1, keepdims=True) + LN_EPS)


def _sigmoid(x):
    return 1.0 / (1.0 + jnp.exp(-x))


def _silu(x):
    return x * _sigmoid(x)


def _log_sigmoid(x):
    return jnp.minimum(x, 0.0) - jnp.log(1.0 + jnp.exp(-jnp.abs(x)))


def _softplus(x):
    return jnp.maximum(x, 0.0) + jnp.log(1.0 + jnp.exp(-jnp.abs(x)))


def _nt_dot(a, b, **kw):
    return lax.dot_general(a, b, (((1,), (1,)), ((), ())), preferred_element_type=F32, **kw)


def _tn_dot(a, b, **kw):
    return lax.dot_general(a, b, (((0,), (0,)), ((), ())), preferred_element_type=F32, **kw)


def _dot(a, b, **kw):
    return jnp.dot(a, b, preferred_element_type=F32, **kw)


def _split3(a):
    hi = a.astype(BF16)
    r = a - hi.astype(F32)
    mid = r.astype(BF16)
    return hi, mid, (r - mid.astype(F32)).astype(BF16)


def _dot_f32_by_exact(a, b_exact):
    b = b_exact.astype(BF16)
    hi, mid, lo = _split3(a)
    return _dot(hi, b) + _dot(mid, b) + _dot(lo, b)


def _mod_kernel(c_ref, w_ref, b_ref, o_ref):
    o_ref[0] = _dot(_silu(c_ref[...]), w_ref[0], precision=HIGHEST) + b_ref[0]


def _modulation(cc, w_mod, b_mod):
    n_layers, d, n = w_mod.shape
    tn = 1536
    return pl.pallas_call(
        _mod_kernel,
        grid=(n_layers, n // tn),
        in_specs=[pl.BlockSpec((8, d), lambda l, j: (0, 0)),
                  pl.BlockSpec((1, d, tn), lambda l, j: (l, 0, j)),
                  pl.BlockSpec((1, 1, tn), lambda l, j: (l, 0, j))],
        out_specs=pl.BlockSpec((1, 8, tn), lambda l, j: (l, 0, j)),
        out_shape=jax.ShapeDtypeStruct((n_layers, 8, n), F32),
        compiler_params=_cp(("parallel", "parallel")),
        name="modulation",
    )(cc, w_mod, b_mod.reshape(n_layers, 1, n))


def _in_kernel(xp_ref, x_ref, xn_ref, mod_ref, w_ref, conv_ref, mu_ref, cos_ref, sin_ref, nq_ref, nkv_ref,
               wq_ref, wqr_ref, wk_ref, wv_ref, zm_ref, zg_ref, zr_ref, q_ref, k_ref, vt_ref, *, tm, nt_lat, nt):
    i = pl.program_id(1)
    has_prev = jnp.logical_and(i != 0, i != nt_lat)
    has_next = jnp.logical_and(i != nt_lat - 1, i != nt - 1)
    xt = jnp.concatenate([xp_ref[0], x_ref[0], xn_ref[0]], axis=0)
    sh = mod_ref[0, 0, 0:1, :]
    sc = mod_ref[0, 0, 1:2, :]
    h = (_ln(xt) * (1.0 + sc) + sh).astype(BF16)
    z = _dot(h, w_ref[...])
    rows = lax.broadcasted_iota(jnp.int32, (tm + 2 * HALO, 1), 0)
    lo = jnp.where(has_prev, 0, HALO)
    hi = jnp.where(has_next, tm + 2 * HALO, tm + HALO)
    keep = jnp.logical_and(rows >= lo, rows < hi)
    zs = jnp.where(keep, z[:, :N_SHIFT], 0.0)
    zc = zs[HALO:HALO + tm]
    zprev = pltpu.roll(zs, 1, axis=0)[HALO:HALO + tm]
    znext = pltpu.roll(zs, tm + 2 * HALO - 1, axis=0)[HALO:HALO + tm]
    cw = conv_ref[...]
    qk = (cw[0:1] * zprev[:, :C_R] + cw[1:2] * zc[:, :C_R] + cw[2:3] * znext[:, :C_R])
    qk = _silu(qk)
    lane = lax.broadcasted_iota(jnp.int32, (1, C_R), 1)
    qk = qk * jnp.where(lane < MLSTM_W, MLSTM_DH ** -0.5, 1.0)
    zm_ref[0, :, 0:512] = qk
    zm_ref[0, :, 512:1024] = z[HALO:HALO + tm, C_V:C_G]
    zg_ref[0] = z[HALO:HALO + tm, C_G:C_A]
    zr = zc[:, C_R:]
    zr_ref[0] = zr + mu_ref[...] * (0.5 * (zprev[:, C_R:] + znext[:, C_R:]) - zr)
    _mla_project(z[HALO:HALO + tm, C_A:], cos_ref, sin_ref, nq_ref, nkv_ref, wq_ref, wqr_ref, wk_ref, wv_ref,
                 q_ref, k_ref, vt_ref)


def _in_proj(xa, modv, w_ext, conv, mu, mla, *, t_lat, tm):
    b, ta, d = xa.shape
    full = lambda a: pl.BlockSpec(a.shape, lambda bi, i: (0,) * a.ndim)
    rope = pl.BlockSpec((tm, 128), lambda bi, i: (i, 0))
    nt, nt_lat = ta // tm, t_lat // tm
    nh = tm // HALO
    last = ta // HALO - 1
    kern = functools.partial(_in_kernel, tm=tm, nt_lat=nt_lat, nt=nt)
    return pl.pallas_call(
        kern,
        grid=(b, nt),
        in_specs=[pl.BlockSpec((1, HALO, d), lambda bi, i: (bi, jnp.maximum(i * nh - 1, 0), 0)),
                  pl.BlockSpec((1, tm, d), lambda bi, i: (bi, i, 0)),
                  pl.BlockSpec((1, HALO, d), lambda bi, i: (bi, jnp.minimum((i + 1) * nh, last), 0)),
                  pl.BlockSpec((1, 1, 6, d), lambda bi, i: (bi, i // nt_lat, 0, 0)),
                  pl.BlockSpec((d, N_EXT), lambda bi, i: (0, 0)),
                  pl.BlockSpec((3, C_R), lambda bi, i: (0, 0)),
                  pl.BlockSpec((1, N_RWKV_IN), lambda bi, i: (0, 0)),
                  rope, rope] + [full(a) for a in mla[2:]],
        out_specs=[pl.BlockSpec((1, tm, 1024), lambda bi, i: (bi, i, 0)),
                   pl.BlockSpec((1, tm, 128), lambda bi, i: (bi, i, 0)),
                   pl.BlockSpec((1, tm, N_RWKV_IN), lambda bi, i: (bi, i, 0)),
                   pl.BlockSpec((1, tm, 1024), lambda bi, i: (bi, i, 0)),
                   pl.BlockSpec((1, tm, 1024), lambda bi, i: (bi, i, 0)),
                   pl.BlockSpec((1, 512, tm), lambda bi, i: (bi, 0, i))],
        out_shape=[jax.ShapeDtypeStruct((b, ta, 1024), F32),
                   jax.ShapeDtypeStruct((b, ta, 128), F32),
                   jax.ShapeDtypeStruct((b, ta, N_RWKV_IN), F32),
                   jax.ShapeDtypeStruct((b, ta, 1024), BF16),
                   jax.ShapeDtypeStruct((b, ta, 1024), BF16),
                   jax.ShapeDtypeStruct((b, 512, ta), BF16)],
        compiler_params=_cp(("parallel", "parallel")),
        name="in_proj",
    )(xa, xa, xa, modv, w_ext, conv, mu, *mla)


def _mlstm_kernel(zm_ref, gc_ref, gr_ref, bc_ref, br_ref, o_ref, ct_ref, n_ref, m_ref, *, cl):
    d = pl.program_id(1)

    @pl.when(pl.program_id(2) == 0)
    def _():
        ct_ref[...] = jnp.zeros_like(ct_ref)
        n_ref[...] = jnp.zeros_like(n_ref)
        m_ref[...] = jnp.zeros_like(m_ref)

    sgn = 1 - 2 * d
    gc = gc_ref[0, 0] + bc_ref[0]
    gr = gr_ref[0, 0] + br_ref[0]
    lf_c = _log_sigmoid(gc)
    lf_r = _log_sigmoid(gr)
    ii = lax.broadcasted_iota(jnp.int32, (cl, cl), 0)
    jj = lax.broadcasted_iota(jnp.int32, (cl, cl), 1)
    mask = (jj - ii) * sgn <= 0
    mask_t = (ii - jj) * sgn <= 0
    cum_c = _dot(mask.astype(F32), lf_c, precision=HIGHEST)
    cum_r = _dot(lf_r, mask_t.astype(F32), precision=HIGHEST)
    outs = []
    for h in range(MLSTM_H):
        sl = slice(h * MLSTM_DH, (h + 1) * MLSTM_DH)
        q = zm_ref[0, :, sl]
        k = zm_ref[0, :, MLSTM_W + h * MLSTM_DH:MLSTM_W + (h + 1) * MLSTM_DH]
        v = zm_ref[0, :, 2 * MLSTM_W + h * MLSTM_DH:2 * MLSTM_W + (h + 1) * MLSTM_DH]
        qb, kb = q.astype(BF16), k.astype(BF16)
        li_c, li_r = gc[:, h:h + 1], gr[h:h + 1, :]
        cu_c, cu_r = cum_c[:, 4 + h:5 + h], cum_r[4 + h:5 + h, :]
        b_end = jnp.sum(lf_r[4 + h:5 + h, :], axis=1, keepdims=True)
        m_prev = m_ref[h:h + 1, 0:1]
        ct = ct_ref[h]
        nv = n_ref[h:h + 1, 0:MLSTM_DH]
        d_log = jnp.where(mask, cu_c - cu_r + li_r, NEG)
        g_log = cu_c + m_prev
        m_row = jnp.maximum(g_log, jnp.max(d_log, axis=1, keepdims=True))
        w_intra = jnp.exp(d_log - m_row) * _nt_dot(qb, kb)
        e_inter = jnp.exp(g_log - m_row)
        num = _dot(w_intra.astype(BF16), v.astype(BF16)) + e_inter * _dot(qb, ct.astype(BF16))
        den = jnp.sum(w_intra, axis=1, keepdims=True) + e_inter * jnp.sum(q * nv, axis=1, keepdims=True)
        outs.append(num / jnp.maximum(jnp.abs(den), jnp.exp(-m_row)))
        w_end = b_end - cu_c + li_c
        m_loc = jnp.max(w_end, axis=0, keepdims=True)
        e_end = jnp.exp(w_end - m_loc)
        m_new = jnp.maximum(b_end + m_prev, m_loc)
        a_old = jnp.exp(b_end + m_prev - m_new)
        a_loc = jnp.exp(m_loc - m_new)
        ct_ref[h] = a_old * ct + a_loc * _tn_dot(kb, (v * e_end).astype(BF16))
        n_ref[h:h + 1, 0:MLSTM_DH] = a_old * nv + a_loc * jnp.sum(k * e_end, axis=0, keepdims=True)
        m_ref[h:h + 1, :] = jnp.broadcast_to(m_new, (1, 128))
    o_ref[0, 0] = jnp.concatenate(outs, axis=1)


def _mlstm(zm, g_col, g_row, b_col, b_row, *, t_lat, cl):
    b, ta, _ = zm.shape
    nc, nc_lat = ta // cl, t_lat // cl
    nc_ctx = nc - nc_lat

    def chunk(di, c):
        fwd_idx = jnp.where(c < nc_ctx, nc_lat + c, c - nc_ctx)
        return jnp.where(di == 0, fwd_idx, nc - 1 - c)

    return pl.pallas_call(
        functools.partial(_mlstm_kernel, cl=cl),
        grid=(b, 2, nc),
        in_specs=[pl.BlockSpec((1, cl, 1024), lambda bi, di, c: (bi, chunk(di, c), 0)),
                  pl.BlockSpec((1, 1, cl, 8), lambda bi, di, c: (bi, di, chunk(di, c), 0)),
                  pl.BlockSpec((1, 1, 8, cl), lambda bi, di, c: (bi, di, 0, chunk(di, c))),
                  pl.BlockSpec((1, 1, 8), lambda bi, di, c: (di, 0, 0)),
                  pl.BlockSpec((1, 8, 1), lambda bi, di, c: (di, 0, 0))],
        out_specs=pl.BlockSpec((1, 1, cl, MLSTM_W), lambda bi, di, c: (bi, di, chunk(di, c), 0)),
        out_shape=jax.ShapeDtypeStruct((b, 2, ta, MLSTM_W), F32),
        scratch_shapes=[pltpu.VMEM((MLSTM_H, MLSTM_DH, MLSTM_DH), F32),
                        pltpu.VMEM((8, 128), F32),
                        pltpu.VMEM((8, 128), F32)],
        compiler_params=_cp(("parallel", "parallel", "arbitrary")),
        name="mlstm",
    )(zm, g_col, g_row, b_col, b_row)


def _rwkv_prep_kernel(zr_ref, wup_ref, aup_ref, gup_ref, vec_ref, hsum_ref, pd_ref, v_ref, g_ref, bonus_ref):
    nb, tm = zr_ref.shape[0], zr_ref.shape[1]
    nbh = nb * RWKV_H
    vec = vec_ref[...]
    hsum = hsum_ref[...]
    zero = jnp.zeros((tm, RWKV_N), F32)
    for bi in range(nb):
        z = zr_ref[bi]
        zr_, zk, zv = z[:, 0:256], z[:, 256:512], z[:, 512:768]
        zw, za, zg = z[:, 768:896], z[:, 896:1024], z[:, 1024:1152]
        w_raw = _dot(jnp.tanh(zw).astype(BF16), wup_ref[...])
        a_raw = _dot(za.astype(BF16), aup_ref[...])
        g_ref[bi] = _dot(_sigmoid(zg).astype(BF16), gup_ref[...])
        kk = zk * vec[4:5]
        kk_ss = _dot_f32_by_exact(kk * kk, hsum)
        kk = kk * lax.rsqrt(jnp.maximum(kk_ss, 1e-12))
        k_sum = jnp.zeros_like(zk)
        for di in range(2):
            sl = slice(di * RWKV_W, (di + 1) * RWKV_W)
            decay = jnp.exp(-jnp.exp(-_softplus(-(vec[di:di + 1] + w_raw[:, sl])) - 0.5))
            a = _sigmoid(vec[2 + di:3 + di] + a_raw[:, sl])
            k_dir = zk * (1.0 + (a - 1.0) * vec[5:6])
            k_sum = k_sum + k_dir
            ka = kk * a
            for h in range(RWKV_H):
                hs = slice(h * RWKV_N, (h + 1) * RWKV_N)
                rows = pl.ds(bi * RWKV_H + h, tm, stride=nbh)
                pd_ref[di, 0, rows, :] = jnp.concatenate([decay[:, hs], ka[:, hs]], axis=1)
                pd_ref[di, 1, rows, :] = jnp.concatenate([k_dir[:, hs], kk[:, hs]], axis=1)
                pd_ref[di, 2, rows, :] = jnp.concatenate([zr_[:, hs], zero], axis=1)
        v_ref[bi] = zv
        bonus_ref[bi] = _dot_f32_by_exact(zr_ * vec[6:7] * k_sum, hsum) * zv


def _rwkv_prep(zr, wup, aup, gup, vec, hsum, *, tm):
    b, ta, _ = zr.shape
    nbh = b * RWKV_H
    tok = pl.BlockSpec((b, tm, RWKV_W), lambda i: (0, i, 0))
    full = lambda a: pl.BlockSpec(a.shape, lambda i: (0,) * a.ndim)
    return pl.pallas_call(
        _rwkv_prep_kernel,
        grid=(ta // tm,),
        in_specs=[pl.BlockSpec((b, tm, N_RWKV_IN), lambda i: (0, i, 0)),
                  full(wup), full(aup), full(gup), full(vec), full(hsum)],
        out_specs=[pl.BlockSpec((2, 3, tm * nbh, 128), lambda i: (0, 0, i, 0)), tok, tok, tok],
        out_shape=[jax.ShapeDtypeStruct((2, 3, ta * nbh, 128), F32)] + [jax.ShapeDtypeStruct((b, ta, RWKV_W), F32)] * 3,
        compiler_params=_cp(("parallel",)),
        name="rwkv_prep",
    )(zr, wup, aup, gup, vec, hsum)


def _rwkv_scan_kernel(pf_ref, pb_ref, pat_ref, vf_ref, vb_ref, of_ref, ob_ref, st_ref, x_ref, tt_ref, *, ts, nvh, nbh):
    ng = ts // 8
    rows = 8 * nbh

    @pl.when(pl.program_id(0) == 0)
    def _():
        st_ref[...] = jnp.zeros_like(st_ref)

    def transpose(di, p_ref, grp):
        sl = pl.ds(pl.multiple_of(grp * rows, rows), rows)
        for j in range(3):
            tt_ref[di, j] = p_ref[0, j, sl, :].T

    def update(di, u, v_ref, o_ref, t):
        pat = pat_ref[u]
        for j, (tj, half) in enumerate(((0, 0), (0, 1), (1, 0), (1, 1), (2, 0))):
            x_ref[di,j] = jnp.take_along_axis(tt_ref[di, tj, half * RWKV_N:(half + 1) * RWKV_N, :], pat, axis=1)
        vt = v_ref[t]
        nkb = RWKV_N // 8
        kblk = lambda kb: pl.ds(kb * 8, 8)
        acc = [None] * nvh
        for kb in range(nkb):
            kk = x_ref[di,3, kblk(kb), :]
            for vh in range(nvh):
                term = st_ref[di, vh, kblk(kb), :] * kk
                acc[vh] = term if kb == 0 else acc[vh] + term
        sa = [jnp.sum(a, axis=0, keepdims=True) for a in acc]
        out = [None] * nvh
        for kb in range(nkb):
            w, ka = x_ref[di,0, kblk(kb), :], x_ref[di,1, kblk(kb), :]
            k, r = x_ref[di,2, kblk(kb), :], x_ref[di,4, kblk(kb), :]
            for vh in range(nvh):
                sv = st_ref[di, vh, kblk(kb), :] * w - sa[vh] * ka + vt[vh:vh + 1, :] * k
                st_ref[di, vh, kblk(kb), :] = sv
                out[vh] = sv * r if kb == 0 else out[vh] + sv * r
        o_ref[t] = jnp.concatenate([jnp.sum(o, axis=0, keepdims=True) for o in out], axis=0)

    def group(g, carry):
        transpose(0, pf_ref, g)
        transpose(1, pb_ref, ng - 1 - g)
        for u in range(8):
            update(0, u, vf_ref, of_ref, g * 8 + u)
            update(1, 7 - u, vb_ref, ob_ref, ts - 1 - (g * 8 + u))
        return carry

    lax.fori_loop(0, ng, group, 0)


def _rwkv_scan(pd, pats, vs, *, t_lat, ts):
    ta, nvh, nl = vs.shape
    nbh = pd.shape[2] // ta
    nt, nt_lat = ta // ts, t_lat // ts
    fwd = lambda i: (i + nt_lat) % nt
    bwd = lambda i: nt - 1 - i
    return pl.pallas_call(
        functools.partial(_rwkv_scan_kernel, ts=ts, nvh=nvh, nbh=nbh),
        grid=(nt,),
        in_specs=[pl.BlockSpec((1, 3, ts * nbh, 128), lambda i: (0, 0, fwd(i), 0)),
                  pl.BlockSpec((1, 3, ts * nbh, 128), lambda i: (1, 0, bwd(i), 0)),
                  pl.BlockSpec(pats.shape, lambda i: (0, 0, 0)),
                  pl.BlockSpec((ts, nvh, nl), lambda i: (fwd(i), 0, 0)),
                  pl.BlockSpec((ts, nvh, nl), lambda i: (bwd(i), 0, 0))],
        out_specs=[pl.BlockSpec((ts, nvh, nl), lambda i: (fwd(i), 0, 0)),
                   pl.BlockSpec((ts, nvh, nl), lambda i: (bwd(i), 0, 0))],
        out_shape=[jax.ShapeDtypeStruct((ta, nvh, nl), F32)] * 2,
        scratch_shapes=[pltpu.VMEM((2, nvh, RWKV_N, nl), F32), pltpu.VMEM((2, 5, RWKV_N, nl), F32),
                        pltpu.VMEM((2, 3, 128, 8 * nbh), F32)],
        compiler_params=_cp(("arbitrary",)),
        name="rwkv_scan",
    )(pd, pd, pats, vs, vs)


def _mla_project(za, cos_ref, sin_ref, nq_ref, nkv_ref, wq_ref, wqr_ref, wk_ref, wv_ref, q_ref, k_ref, v_ref):
    cq, ckv = za[:, 0:Q_RANK], za[:, Q_RANK:Q_RANK + KV_RANK]
    kr, krr = za[:, 384:512], za[:, 512:640]
    cos, sin = cos_ref[...], sin_ref[...]
    cqn = (cq * lax.rsqrt(jnp.mean(cq * cq, axis=-1, keepdims=True) + 1e-6) * nq_ref[...]).astype(BF16)
    ckvn = (ckv * lax.rsqrt(jnp.mean(ckv * ckv, axis=-1, keepdims=True) + 1e-6) * nkv_ref[...]).astype(BF16)
    q = _dot(cqn, wq_ref[...])
    qr = _dot(cqn, wqr_ref[...])
    kn = _dot(ckvn, wk_ref[...])
    v_ref[0] = _nt_dot(wv_ref[...], ckvn).astype(BF16)
    k_rope = kr * cos + krr * sin
    scale = (MLA_NOPE + MLA_ROPE) ** -0.5 * np.log2(np.e)
    for h in range(MLA_H):
        sl = slice(h * 128, (h + 1) * 128)
        q_ref[0, :, sl] = ((q[:, sl] * cos + qr[:, sl] * sin) * scale).astype(BF16)
        k_ref[0, :, sl] = (kn[:, sl] + k_rope).astype(BF16)


def _attn_kernel(q_ref, k_ref, vt_ref, o_ref):
    outs = []
    for h in range(ATTN_HEADS):
        sl = slice(h * 128, (h + 1) * 128)
        pair = h // 2
        st = _nt_dot(k_ref[0, :, sl], q_ref[0, :, sl])
        m = jnp.max(st, axis=0, keepdims=True)
        p = jnp.exp2(st - m)
        l = jnp.sum(p, axis=0, keepdims=True)
        ot = _dot(vt_ref[0, pair * 128:(pair + 1) * 128, :], p.astype(BF16))
        outs.append(ot[(h % 2) * MLA_V:(h % 2 + 1) * MLA_V] / l)
    o_ref[0] = jnp.concatenate(outs, axis=0).T


ATTN_HEADS = 4


def _attention(q, k, vt, *, q_off, n_q, k_off, n_keys, tq):
    b = q.shape[0]
    qo, ko = q_off // tq, k_off // n_keys
    hq, hv = ATTN_HEADS * 128, ATTN_HEADS * MLA_V
    return pl.pallas_call(
        _attn_kernel,
        grid=(b, MLA_H // ATTN_HEADS, n_q // tq),
        in_specs=[pl.BlockSpec((1, tq, hq), lambda bi, hp, i: (bi, qo + i, hp)),
                  pl.BlockSpec((1, n_keys, hq), lambda bi, hp, i: (bi, ko, hp)),
                  pl.BlockSpec((1, hv, n_keys), lambda bi, hp, i: (bi, hp, ko))],
        out_specs=pl.BlockSpec((1, tq, hv), lambda bi, hp, i: (bi, i, hp)),
        out_shape=jax.ShapeDtypeStruct((b, n_q, 512), F32),
        compiler_params=_cp(("parallel", "parallel", "parallel")),
        name="mla_attention",
    )(q, k, vt)


def _out_kernel(x_ref, hf_ref, hb_ref, zm_ref, of_ref, ob_ref, g_ref, bonus_ref, attl_ref, attc_ref, mod_ref, vec_ref,
                hmean_ref, wo_ref, ln_ref, o_ref, *, alpha, nt_lat):
    hmean = hmean_ref[...]
    att = jnp.where(pl.program_id(1) < nt_lat, attl_ref[0], attc_ref[0])

    def head_norm(y, eps):
        mu = _dot_f32_by_exact(y, hmean)
        yc = y - mu
        return yc * lax.rsqrt(_dot_f32_by_exact(yc * yc, hmean) + eps)

    vec = vec_ref[...]
    zo = zm_ref[0]
    m_mix = _sigmoid(zo) * (head_norm(hf_ref[0, 0] + hb_ref[0, 0], 1e-6) * vec[0:1])
    r_o = head_norm(of_ref[0, 0] + ob_ref[0, 0], RWKV_GN_EPS) * vec[1:2] + vec[2:3] + bonus_ref[0]
    r_mix = r_o * g_ref[0]
    mix = jnp.concatenate([m_mix, r_mix, att], axis=1).astype(BF16)
    y = _dot(mix, wo_ref[...])
    gate = mod_ref[0, 0, 2:3, :]
    o_ref[0] = _ln(alpha * x_ref[0] + gate * y) * ln_ref[0:1] + ln_ref[1:2]


def _out_proj(xa, h_m, zm, o_r, g, bonus, att_l, att_c, modv, vec, hmean, wo, ln, *, t_lat, tm, alpha):
    b, ta, d = xa.shape
    nt_lat = t_lat // tm
    tok = lambda w: pl.BlockSpec((1, tm, w), lambda bi, i: (bi, i, 0))
    full = lambda a: pl.BlockSpec(a.shape, lambda bi, i: (0,) * a.ndim)
    return pl.pallas_call(
        functools.partial(_out_kernel, alpha=alpha, nt_lat=nt_lat),
        grid=(b, ta // tm),
        in_specs=[tok(d),
                  pl.BlockSpec((1, 1, tm, MLSTM_W), lambda bi, i: (bi, 0, i, 0)),
                  pl.BlockSpec((1, 1, tm, MLSTM_W), lambda bi, i: (bi, 1, i, 0)),
                  pl.BlockSpec((1, tm, MLSTM_W), lambda bi, i: (bi, i, 3)),
                  pl.BlockSpec((1, 1, tm, RWKV_W), lambda bi, i: (0, bi, i, 0)),
                  pl.BlockSpec((1, 1, tm, RWKV_W), lambda bi, i: (1, bi, i, 0)),
                  tok(RWKV_W), tok(RWKV_W),
                  pl.BlockSpec((1, tm, 512), lambda bi, i: (bi, jnp.minimum(i, nt_lat - 1), 0)),
                  pl.BlockSpec((1, tm, 512), lambda bi, i: (bi, jnp.maximum(i - nt_lat, 0), 0)),
                  pl.BlockSpec((1, 1, 6, d), lambda bi, i: (bi, i // nt_lat, 0, 0)),
                  full(vec), full(hmean), full(wo), full(ln)],
        out_specs=tok(d),
        out_shape=jax.ShapeDtypeStruct((b, ta, d), F32),
        compiler_params=_cp(("parallel", "parallel")),
        name="out_proj",
    )(xa, h_m, h_m, zm, o_r, o_r, g, bonus, att_l, att_c, modv, vec, hmean, wo, ln)


def _first_max(x, idx, axis):
    mx = jnp.max(x, axis=axis, keepdims=True)
    first = jnp.min(jnp.where(x == mx, idx, 1 << 20), axis=axis, keepdims=True)
    return mx, first, idx == first


def _router_kernel(x_ref, mod_ref, w_ref, bias_ref, id_ref, gate_ref, h_ref):
    tm = x_ref.shape[1]
    sh, sc = mod_ref[0, 0, 3:4, :], mod_ref[0, 0, 4:5, :]
    h = _ln(x_ref[0]) * (1.0 + sc) + sh
    for j in range(h.shape[1] // 128):
        h_ref[0, pl.ds(j, tm, stride=h.shape[1] // 128), :] = h[:, j * 128:(j + 1) * 128]
    scores = _sigmoid(_nt_dot(w_ref[...], h, precision=HIGHEST))
    biased = scores + bias_ref[...]
    gsz = N_EXPERTS // N_GROUPS
    b3 = biased.reshape(N_GROUPS, gsz, tm)
    s3 = scores.reshape(N_GROUPS, gsz, tm)
    e_idx = lax.broadcasted_iota(jnp.int32, (N_GROUPS, gsz, tm), 1)
    m1, _, hit = _first_max(b3, e_idx, 1)
    m2 = jnp.max(jnp.where(hit, -jnp.inf, b3), axis=1, keepdims=True)
    gscore = m1 + m2
    g_idx = lax.broadcasted_iota(jnp.int32, (N_GROUPS, 1, tm), 0)
    gsel = jnp.zeros((N_GROUPS, 1, tm), F32)
    for _ in range(TOPK_GROUPS):
        _, _, hit = _first_max(jnp.where(gsel > 0, -jnp.inf, gscore), g_idx, 0)
        gsel = jnp.where(hit, 1.0, gsel)
    cand = jnp.where(jnp.broadcast_to(gsel, b3.shape) > 0, b3, -jnp.inf)
    x_idx = g_idx * gsz + e_idx
    sel = jnp.zeros(b3.shape, F32)
    ids, picked = [], []
    for _ in range(TOP_K):
        _, first, hit = _first_max(jnp.where(sel > 0, -jnp.inf, cand), x_idx, (0, 1))
        sel = jnp.where(hit, 1.0, sel)
        ids.append(first[0])
        picked.append(jnp.sum(jnp.where(hit, s3, 0.0), axis=(0, 1), keepdims=True)[0])
    picked = jnp.concatenate(picked, axis=0)
    id_ref[0] = jnp.concatenate(ids, axis=0)
    gate_ref[0] = ROUTED_SCALE * picked / jnp.sum(picked, axis=0, keepdims=True)


def _router(xa, modv, w_t, bias, *, t_lat, tm):
    b, ta, d = xa.shape
    nt_lat = t_lat // tm
    return pl.pallas_call(
        _router_kernel,
        grid=(b, ta // tm),
        in_specs=[pl.BlockSpec((1, tm, d), lambda bi, i: (bi, i, 0)),
                  pl.BlockSpec((1, 1, 6, d), lambda bi, i: (bi, i // nt_lat, 0, 0)),
                  pl.BlockSpec((N_EXPERTS, d), lambda bi, i: (0, 0)),
                  pl.BlockSpec((N_EXPERTS, 1), lambda bi, i: (0, 0))],
        out_specs=[pl.BlockSpec((1, TOP_K, tm), lambda bi, i: (bi, 0, i)),
                   pl.BlockSpec((1, TOP_K, tm), lambda bi, i: (bi, 0, i)),
                   pl.BlockSpec((1, tm * (d // 128), 128), lambda bi, i: (bi, i, 0))],
        out_shape=[jax.ShapeDtypeStruct((b, TOP_K, ta), jnp.int32),
                   jax.ShapeDtypeStruct((b, TOP_K, ta), F32),
                   jax.ShapeDtypeStruct((b, ta * (d // 128), 128), F32)],
        compiler_params=_cp(("parallel", "parallel")),
        name="router",
    )(xa, modv, w_t, bias)


MOE_RB = 256
MOE_PITCH = MOE_RB + 8
MOE_U = 8


def _moe_routed_kernel(vblk_ref, vexp_ref, vlo_ref, vhi_ref, vvalid_ref, vfirst_ref,
                       tok_ref, dst_ref, gate_ref, src_ref, wg_ref, wu_ref, wd_ref, acc_ref, xt_ref, yt_ref, xb_ref,
                       gc_ref, *, nv, nchunk):
    bi, v = pl.program_id(0), pl.program_id(1)
    pos = bi * nv + v
    lo, hi = vlo_ref[pos], vhi_ref[pos]

    @pl.when(v == 0)
    def _():
        acc_ref[...] = jnp.zeros_like(acc_ref)

    @pl.when(vfirst_ref[pos] == 1)
    def _():
        for r in range(MOE_RB):
            off = pl.multiple_of(tok_ref[0, 0, 0, r], nchunk)
            xt_ref[pl.ds(r, nchunk, stride=MOE_PITCH), :] = src_ref[0, pl.ds(off, nchunk), :]
        for j in range(nchunk):
            xb_ref[:, j * 128:(j + 1) * 128] = xt_ref[j * MOE_PITCH:j * MOE_PITCH + MOE_RB, :].astype(BF16)
        ii = lax.broadcasted_iota(jnp.int32, (MOE_RB, MOE_RB), 0)
        jj = lax.broadcasted_iota(jnp.int32, (MOE_RB, MOE_RB), 1)
        gc_ref[...] = jnp.sum(jnp.where(ii == jj, gate_ref[0, 0], 0.0), axis=1, keepdims=True)

    @pl.when(vvalid_ref[pos] == 1)
    def _():
        x = xb_ref[...]
        row = lax.broadcasted_iota(jnp.int32, (MOE_RB, 1), 0)
        gcol = jnp.where(jnp.logical_and(row >= lo, row < hi), gc_ref[...], 0.0)
        a = _silu(_dot(x, wg_ref[0])) * _dot(x, wu_ref[0]) * gcol
        y = _dot(a.astype(BF16), wd_ref[0])
        for j in range(nchunk):
            yt_ref[j * MOE_PITCH:j * MOE_PITCH + MOE_RB, :] = y[:, j * 128:(j + 1) * 128]
        for g in range(MOE_RB // MOE_U):
            pending = []
            for u in range(MOE_U):
                r = g * MOE_U + u
                off = pl.multiple_of(dst_ref[0, 0, 0, r], nchunk)
                pending.append((off, acc_ref[0, pl.ds(off, nchunk), :] + yt_ref[pl.ds(r, nchunk, stride=MOE_PITCH), :]))
            for off, val in pending:
                acc_ref[0, pl.ds(off, nchunk), :] = val


def _moe_routed(h_rows, tok, dst, gate, tables, wg, wu, wd, *, nv, layer):
    b, rows, _ = h_rows.shape
    d, de = wg.shape[1], wg.shape[2]
    nchunk = d // 128
    nblk = tok.shape[1] // MOE_RB
    blk = lambda bi, v, vblk, *_: (bi, vblk[bi * nv + v], 0, 0)
    exp = lambda bi, v, vblk, vexp, *_: (layer * N_EXPERTS + vexp[bi * nv + v], 0, 0)
    grid_spec = pltpu.PrefetchScalarGridSpec(
        num_scalar_prefetch=6,
        grid=(b, nv),
        in_specs=[pl.BlockSpec((1, 1, 1, MOE_RB), blk, memory_space=pltpu.SMEM),
                  pl.BlockSpec((1, 1, 1, MOE_RB), lambda bi, v, *_: (bi, v, 0, 0), memory_space=pltpu.SMEM),
                  pl.BlockSpec((1, 1, 1, MOE_RB), blk),
                  pl.BlockSpec((1, rows, 128), lambda bi, v, *_: (bi, 0, 0), pipeline_mode=pl.Buffered(1)),
                  pl.BlockSpec((1, d, de), exp),
                  pl.BlockSpec((1, d, de), exp),
                  pl.BlockSpec((1, de, d), exp)],
        out_specs=pl.BlockSpec((1, rows + MOE_U * nchunk, 128), lambda bi, v, *_: (bi, 0, 0),
                               pipeline_mode=pl.Buffered(1)),
        scratch_shapes=[pltpu.VMEM((nchunk * MOE_PITCH, 128), F32), pltpu.VMEM((nchunk * MOE_PITCH, 128), F32),
                        pltpu.VMEM((MOE_RB, d), BF16), pltpu.VMEM((MOE_RB, 1), F32)],
    )
    return pl.pallas_call(
        functools.partial(_moe_routed_kernel, nv=nv, nchunk=nchunk),
        grid_spec=grid_spec,
        out_shape=jax.ShapeDtypeStruct((b, rows + MOE_U * nchunk, 128), F32),
        compiler_params=_cp(("parallel", "arbitrary")),
        name="moe_routed",
    )(*tables, tok.reshape(b, nblk, 1, MOE_RB), dst.reshape(b, nv, 1, MOE_RB), gate.reshape(b, nblk, 1, MOE_RB),
      h_rows, wg, wu, wd)


def _moe_tables(ids, gates, *, ta, nv, nchunk):
    b = ids.shape[0]
    n = TOP_K * ta
    nblk = n // MOE_RB
    t_idx = jnp.broadcast_to(jnp.arange(ta, dtype=jnp.int32), (b, TOP_K, ta))
    keys, g_sorted = lax.sort(((ids * ta + t_idx).reshape(b, n), gates.reshape(b, n)), dimension=1, num_keys=1)
    tok, e_sorted = keys % ta, keys // ta
    ends = jnp.sum(e_sorted[:, None, :] <= jnp.arange(N_EXPERTS, dtype=jnp.int32)[None, :, None], axis=2)
    starts = ends - jnp.sum(e_sorted[:, None, :] == jnp.arange(N_EXPERTS, dtype=jnp.int32)[None, :, None], axis=2)
    e_lo, e_hi = e_sorted[:, ::MOE_RB], e_sorted[:, MOE_RB - 1::MOE_RB]
    n_vis = e_hi - e_lo + 1
    v_end = jnp.cumsum(n_vis, axis=1)
    v_start = v_end - n_vis
    v = jnp.arange(nv, dtype=jnp.int32)
    blk = jnp.minimum(jnp.sum(v_end[:, None, :] <= v[None, :, None], axis=2), nblk - 1).astype(jnp.int32)
    def take(a, i):
        pick = i[:, :, None] == jnp.arange(a.shape[1], dtype=jnp.int32)[None, None, :]
        return jnp.sum(jnp.where(pick, a[:, None, :], 0), axis=2)
    valid = v[None, :] < v_end[:, -1:]
    exp = jnp.where(valid, take(e_lo, blk) + v[None, :] - take(v_start, blk), take(e_hi, blk)).astype(jnp.int32)
    lo = jnp.clip(take(starts, exp) - blk * MOE_RB, 0, MOE_RB)
    hi = jnp.where(valid, jnp.clip(take(ends, exp) - blk * MOE_RB, 0, MOE_RB), 0)
    first = jnp.logical_and(valid, v[None, :] == take(v_start, blk))
    flat = lambda a: a.astype(jnp.int32).reshape(-1)
    r = jnp.arange(MOE_RB, dtype=jnp.int32)
    tok_v = jnp.take_along_axis(tok.reshape(b, nblk, MOE_RB), blk[:, :, None], axis=1)
    mine = jnp.logical_and(r >= lo[:, :, None], r < hi[:, :, None])
    dst = jnp.where(mine, tok_v, ta + r % MOE_U)
    rows = nchunk
    return ((tok * rows).astype(jnp.int32), (dst * rows).astype(jnp.int32), g_sorted,
            tuple(flat(a) for a in (blk, exp, lo, hi, valid, first)))


def _moe_finish_kernel(x_ref, r_ref, mod_ref, sg_ref, su_ref, sd_ref, ln_ref, o_ref, *, alpha):
    tm, d = x_ref.shape[1], x_ref.shape[2]
    nchunk = d // 128
    x = x_ref[0]
    h = (_ln(x) * (1.0 + mod_ref[0, 0, 4:5, :]) + mod_ref[0, 0, 3:4, :]).astype(BF16)
    shared = _dot((_silu(_dot(h, sg_ref[...])) * _dot(h, su_ref[...])).astype(BF16), sd_ref[...])
    routed = jnp.concatenate([r_ref[0, pl.ds(j, tm, stride=nchunk), :] for j in range(nchunk)], axis=1)
    o_ref[0] = _ln(alpha * x + mod_ref[0, 0, 5:6, :] * (routed + shared)) * ln_ref[0:1] + ln_ref[1:2]


def _moe_finish(xa, routed, modv, sg, su, sd, ln, *, t_lat, tm, alpha, n_rows):
    b, ta, d = xa.shape
    nt_lat = t_lat // tm
    full = lambda a: pl.BlockSpec(a.shape, lambda bi, i: (0,) * a.ndim)
    return pl.pallas_call(
        functools.partial(_moe_finish_kernel, alpha=alpha),
        grid=(b, n_rows // tm),
        in_specs=[pl.BlockSpec((1, tm, d), lambda bi, i: (bi, i, 0)),
                  pl.BlockSpec((1, tm * (d // 128), 128), lambda bi, i: (bi, i, 0)),
                  pl.BlockSpec((1, 1, 6, d), lambda bi, i: (bi, i // nt_lat, 0, 0)),
                  full(sg), full(su), full(sd), full(ln)],
        out_specs=pl.BlockSpec((1, tm, d), lambda bi, i: (bi, i, 0)),
        out_shape=jax.ShapeDtypeStruct((b, n_rows, d), F32),
        compiler_params=_cp(("parallel", "parallel")),
        name="moe_finish",
    )(xa, routed, modv, sg, su, sd, ln)


def _rope_blocks(w_rope):
    k = w_rope.shape[0]
    ev, od = w_rope[:, 0::2], w_rope[:, 1::2]
    z64, z32 = jnp.zeros((k, 64), w_rope.dtype), jnp.zeros((k, 32), w_rope.dtype)
    return (jnp.concatenate([z64, ev, od, z32], 1), jnp.concatenate([z64, -od, ev, z32], 1))


def _layer_weights(l, w_in, mla_w_uq, mla_w_ukv):
    w = w_in[l]
    d = w.shape[0]
    wm, wr, wa = w[:, :N_MLSTM_IN], w[:, N_MLSTM_IN:N_MLSTM_IN + N_RWKV_IN], w[:, N_MLSTM_IN + N_RWKV_IN:]
    kr, krr = _rope_blocks(wa[:, Q_RANK + KV_RANK:])
    w_ext = jnp.concatenate([
        wm[:, :2 * MLSTM_W], wr, wm[:, 2 * MLSTM_W:4 * MLSTM_W],
        wm[:, 4 * MLSTM_W:], jnp.zeros((d, 128 - 4 * MLSTM_H), w.dtype),
        wa[:, :Q_RANK + KV_RANK], kr, krr], axis=1).astype(BF16)
    uq = mla_w_uq[l].reshape(Q_RANK, MLA_H, MLA_NOPE + MLA_ROPE)
    nope, ev, od = uq[:, :, :MLA_NOPE], uq[:, :, MLA_NOPE::2], uq[:, :, MLA_NOPE + 1::2]
    z64, z32 = jnp.zeros((Q_RANK, MLA_H, 64), F32), jnp.zeros((Q_RANK, MLA_H, 32), F32)
    wq = jnp.concatenate([nope, ev, od, z32], axis=2).reshape(Q_RANK, MLA_H * 128)
    wqr = jnp.concatenate([z64, -od, ev, z32], axis=2).reshape(Q_RANK, MLA_H * 128)
    ukv = mla_w_ukv[l].reshape(KV_RANK, MLA_H, MLA_NOPE + MLA_V)
    wk = jnp.concatenate([ukv[:, :, :MLA_NOPE], jnp.zeros((KV_RANK, MLA_H, 64), F32)], axis=2).reshape(KV_RANK, MLA_H * 128)
    wv = ukv[:, :, MLA_NOPE:].reshape(KV_RANK, MLA_H * MLA_V)
    return w_ext, wq.astype(BF16), wqr.astype(BF16), wk.astype(BF16), wv.T.astype(BF16)


def _rope_tables(t_lat, n_ctx):
    rows = t_lat // GRID_W
    row = jnp.repeat(jnp.arange(rows), GRID_W).astype(F32)
    col = jnp.tile(jnp.arange(GRID_W), rows).astype(F32)
    n_freq = MLA_ROPE // 4
    freq = ROPE_THETA ** (-jnp.arange(n_freq, dtype=F32) / n_freq)
    ang = jnp.concatenate([row[:, None] * freq, col[:, None] * freq], -1)
    cos, sin = jnp.cos(ang), jnp.sin(ang)
    one, zero = jnp.ones((t_lat, 64), F32), jnp.zeros((t_lat, 32), F32)
    cos_l = jnp.concatenate([one, cos, cos, zero], 1)
    sin_l = jnp.concatenate([0 * one, sin, sin, zero], 1)
    cos_c = jnp.concatenate([jnp.ones((n_ctx, 96), F32), jnp.zeros((n_ctx, 32), F32)], 1)
    return jnp.concatenate([cos_l, cos_c], 0), jnp.concatenate([sin_l, jnp.zeros((n_ctx, 128), F32)], 0)


def kernel(x, c, ctx, c_ctx, w_mod, b_mod, w_in, mlstm_conv, mlstm_gate_bias, mlstm_norm_w, rwkv_mu, rwkv_w0, rwkv_w_up, rwkv_a0, rwkv_a_up, rwkv_g_up, rwkv_k_k, rwkv_k_a, rwkv_r_k, rwkv_ln_w, rwkv_ln_b, mla_q_norm, mla_kv_norm, mla_w_uq, mla_w_ukv, w_out, ln1_w, ln1_b, router_w, router_bias, exp_w_gate, exp_w_up, exp_w_down, sh_w_gate, sh_w_up, sh_w_down, ln2_w, ln2_b):
    b, t_lat, d = x.shape
    n_ctx = ctx.shape[1]
    ta = t_lat + n_ctx
    depth = w_in.shape[0]
    alpha = (2 * depth) ** 0.25
    tm = 256
    cl = 256
    nv_moe = TOP_K * ta // MOE_RB + N_EXPERTS
    assert b + 1 <= 8 and n_ctx % tm == 0 and t_lat % tm == 0 and (TOP_K * ta) % MOE_RB == 0

    cc = jnp.concatenate([c, c_ctx[None], jnp.zeros((8 - b - 1, d), F32)], 0)
    mods = _modulation(cc, w_mod, b_mod)
    cos, sin = _rope_tables(t_lat, n_ctx)
    hsum = jnp.kron(jnp.eye(RWKV_H, dtype=F32), jnp.ones((RWKV_N, RWKV_N), F32))
    hmean = hsum / RWKV_N
    lane = jnp.arange(128, dtype=jnp.int32) // (128 // (b * RWKV_H))
    pats = jnp.broadcast_to((jnp.arange(8, dtype=jnp.int32) * (b * RWKV_H))[:, None, None] + lane, (8, RWKV_N, 128))

    exp_w = [w.astype(BF16).reshape((depth * N_EXPERTS,) + w.shape[2:]) for w in (exp_w_gate, exp_w_up, exp_w_down)]
    xa = jnp.concatenate([x, ctx], axis=1)
    for l in range(depth):
        m_lat = mods[l, :b].reshape(b, 1, 6, d)
        m_ctx = jnp.broadcast_to(mods[l, b].reshape(1, 1, 6, d), (b, 1, 6, d))
        modv = jnp.concatenate([m_lat, m_ctx], axis=1)
        w_ext, wq, wqr, wk, wv = _layer_weights(l, w_in, mla_w_uq, mla_w_ukv)
        mu = rwkv_mu[l].reshape(1, N_RWKV_IN)
        mla = (cos, sin, mla_q_norm[l][None], mla_kv_norm[l][None], wq, wqr, wk, wv)
        zm, zg, zr, q, k, vt = _in_proj(xa, modv, w_ext, mlstm_conv[l], mu, mla, t_lat=t_lat, tm=tm)

        gates = zg[:, :, :4 * MLSTM_H].reshape(b, ta, 2, 8)
        g_col = gates.transpose(0, 2, 1, 3)
        g_row = gates.transpose(0, 2, 3, 1)
        gb = mlstm_gate_bias[l].reshape(2, 8)
        h_m = _mlstm(zm, g_col, g_row, gb.reshape(2, 1, 8), gb.reshape(2, 8, 1), t_lat=t_lat, cl=cl)

        zeros = jnp.zeros((DECAY_LORA, RWKV_W), F32)
        wup = jnp.concatenate([jnp.concatenate([rwkv_w_up[l, 0], zeros], 1),
                               jnp.concatenate([zeros, rwkv_w_up[l, 1]], 1)], 0).astype(BF16)
        aup = jnp.concatenate([jnp.concatenate([rwkv_a_up[l, 0], zeros], 1),
                               jnp.concatenate([zeros, rwkv_a_up[l, 1]], 1)], 0).astype(BF16)
        vec = jnp.concatenate([rwkv_w0[l], rwkv_a0[l], rwkv_k_k[l][None], rwkv_k_a[l][None], rwkv_r_k[l][None],
                               jnp.zeros((1, RWKV_W), F32)], 0)
        pd, v_tok, g_tok, bonus = _rwkv_prep(zr, wup, aup, rwkv_g_up[l].astype(BF16), vec, hsum, tm=128)
        rep = 128 // (b * RWKV_H)
        nvh = RWKV_N // rep
        vs = v_tok.reshape(b, ta, RWKV_H, nvh, rep).transpose(1, 3, 0, 2, 4).reshape(ta, nvh, 128)
        o_f, o_b = _rwkv_scan(pd, pats, vs, t_lat=t_lat, ts=32)
        o_s = jnp.stack([o_f, o_b], 0).reshape(2, ta, nvh, b, RWKV_H, rep)
        o_s = o_s.transpose(0, 3, 1, 4, 2, 5).reshape(2, b, ta, RWKV_W)

        att_l = _attention(q, k, vt, q_off=0, n_q=t_lat, k_off=0, n_keys=ta, tq=tm)
        att_c = _attention(q, k, vt, q_off=t_lat, n_q=n_ctx, k_off=t_lat, n_keys=n_ctx, tq=tm)

        vec_o = jnp.concatenate([mlstm_norm_w[l][None], rwkv_ln_w[l][None], rwkv_ln_b[l][None],
                                 jnp.zeros((5, RWKV_W), F32)], 0)
        ln1 = jnp.stack([ln1_w[l], ln1_b[l]], 0)
        xa = _out_proj(xa, h_m, zm, o_s, g_tok, bonus, att_l, att_c, modv, vec_o, hmean, w_out[l].astype(BF16), ln1,
                       t_lat=t_lat, tm=tm, alpha=alpha)

        ids, gts, h_rows = _router(xa, modv, router_w[l].T, router_bias[l].reshape(N_EXPERTS, 1), t_lat=t_lat, tm=tm)
        tok, dst, g_sorted, tables = _moe_tables(ids, gts, ta=ta, nv=nv_moe, nchunk=d // 128)
        routed = _moe_routed(h_rows, tok, dst, g_sorted, tables, *exp_w, nv=nv_moe, layer=l)
        ln2 = jnp.stack([ln2_w[l], ln2_b[l]], 0)
        xa = _moe_finish(xa, routed, modv, sh_w_gate[l].astype(BF16), sh_w_up[l].astype(BF16),
                         sh_w_down[l].astype(BF16), ln2, t_lat=t_lat, tm=tm, alpha=alpha,
                         n_rows=t_lat if l == depth - 1 else ta)
    return xa
```

```python
import functools

import jax
import jax.numpy as jnp
import numpy as np
from jax import lax
from jax.experimental import pallas as pl
from jax.experimental.pallas import tpu as pltpu

F32 = jnp.float32
BF16 = jnp.bfloat16
HIGHEST = lax.Precision.HIGHEST

GRID_W = 64
MLSTM_H, MLSTM_DH = 4, 64
MLSTM_W = MLSTM_H * MLSTM_DH
RWKV_H, RWKV_N = 4, 64
RWKV_W = RWKV_H * RWKV_N
DECAY_LORA, AAA_LORA, GATE_LORA = 64, 64, 128
RWKV_GN_EPS = 64e-5
MLA_H, MLA_NOPE, MLA_ROPE, MLA_V = 8, 64, 32, 64
Q_RANK, KV_RANK = 256, 128
ROPE_THETA = 10000.0
N_MLSTM_IN = 4 * MLSTM_W + 4 * MLSTM_H
N_RWKV_IN = 3 * RWKV_W + 2 * DECAY_LORA + 2 * AAA_LORA + GATE_LORA
N_EXPERTS, TOP_K, N_GROUPS, TOPK_GROUPS = 64, 8, 8, 4
ROUTED_SCALE = 2.5
LN_EPS = 1e-6
NEG = -1e30

C_QK, C_R, C_V, C_O, C_G, C_A = 0, 512, 1664, 1920, 2176, 2304
N_SHIFT = 1664
N_A = 640
N_EXT = C_A + N_A
HALO = 8
VMEM_LIMIT = 56 * 1024 * 1024


def _cp(sem):
    return pltpu.CompilerParams(dimension_semantics=sem, vmem_limit_bytes=VMEM_LIMIT)


def _ln(x):
    mu = jnp.mean(x, axis=-1, keepdims=True)
    xc = x - mu
    return xc * lax.rsqrt(jnp.mean(xc * xc, axis=-1, keepdims=True) + LN_EPS)


def _sigmoid(x):
    return 1.0 / (1.0 + jnp.exp(-x))


def _silu(x):
    return x * _sigmoid(x)


def _log_sigmoid(x):
    return jnp.minimum(x, 0.0) - jnp.log(1.0 + jnp.exp(-jnp.abs(x)))


def _softplus(x):
    return jnp.maximum(x, 0.0) + jnp.log(1.0 + jnp.exp(-jnp.abs(x)))


def _nt_dot(a, b, **kw):
    return lax.dot_general(a, b, (((1,), (1,)), ((), ())), preferred_element_type=F32, **kw)


def _tn_dot(a, b, **kw):
    return lax.dot_general(a, b, (((0,), (0,)), ((), ())), preferred_element_type=F32, **kw)


def _dot(a, b, **kw):
    return jnp.dot(a, b, preferred_element_type=F32, **kw)


def _split3(a):
    hi = a.astype(BF16)
    r = a - hi.astype(F32)
    mid = r.astype(BF16)
    return hi, mid, (r - mid.astype(F32)).astype(BF16)


def _dot_f32_by_exact(a, b_exact):
    b = b_exact.astype(BF16)
    hi, mid, lo = _split3(a)
    return _dot(hi, b) + _dot(mid, b) + _dot(lo, b)


def _mod_kernel(c_ref, w_ref, b_ref, o_ref):
    o_ref[0] = _dot(_silu(c_ref[...]), w_ref[0], precision=HIGHEST) + b_ref[0]


def _modulation(cc, w_mod, b_mod):
    n_layers, d, n = w_mod.shape
    tn = 1536
    return pl.pallas_call(
        _mod_kernel,
        grid=(n_layers, n // tn),
        in_specs=[pl.BlockSpec((8, d), lambda l, j: (0, 0)),
                  pl.BlockSpec((1, d, tn), lambda l, j: (l, 0, j)),
                  pl.BlockSpec((1, 1, tn), lambda l, j: (l, 0, j))],
        out_specs=pl.BlockSpec((1, 8, tn), lambda l, j: (l, 0, j)),
        out_shape=jax.ShapeDtypeStruct((n_layers, 8, n), F32),
        compiler_params=_cp(("parallel", "parallel")),
        name="modulation",
    )(cc, w_mod, b_mod.reshape(n_layers, 1, n))


def _in_kernel(xp_ref, x_ref, xn_ref, mod_ref, w_ref, conv_ref, mu_ref, cos_ref, sin_ref, nq_ref, nkv_ref,
               wq_ref, wqr_ref, wk_ref, wv_ref, zm_ref, zg_ref, zr_ref, q_ref, k_ref, vt_ref, *, tm, nt_lat, nt):
    i = pl.program_id(1)
    has_prev = jnp.logical_and(i != 0, i != nt_lat)
    has_next = jnp.logical_and(i != nt_lat - 1, i != nt - 1)
    xt = jnp.concatenate([xp_ref[0], x_ref[0], xn_ref[0]], axis=0)
    sh = mod_ref[0, 0, 0:1, :]
    sc = mod_ref[0, 0, 1:2, :]
    h = (_ln(xt) * (1.0 + sc) + sh).astype(BF16)
    z = _dot(h, w_ref[...])
    rows = lax.broadcasted_iota(jnp.int32, (tm + 2 * HALO, 1), 0)
    lo = jnp.where(has_prev, 0, HALO)
    hi = jnp.where(has_next, tm + 2 * HALO, tm + HALO)
    keep = jnp.logical_and(rows >= lo, rows < hi)
    zs = jnp.where(keep, z[:, :N_SHIFT], 0.0)
    zc = zs[HALO:HALO + tm]
    zprev = pltpu.roll(zs, 1, axis=0)[HALO:HALO + tm]
    znext = pltpu.roll(zs, tm + 2 * HALO - 1, axis=0)[HALO:HALO + tm]
    cw = conv_ref[...]
    qk = (cw[0:1] * zprev[:, :C_R] + cw[1:2] * zc[:, :C_R] + cw[2:3] * znext[:, :C_R])
    qk = _silu(qk)
    lane = lax.broadcasted_iota(jnp.int32, (1, C_R), 1)
    qk = qk * jnp.where(lane < MLSTM_W, MLSTM_DH ** -0.5, 1.0)
    zm_ref[0, :, 0:512] = qk
    zm_ref[0, :, 512:1024] = z[HALO:HALO + tm, C_V:C_G]
    zg_ref[0] = z[HALO:HALO + tm, C_G:C_A]
    zr = zc[:, C_R:]
    zr_ref[0] = zr + mu_ref[...] * (0.5 * (zprev[:, C_R:] + znext[:, C_R:]) - zr)
    _mla_project(z[HALO:HALO + tm, C_A:], cos_ref, sin_ref, nq_ref, nkv_ref, wq_ref, wqr_ref, wk_ref, wv_ref,
                 q_ref, k_ref, vt_ref)


def _in_proj(xa, modv, w_ext, conv, mu, mla, *, t_lat, tm):
    b, ta, d = xa.shape
    full = lambda a: pl.BlockSpec(a.shape, lambda bi, i: (0,) * a.ndim)
    rope = pl.BlockSpec((tm, 128), lambda bi, i: (i, 0))
    nt, nt_lat = ta // tm, t_lat // tm
    nh = tm // HALO
    last = ta // HALO - 1
    kern = functools.partial(_in_kernel, tm=tm, nt_lat=nt_lat, nt=nt)
    return pl.pallas_call(
        kern,
        grid=(b, nt),
        in_specs=[pl.BlockSpec((1, HALO, d), lambda bi, i: (bi, jnp.maximum(i * nh - 1, 0), 0)),
                  pl.BlockSpec((1, tm, d), lambda bi, i: (bi, i, 0)),
                  pl.BlockSpec((1, HALO, d), lambda bi, i: (bi, jnp.minimum((i + 1) * nh, last), 0)),
                  pl.BlockSpec((1, 1, 6, d), lambda bi, i: (bi, i // nt_lat, 0, 0)),
                  pl.BlockSpec((d, N_EXT), lambda bi, i: (0, 0)),
                  pl.BlockSpec((3, C_R), lambda bi, i: (0, 0)),
                  pl.BlockSpec((1, N_RWKV_IN), lambda bi, i: (0, 0)),
                  rope, rope] + [full(a) for a in mla[2:]],
        out_specs=[pl.BlockSpec((1, tm, 1024), lambda bi, i: (bi, i, 0)),
                   pl.BlockSpec((1, tm, 128), lambda bi, i: (bi, i, 0)),
                   pl.BlockSpec((1, tm, N_RWKV_IN), lambda bi, i: (bi, i, 0)),
                   pl.BlockSpec((1, tm, 1024), lambda bi, i: (bi, i, 0)),
                   pl.BlockSpec((1, tm, 1024), lambda bi, i: (bi, i, 0)),
                   pl.BlockSpec((1, 512, tm), lambda bi, i: (bi, 0, i))],
        out_shape=[jax.ShapeDtypeStruct((b, ta, 1024), F32),
                   jax.ShapeDtypeStruct((b, ta, 128), F32),
                   jax.ShapeDtypeStruct((b, ta, N_RWKV_IN), F32),
                   jax.ShapeDtypeStruct((b, ta, 1024), BF16),
                   jax.ShapeDtypeStruct((b, ta, 1024), BF16),
                   jax.ShapeDtypeStruct((b, 512, ta), BF16)],
        compiler_params=_cp(("parallel", "parallel")),
        name="in_proj",
    )(xa, xa, xa, modv, w_ext, conv, mu, *mla)


def _mlstm_kernel(zm_ref, gc_ref, gr_ref, bc_ref, br_ref, o_ref, ct_ref, n_ref, m_ref, *, cl):
    d = pl.program_id(1)

    @pl.when(pl.program_id(2) == 0)
    def _():
        ct_ref[...] = jnp.zeros_like(ct_ref)
        n_ref[...] = jnp.zeros_like(n_ref)
        m_ref[...] = jnp.zeros_like(m_ref)

    sgn = 1 - 2 * d
    gc = gc_ref[0, 0] + bc_ref[0]
    gr = gr_ref[0, 0] + br_ref[0]
    lf_c = _log_sigmoid(gc)
    lf_r = _log_sigmoid(gr)
    ii = lax.broadcasted_iota(jnp.int32, (cl, cl), 0)
    jj = lax.broadcasted_iota(jnp.int32, (cl, cl), 1)
    mask = (jj - ii) * sgn <= 0
    mask_t = (ii - jj) * sgn <= 0
    cum_c = _dot(mask.astype(F32), lf_c, precision=HIGHEST)
    cum_r = _dot(lf_r, mask_t.astype(F32), precision=HIGHEST)
    outs = []
    for h in range(MLSTM_H):
        sl = slice(h * MLSTM_DH, (h + 1) * MLSTM_DH)
        q = zm_ref[0, :, sl]
        k = zm_ref[0, :, MLSTM_W + h * MLSTM_DH:MLSTM_W + (h + 1) * MLSTM_DH]
        v = zm_ref[0, :, 2 * MLSTM_W + h * MLSTM_DH:2 * MLSTM_W + (h + 1) * MLSTM_DH]
        qb, kb = q.astype(BF16), k.astype(BF16)
        li_c, li_r = gc[:, h:h + 1], gr[h:h + 1, :]
        cu_c, cu_r = cum_c[:, 4 + h:5 + h], cum_r[4 + h:5 + h, :]
        b_end = jnp.sum(lf_r[4 + h:5 + h, :], axis=1, keepdims=True)
        m_prev = m_ref[h:h + 1, 0:1]
        ct = ct_ref[h]
        nv = n_ref[h:h + 1, 0:MLSTM_DH]
        d_log = jnp.where(mask, cu_c - cu_r + li_r, NEG)
        g_log = cu_c + m_prev
        m_row = jnp.maximum(g_log, jnp.max(d_log, axis=1, keepdims=True))
        w_intra = jnp.exp(d_log - m_row) * _nt_dot(qb, kb)
        e_inter = jnp.exp(g_log - m_row)
        num = _dot(w_intra.astype(BF16), v.astype(BF16)) + e_inter * _dot(qb, ct.astype(BF16))
        den = jnp.sum(w_intra, axis=1, keepdims=True) + e_inter * jnp.sum(q * nv, axis=1, keepdims=True)
        outs.append(num / jnp.maximum(jnp.abs(den), jnp.exp(-m_row)))
        w_end = b_end - cu_c + li_c
        m_loc = jnp.max(w_end, axis=0, keepdims=True)
        e_end = jnp.exp(w_end - m_loc)
        m_new = jnp.maximum(b_end + m_prev, m_loc)
        a_old = jnp.exp(b_end + m_prev - m_new)
        a_loc = jnp.exp(m_loc - m_new)
        ct_ref[h] = a_old * ct + a_loc * _tn_dot(kb, (v * e_end).astype(BF16))
        n_ref[h:h + 1, 0:MLSTM_DH] = a_old * nv + a_loc * jnp.sum(k * e_end, axis=0, keepdims=True)
        m_ref[h:h + 1, :] = jnp.broadcast_to(m_new, (1, 128))
    o_ref[0, 0] = jnp.concatenate(outs, axis=1)


def _mlstm(zm, g_col, g_row, b_col, b_row, *, t_lat, cl):
    b, ta, _ = zm.shape
    nc, nc_lat = ta // cl, t_lat // cl
    nc_ctx = nc - nc_lat

    def chunk(di, c):
        fwd_idx = jnp.where(c < nc_ctx, nc_lat + c, c - nc_ctx)
        return jnp.where(di == 0, fwd_idx, nc - 1 - c)

    return pl.pallas_call(
        functools.partial(_mlstm_kernel, cl=cl),
        grid=(b, 2, nc),
        in_specs=[pl.BlockSpec((1, cl, 1024), lambda bi, di, c: (bi, chunk(di, c), 0)),
                  pl.BlockSpec((1, 1, cl, 8), lambda bi, di, c: (bi, di, chunk(di, c), 0)),
                  pl.BlockSpec((1, 1, 8, cl), lambda bi, di, c: (bi, di, 0, chunk(di, c))),
                  pl.BlockSpec((1, 1, 8), lambda bi, di, c: (di, 0, 0)),
                  pl.BlockSpec((1, 8, 1), lambda bi, di, c: (di, 0, 0))],
        out_specs=pl.BlockSpec((1, 1, cl, MLSTM_W), lambda bi, di, c: (bi, di, chunk(di, c), 0)),
        out_shape=jax.ShapeDtypeStruct((b, 2, ta, MLSTM_W), F32),
        scratch_shapes=[pltpu.VMEM((MLSTM_H, MLSTM_DH, MLSTM_DH), F32),
                        pltpu.VMEM((8, 128), F32),
                        pltpu.VMEM((8, 128), F32)],
        compiler_params=_cp(("parallel", "parallel", "arbitrary")),
        name="mlstm",
    )(zm, g_col, g_row, b_col, b_row)


def _rwkv_prep_kernel(zr_ref, wup_ref, aup_ref, gup_ref, vec_ref, hsum_ref, pd_ref, v_ref, g_ref, bonus_ref):
    nb, tm = zr_ref.shape[0], zr_ref.shape[1]
    nbh = nb * RWKV_H
    vec = vec_ref[...]
    hsum = hsum_ref[...]
    zero = jnp.zeros((tm, RWKV_N), F32)
    for bi in range(nb):
        z = zr_ref[bi]
        zr_, zk, zv = z[:, 0:256], z[:, 256:512], z[:, 512:768]
        zw, za, zg = z[:, 768:896], z[:, 896:1024], z[:, 1024:1152]
        w_raw = _dot(jnp.tanh(zw).astype(BF16), wup_ref[...])
        a_raw = _dot(za.astype(BF16), aup_ref[...])
        g_ref[bi] = _dot(_sigmoid(zg).astype(BF16), gup_ref[...])
        kk = zk * vec[4:5]
        kk_ss = _dot_f32_by_exact(kk * kk, hsum)
        kk = kk * lax.rsqrt(jnp.maximum(kk_ss, 1e-12))
        k_sum = jnp.zeros_like(zk)
        for di in range(2):
            sl = slice(di * RWKV_W, (di + 1) * RWKV_W)
            decay = jnp.exp(-jnp.exp(-_softplus(-(vec[di:di + 1] + w_raw[:, sl])) - 0.5))
            a = _sigmoid(vec[2 + di:3 + di] + a_raw[:, sl])
            k_dir = zk * (1.0 + (a - 1.0) * vec[5:6])
            k_sum = k_sum + k_dir
            ka = kk * a
            for h in range(RWKV_H):
                hs = slice(h * RWKV_N, (h + 1) * RWKV_N)
                rows = pl.ds(bi * RWKV_H + h, tm, stride=nbh)
                pd_ref[di, 0, rows, :] = jnp.concatenate([decay[:, hs], ka[:, hs]], axis=1)
                pd_ref[di, 1, rows, :] = jnp.concatenate([k_dir[:, hs], kk[:, hs]], axis=1)
                pd_ref[di, 2, rows, :] = jnp.concatenate([zr_[:, hs], zero], axis=1)
        v_ref[bi] = zv
        bonus_ref[bi] = _dot_f32_by_exact(zr_ * vec[6:7] * k_sum, hsum) * zv


def _rwkv_prep(zr, wup, aup, gup, vec, hsum, *, tm):
    b, ta, _ = zr.shape
    nbh = b * RWKV_H
    tok = pl.BlockSpec((b, tm, RWKV_W), lambda i: (0, i, 0))
    full = lambda a: pl.BlockSpec(a.shape, lambda i: (0,) * a.ndim)
    return pl.pallas_call(
        _rwkv_prep_kernel,
        grid=(ta // tm,),
        in_specs=[pl.BlockSpec((b, tm, N_RWKV_IN), lambda i: (0, i, 0)),
                  full(wup), full(aup), full(gup), full(vec), full(hsum)],
        out_specs=[pl.BlockSpec((2, 3, tm * nbh, 128), lambda i: (0, 0, i, 0)), tok, tok, tok],
        out_shape=[jax.ShapeDtypeStruct((2, 3, ta * nbh, 128), F32)] + [jax.ShapeDtypeStruct((b, ta, RWKV_W), F32)] * 3,
        compiler_params=_cp(("parallel",)),
        name="rwkv_prep",
    )(zr, wup, aup, gup, vec, hsum)


def _rwkv_scan_kernel(pf_ref, pb_ref, pat_ref, vf_ref, vb_ref, of_ref, ob_ref, st_ref, x_ref, tt_ref, *, ts, nvh, nbh):
    ng = ts // 8
    rows = 8 * nbh

    @pl.when(pl.program_id(0) == 0)
    def _():
        st_ref[...] = jnp.zeros_like(st_ref)

    def transpose(di, p_ref, grp):
        sl = pl.ds(pl.multiple_of(grp * rows, rows), rows)
        for j in range(3):
            tt_ref[di, j] = p_ref[0, j, sl, :].T

    def update(di, u, v_ref, o_ref, t):
        pat = pat_ref[u]
        for j, (tj, half) in enumerate(((0, 0), (0, 1), (1, 0), (1, 1), (2, 0))):
            x_ref[di,j] = jnp.take_along_axis(tt_ref[di, tj, half * RWKV_N:(half + 1) * RWKV_N, :], pat, axis=1)
        vt = v_ref[t]
        nkb = RWKV_N // 8
        kblk = lambda kb: pl.ds(kb * 8, 8)
        acc = [None] * nvh
        for kb in range(nkb):
            kk = x_ref[di,3, kblk(kb), :]
            for vh in range(nvh):
                term = st_ref[di, vh, kblk(kb), :] * kk
                acc[vh] = term if kb == 0 else acc[vh] + term
        sa = [jnp.sum(a, axis=0, keepdims=True) for a in acc]
        out = [None] * nvh
        for kb in range(nkb):
            w, ka = x_ref[di,0, kblk(kb), :], x_ref[di,1, kblk(kb), :]
            k, r = x_ref[di,2, kblk(kb), :], x_ref[di,4, kblk(kb), :]
            for vh in range(nvh):
                sv = st_ref[di, vh, kblk(kb), :] * w - sa[vh] * ka + vt[vh:vh + 1, :] * k
                st_ref[di, vh, kblk(kb), :] = sv
                out[vh] = sv * r if kb == 0 else out[vh] + sv * r
        o_ref[t] = jnp.concatenate([jnp.sum(o, axis=0, keepdims=True) for o in out], axis=0)

    def group(g, carry):
        transpose(0, pf_ref, g)
        transpose(1, pb_ref, ng - 1 - g)
        for u in range(8):
            update(0, u, vf_ref, of_ref, g * 8 + u)
            update(1, 7 - u, vb_ref, ob_ref, ts - 1 - (g * 8 + u))
        return carry

    lax.fori_loop(0, ng, group, 0)


def _rwkv_scan(pd, pats, vs, *, t_lat, ts):
    ta, nvh, nl = vs.shape
    nbh = pd.shape[2] // ta
    nt, nt_lat = ta // ts, t_lat // ts
    fwd = lambda i: (i + nt_lat) % nt
    bwd = lambda i: nt - 1 - i
    return pl.pallas_call(
        functools.partial(_rwkv_scan_kernel, ts=ts, nvh=nvh, nbh=nbh),
        grid=(nt,),
        in_specs=[pl.BlockSpec((1, 3, ts * nbh, 128), lambda i: (0, 0, fwd(i), 0)),
                  pl.BlockSpec((1, 3, ts * nbh, 128), lambda i: (1, 0, bwd(i), 0)),
                  pl.BlockSpec(pats.shape, lambda i: (0, 0, 0)),
                  pl.BlockSpec((ts, nvh, nl), lambda i: (fwd(i), 0, 0)),
                  pl.BlockSpec((ts, nvh, nl), lambda i: (bwd(i), 0, 0))],
        out_specs=[pl.BlockSpec((ts, nvh, nl), lambda i: (fwd(i), 0, 0)),
                   pl.BlockSpec((ts, nvh, nl), lambda i: (bwd(i), 0, 0))],
        out_shape=[jax.ShapeDtypeStruct((ta, nvh, nl), F32)] * 2,
        scratch_shapes=[pltpu.VMEM((2, nvh, RWKV_N, nl), F32), pltpu.VMEM((2, 5, RWKV_N, nl), F32),
                        pltpu.VMEM((2, 3, 128, 8 * nbh), F32)],
        compiler_params=_cp(("arbitrary",)),
        name="rwkv_scan",
    )(pd, pd, pats, vs, vs)


def _mla_project(za, cos_ref, sin_ref, nq_ref, nkv_ref, wq_ref, wqr_ref, wk_ref, wv_ref, q_ref, k_ref, v_ref):
    cq, ckv = za[:, 0:Q_RANK], za[:, Q_RANK:Q_RANK + KV_RANK]
    kr, krr = za[:, 384:512], za[:, 512:640]
    cos, sin = cos_ref[...], sin_ref[...]
    cqn = (cq * lax.rsqrt(jnp.mean(cq * cq, axis=-1, keepdims=True) + 1e-6) * nq_ref[...]).astype(BF16)
    ckvn = (ckv * lax.rsqrt(jnp.mean(ckv * ckv, axis=-1, keepdims=True) + 1e-6) * nkv_ref[...]).astype(BF16)
    q = _dot(cqn, wq_ref[...])
    qr = _dot(cqn, wqr_ref[...])
    kn = _dot(ckvn, wk_ref[...])
    v_ref[0] = _nt_dot(wv_ref[...], ckvn).astype(BF16)
    k_rope = kr * cos + krr * sin
    scale = (MLA_NOPE + MLA_ROPE) ** -0.5 * np.log2(np.e)
    for h in range(MLA_H):
        sl = slice(h * 128, (h + 1) * 128)
        q_ref[0, :, sl] = ((q[:, sl] * cos + qr[:, sl] * sin) * scale).astype(BF16)
        k_ref[0, :, sl] = (kn[:, sl] + k_rope).astype(BF16)


def _attn_kernel(q_ref, k_ref, vt_ref, o_ref):
    outs = []
    for h in range(ATTN_HEADS):
        sl = slice(h * 128, (h + 1) * 128)
        pair = h // 2
        st = _nt_dot(k_ref[0, :, sl], q_ref[0, :, sl])
        m = jnp.max(st, axis=0, keepdims=True)
        n_keys = st.shape[0]
        l, ot = 0.0, 0.0
        for c0 in range(0, n_keys, ATTN_KEY_CHUNK):
            c1 = min(c0 + ATTN_KEY_CHUNK, n_keys)
            p = jnp.exp2(st[c0:c1] - m)
            l = l + jnp.sum(p, axis=0, keepdims=True)
            ot = ot + _dot(vt_ref[0, pair * 128:(pair + 1) * 128, c0:c1], p.astype(BF16))
        outs.append(ot[(h % 2) * MLA_V:(h % 2 + 1) * MLA_V] / l)
    o_ref[0] = jnp.concatenate(outs, axis=0).T


ATTN_HEADS = 4
ATTN_KEY_CHUNK = 1024


def _attention(q, k, vt, *, q_off, n_q, k_off, n_keys, tq):
    b = q.shape[0]
    qo, ko = q_off // tq, k_off // n_keys
    hq, hv = ATTN_HEADS * 128, ATTN_HEADS * MLA_V
    return pl.pallas_call(
        _attn_kernel,
        grid=(b, MLA_H // ATTN_HEADS, n_q // tq),
        in_specs=[pl.BlockSpec((1, tq, hq), lambda bi, hp, i: (bi, qo + i, hp)),
                  pl.BlockSpec((1, n_keys, hq), lambda bi, hp, i: (bi, ko, hp)),
                  pl.BlockSpec((1, hv, n_keys), lambda bi, hp, i: (bi, hp, ko))],
        out_specs=pl.BlockSpec((1, tq, hv), lambda bi, hp, i: (bi, i, hp)),
        out_shape=jax.ShapeDtypeStruct((b, n_q, 512), F32),
        compiler_params=_cp(("parallel", "parallel", "parallel")),
        name="mla_attention",
    )(q, k, vt)


def _out_kernel(x_ref, hf_ref, hb_ref, zm_ref, of_ref, ob_ref, g_ref, bonus_ref, attl_ref, attc_ref, mod_ref, vec_ref,
                hmean_ref, wo_ref, ln_ref, o_ref, *, alpha, nt_lat):
    hmean = hmean_ref[...]
    att = jnp.where(pl.program_id(1) < nt_lat, attl_ref[0], attc_ref[0])

    def head_norm(y, eps):
        mu = _dot_f32_by_exact(y, hmean)
        yc = y - mu
        return yc * lax.rsqrt(_dot_f32_by_exact(yc * yc, hmean) + eps)

    vec = vec_ref[...]
    zo = zm_ref[0]
    m_mix = _sigmoid(zo) * (head_norm(hf_ref[0, 0] + hb_ref[0, 0], 1e-6) * vec[0:1])
    r_o = head_norm(of_ref[0, 0] + ob_ref[0, 0], RWKV_GN_EPS) * vec[1:2] + vec[2:3] + bonus_ref[0]
    r_mix = r_o * g_ref[0]
    mix = jnp.concatenate([m_mix, r_mix, att], axis=1).astype(BF16)
    y = _dot(mix, wo_ref[...])
    gate = mod_ref[0, 0, 2:3, :]
    o_ref[0] = _ln(alpha * x_ref[0] + gate * y) * ln_ref[0:1] + ln_ref[1:2]


def _out_proj(xa, h_m, zm, o_r, g, bonus, att_l, att_c, modv, vec, hmean, wo, ln, *, t_lat, tm, alpha):
    b, ta, d = xa.shape
    nt_lat = t_lat // tm
    tok = lambda w: pl.BlockSpec((1, tm, w), lambda bi, i: (bi, i, 0))
    full = lambda a: pl.BlockSpec(a.shape, lambda bi, i: (0,) * a.ndim)
    return pl.pallas_call(
        functools.partial(_out_kernel, alpha=alpha, nt_lat=nt_lat),
        grid=(b, ta // tm),
        in_specs=[tok(d),
                  pl.BlockSpec((1, 1, tm, MLSTM_W), lambda bi, i: (bi, 0, i, 0)),
                  pl.BlockSpec((1, 1, tm, MLSTM_W), lambda bi, i: (bi, 1, i, 0)),
                  pl.BlockSpec((1, tm, MLSTM_W), lambda bi, i: (bi, i, 3)),
                  pl.BlockSpec((1, 1, tm, RWKV_W), lambda bi, i: (0, bi, i, 0)),
                  pl.BlockSpec((1, 1, tm, RWKV_W), lambda bi, i: (1, bi, i, 0)),
                  tok(RWKV_W), tok(RWKV_W),
                  pl.BlockSpec((1, tm, 512), lambda bi, i: (bi, jnp.minimum(i, nt_lat - 1), 0)),
                  pl.BlockSpec((1, tm, 512), lambda bi, i: (bi, jnp.maximum(i - nt_lat, 0), 0)),
                  pl.BlockSpec((1, 1, 6, d), lambda bi, i: (bi, i // nt_lat, 0, 0)),
                  full(vec), full(hmean), full(wo), full(ln)],
        out_specs=tok(d),
        out_shape=jax.ShapeDtypeStruct((b, ta, d), F32),
        compiler_params=_cp(("parallel", "parallel")),
        name="out_proj",
    )(xa, h_m, h_m, zm, o_r, o_r, g, bonus, att_l, att_c, modv, vec, hmean, wo, ln)


def _first_max(x, idx, axis):
    mx = jnp.max(x, axis=axis, keepdims=True)
    first = jnp.min(jnp.where(x == mx, idx, 1 << 20), axis=axis, keepdims=True)
    return mx, first, idx == first


def _router_kernel(x_ref, mod_ref, w_ref, bias_ref, id_ref, gate_ref, h_ref):
    tm = x_ref.shape[1]
    sh, sc = mod_ref[0, 0, 3:4, :], mod_ref[0, 0, 4:5, :]
    h = _ln(x_ref[0]) * (1.0 + sc) + sh
    for j in range(h.shape[1] // 128):
        h_ref[0, pl.ds(j, tm, stride=h.shape[1] // 128), :] = h[:, j * 128:(j + 1) * 128]
    scores = _sigmoid(_nt_dot(w_ref[...], h, precision=HIGHEST))
    biased = scores + bias_ref[...]
    gsz = N_EXPERTS // N_GROUPS
    b3 = biased.reshape(N_GROUPS, gsz, tm)
    s3 = scores.reshape(N_GROUPS, gsz, tm)
    e_idx = lax.broadcasted_iota(jnp.int32, (N_GROUPS, gsz, tm), 1)
    m1, _, hit = _first_max(b3, e_idx, 1)
    m2 = jnp.max(jnp.where(hit, -jnp.inf, b3), axis=1, keepdims=True)
    gscore = m1 + m2
    g_idx = lax.broadcasted_iota(jnp.int32, (N_GROUPS, 1, tm), 0)
    gsel = jnp.zeros((N_GROUPS, 1, tm), F32)
    for _ in range(TOPK_GROUPS):
        _, _, hit = _first_max(jnp.where(gsel > 0, -jnp.inf, gscore), g_idx, 0)
        gsel = jnp.where(hit, 1.0, gsel)
    cand = jnp.where(jnp.broadcast_to(gsel, b3.shape) > 0, b3, -jnp.inf)
    x_idx = g_idx * gsz + e_idx
    sel = jnp.zeros(b3.shape, F32)
    ids, picked = [], []
    for _ in range(TOP_K):
        _, first, hit = _first_max(jnp.where(sel > 0, -jnp.inf, cand), x_idx, (0, 1))
        sel = jnp.where(hit, 1.0, sel)
        ids.append(first[0])
        picked.append(jnp.sum(jnp.where(hit, s3, 0.0), axis=(0, 1), keepdims=True)[0])
    picked = jnp.concatenate(picked, axis=0)
    id_ref[0] = jnp.concatenate(ids, axis=0)
    gate_ref[0] = ROUTED_SCALE * picked / jnp.sum(picked, axis=0, keepdims=True)


def _router(xa, modv, w_t, bias, *, t_lat, tm):
    b, ta, d = xa.shape
    nt_lat = t_lat // tm
    return pl.pallas_call(
        _router_kernel,
        grid=(b, ta // tm),
        in_specs=[pl.BlockSpec((1, tm, d), lambda bi, i: (bi, i, 0)),
                  pl.BlockSpec((1, 1, 6, d), lambda bi, i: (bi, i // nt_lat, 0, 0)),
                  pl.BlockSpec((N_EXPERTS, d), lambda bi, i: (0, 0)),
                  pl.BlockSpec((N_EXPERTS, 1), lambda bi, i: (0, 0))],
        out_specs=[pl.BlockSpec((1, TOP_K, tm), lambda bi, i: (bi, 0, i)),
                   pl.BlockSpec((1, TOP_K, tm), lambda bi, i: (bi, 0, i)),
                   pl.BlockSpec((1, tm * (d // 128), 128), lambda bi, i: (bi, i, 0))],
        out_shape=[jax.ShapeDtypeStruct((b, TOP_K, ta), jnp.int32),
                   jax.ShapeDtypeStruct((b, TOP_K, ta), F32),
                   jax.ShapeDtypeStruct((b, ta * (d // 128), 128), F32)],
        compiler_params=_cp(("parallel", "parallel")),
        name="router",
    )(xa, modv, w_t, bias)


MOE_RB = 256
MOE_PITCH = MOE_RB + 8
MOE_U = 16


def _moe_routed_kernel(vblk_ref, vexp_ref, vlo_ref, vhi_ref, vvalid_ref, vfirst_ref,
                       tok_ref, dst_ref, gate_ref, src_ref, wg_ref, wu_ref, wd_ref, acc_ref, xt_ref, yt_ref, xb_ref,
                       gc_ref, *, nv, nchunk):
    bi, v = pl.program_id(0), pl.program_id(1)
    pos = bi * nv + v
    lo, hi = vlo_ref[pos], vhi_ref[pos]

    @pl.when(v == 0)
    def _():
        acc_ref[...] = jnp.zeros_like(acc_ref)

    @pl.when(vfirst_ref[pos] == 1)
    def _():
        for r in range(MOE_RB):
            off = pl.multiple_of(tok_ref[0, 0, 0, r], nchunk)
            xt_ref[pl.ds(r, nchunk, stride=MOE_PITCH), :] = src_ref[0, pl.ds(off, nchunk), :]
        for j in range(nchunk):
            xb_ref[:, j * 128:(j + 1) * 128] = xt_ref[j * MOE_PITCH:j * MOE_PITCH + MOE_RB, :].astype(BF16)
        ii = lax.broadcasted_iota(jnp.int32, (MOE_RB, MOE_RB), 0)
        jj = lax.broadcasted_iota(jnp.int32, (MOE_RB, MOE_RB), 1)
        gc_ref[...] = jnp.sum(jnp.where(ii == jj, gate_ref[0, 0], 0.0), axis=1, keepdims=True)

    @pl.when(vvalid_ref[pos] == 1)
    def _():
        x = xb_ref[...]
        row = lax.broadcasted_iota(jnp.int32, (MOE_RB, 1), 0)
        gcol = jnp.where(jnp.logical_and(row >= lo, row < hi), gc_ref[...], 0.0)
        a = _silu(_dot(x, wg_ref[0])) * _dot(x, wu_ref[0]) * gcol
        y = _dot(a.astype(BF16), wd_ref[0])
        for j in range(nchunk):
            yt_ref[j * MOE_PITCH:j * MOE_PITCH + MOE_RB, :] = y[:, j * 128:(j + 1) * 128]
        for g in range(MOE_RB // MOE_U):
            pending = []
            for u in range(MOE_U):
                r = g * MOE_U + u
                off = pl.multiple_of(dst_ref[0, 0, 0, r], nchunk)
                pending.append((off, acc_ref[0, pl.ds(off, nchunk), :] + yt_ref[pl.ds(r, nchunk, stride=MOE_PITCH), :]))
            for off, val in pending:
                acc_ref[0, pl.ds(off, nchunk), :] = val


def _moe_routed(h_rows, tok, dst, gate, tables, wg, wu, wd, *, nv, layer):
    b, rows, _ = h_rows.shape
    d, de = wg.shape[1], wg.shape[2]
    nchunk = d // 128
    nblk = tok.shape[1] // MOE_RB
    blk = lambda bi, v, vblk, *_: (bi, vblk[bi * nv + v], 0, 0)
    exp = lambda bi, v, vblk, vexp, *_: (layer * N_EXPERTS + vexp[bi * nv + v], 0, 0)
    grid_spec = pltpu.PrefetchScalarGridSpec(
        num_scalar_prefetch=6,
        grid=(b, nv),
        in_specs=[pl.BlockSpec((1, 1, 1, MOE_RB), blk, memory_space=pltpu.SMEM),
                  pl.BlockSpec((1, 1, 1, MOE_RB), lambda bi, v, *_: (bi, v, 0, 0), memory_space=pltpu.SMEM),
                  pl.BlockSpec((1, 1, 1, MOE_RB), blk),
                  pl.BlockSpec((1, rows, 128), lambda bi, v, *_: (bi, 0, 0), pipeline_mode=pl.Buffered(1)),
                  pl.BlockSpec((1, d, de), exp),
                  pl.BlockSpec((1, d, de), exp),
                  pl.BlockSpec((1, de, d), exp)],
        out_specs=pl.BlockSpec((1, rows + MOE_U * nchunk, 128), lambda bi, v, *_: (bi, 0, 0),
                               pipeline_mode=pl.Buffered(1)),
        scratch_shapes=[pltpu.VMEM((nchunk * MOE_PITCH, 128), F32), pltpu.VMEM((nchunk * MOE_PITCH, 128), F32),
                        pltpu.VMEM((MOE_RB, d), BF16), pltpu.VMEM((MOE_RB, 1), F32)],
    )
    return pl.pallas_call(
        functools.partial(_moe_routed_kernel, nv=nv, nchunk=nchunk),
        grid_spec=grid_spec,
        out_shape=jax.ShapeDtypeStruct((b, rows + MOE_U * nchunk, 128), F32),
        compiler_params=_cp(("parallel", "arbitrary")),
        name="moe_routed",
    )(*tables, tok.reshape(b, nblk, 1, MOE_RB), dst.reshape(b, nv, 1, MOE_RB), gate.reshape(b, nblk, 1, MOE_RB),
      h_rows, wg, wu, wd)


def _moe_tables(ids, gates, *, ta, nv, nchunk):
    b = ids.shape[0]
    n = TOP_K * ta
    nblk = n // MOE_RB
    t_idx = jnp.broadcast_to(jnp.arange(ta, dtype=jnp.int32), (b, TOP_K, ta))
    keys, g_sorted = lax.sort(((ids * ta + t_idx).reshape(b, n), gates.reshape(b, n)), dimension=1, num_keys=1)
    tok, e_sorted = keys % ta, keys // ta
    ends = jnp.sum(e_sorted[:, None, :] <= jnp.arange(N_EXPERTS, dtype=jnp.int32)[None, :, None], axis=2)
    starts = ends - jnp.sum(e_sorted[:, None, :] == jnp.arange(N_EXPERTS, dtype=jnp.int32)[None, :, None], axis=2)
    e_lo, e_hi = e_sorted[:, ::MOE_RB], e_sorted[:, MOE_RB - 1::MOE_RB]
    n_vis = e_hi - e_lo + 1
    v_end = jnp.cumsum(n_vis, axis=1)
    v_start = v_end - n_vis
    v = jnp.arange(nv, dtype=jnp.int32)
    blk = jnp.minimum(jnp.sum(v_end[:, None, :] <= v[None, :, None], axis=2), nblk - 1).astype(jnp.int32)
    def take(a, i):
        pick = i[:, :, None] == jnp.arange(a.shape[1], dtype=jnp.int32)[None, None, :]
        return jnp.sum(jnp.where(pick, a[:, None, :], 0), axis=2)
    valid = v[None, :] < v_end[:, -1:]
    exp = jnp.where(valid, take(e_lo, blk) + v[None, :] - take(v_start, blk), take(e_hi, blk)).astype(jnp.int32)
    lo = jnp.clip(take(starts, exp) - blk * MOE_RB, 0, MOE_RB)
    hi = jnp.where(valid, jnp.clip(take(ends, exp) - blk * MOE_RB, 0, MOE_RB), 0)
    first = jnp.logical_and(valid, v[None, :] == take(v_start, blk))
    flat = lambda a: a.astype(jnp.int32).reshape(-1)
    r = jnp.arange(MOE_RB, dtype=jnp.int32)
    tok_v = jnp.take_along_axis(tok.reshape(b, nblk, MOE_RB), blk[:, :, None], axis=1)
    mine = jnp.logical_and(r >= lo[:, :, None], r < hi[:, :, None])
    dst = jnp.where(mine, tok_v, ta + r % MOE_U)
    rows = nchunk
    return ((tok * rows).astype(jnp.int32), (dst * rows).astype(jnp.int32), g_sorted,
            tuple(flat(a) for a in (blk, exp, lo, hi, valid, first)))


def _moe_finish_kernel(x_ref, r_ref, mod_ref, sg_ref, su_ref, sd_ref, ln_ref, o_ref, *, alpha):
    tm, d = x_ref.shape[1], x_ref.shape[2]
    nchunk = d // 128
    x = x_ref[0]
    h = (_ln(x) * (1.0 + mod_ref[0, 0, 4:5, :]) + mod_ref[0, 0, 3:4, :]).astype(BF16)
    shared = _dot((_silu(_dot(h, sg_ref[...])) * _dot(h, su_ref[...])).astype(BF16), sd_ref[...])
    routed = jnp.concatenate([r_ref[0, pl.ds(j, tm, stride=nchunk), :] for j in range(nchunk)], axis=1)
    o_ref[0] = _ln(alpha * x + mod_ref[0, 0, 5:6, :] * (routed + shared)) * ln_ref[0:1] + ln_ref[1:2]


def _moe_finish(xa, routed, modv, sg, su, sd, ln, *, t_lat, tm, alpha, n_rows):
    b, ta, d = xa.shape
    nt_lat = t_lat // tm
    full = lambda a: pl.BlockSpec(a.shape, lambda bi, i: (0,) * a.ndim)
    return pl.pallas_call(
        functools.partial(_moe_finish_kernel, alpha=alpha),
        grid=(b, n_rows // tm),
        in_specs=[pl.BlockSpec((1, tm, d), lambda bi, i: (bi, i, 0)),
                  pl.BlockSpec((1, tm * (d // 128), 128), lambda bi, i: (bi, i, 0)),
                  pl.BlockSpec((1, 1, 6, d), lambda bi, i: (bi, i // nt_lat, 0, 0)),
                  full(sg), full(su), full(sd), full(ln)],
        out_specs=pl.BlockSpec((1, tm, d), lambda bi, i: (bi, i, 0)),
        out_shape=jax.ShapeDtypeStruct((b, n_rows, d), F32),
        compiler_params=_cp(("parallel", "parallel")),
        name="moe_finish",
    )(xa, routed, modv, sg, su, sd, ln)


def _rope_blocks(w_rope):
    k = w_rope.shape[0]
    ev, od = w_rope[:, 0::2], w_rope[:, 1::2]
    z64, z32 = jnp.zeros((k, 64), w_rope.dtype), jnp.zeros((k, 32), w_rope.dtype)
    return (jnp.concatenate([z64, ev, od, z32], 1), jnp.concatenate([z64, -od, ev, z32], 1))


def _layer_weights(l, w_in, mla_w_uq, mla_w_ukv):
    w = w_in[l]
    d = w.shape[0]
    wm, wr, wa = w[:, :N_MLSTM_IN], w[:, N_MLSTM_IN:N_MLSTM_IN + N_RWKV_IN], w[:, N_MLSTM_IN + N_RWKV_IN:]
    kr, krr = _rope_blocks(wa[:, Q_RANK + KV_RANK:])
    w_ext = jnp.concatenate([
        wm[:, :2 * MLSTM_W], wr, wm[:, 2 * MLSTM_W:4 * MLSTM_W],
        wm[:, 4 * MLSTM_W:], jnp.zeros((d, 128 - 4 * MLSTM_H), w.dtype),
        wa[:, :Q_RANK + KV_RANK], kr, krr], axis=1).astype(BF16)
    uq = mla_w_uq[l].reshape(Q_RANK, MLA_H, MLA_NOPE + MLA_ROPE)
    nope, ev, od = uq[:, :, :MLA_NOPE], uq[:, :, MLA_NOPE::2], uq[:, :, MLA_NOPE + 1::2]
    z64, z32 = jnp.zeros((Q_RANK, MLA_H, 64), F32), jnp.zeros((Q_RANK, MLA_H, 32), F32)
    wq = jnp.concatenate([nope, ev, od, z32], axis=2).reshape(Q_RANK, MLA_H * 128)
    wqr = jnp.concatenate([z64, -od, ev, z32], axis=2).reshape(Q_RANK, MLA_H * 128)
    ukv = mla_w_ukv[l].reshape(KV_RANK, MLA_H, MLA_NOPE + MLA_V)
    wk = jnp.concatenate([ukv[:, :, :MLA_NOPE], jnp.zeros((KV_RANK, MLA_H, 64), F32)], axis=2).reshape(KV_RANK, MLA_H * 128)
    wv = ukv[:, :, MLA_NOPE:].reshape(KV_RANK, MLA_H * MLA_V)
    return w_ext, wq.astype(BF16), wqr.astype(BF16), wk.astype(BF16), wv.T.astype(BF16)


def _rope_tables(t_lat, n_ctx):
    rows = t_lat // GRID_W
    row = jnp.repeat(jnp.arange(rows), GRID_W).astype(F32)
    col = jnp.tile(jnp.arange(GRID_W), rows).astype(F32)
    n_freq = MLA_ROPE // 4
    freq = ROPE_THETA ** (-jnp.arange(n_freq, dtype=F32) / n_freq)
    ang = jnp.concatenate([row[:, None] * freq, col[:, None] * freq], -1)
    cos, sin = jnp.cos(ang), jnp.sin(ang)
    one, zero = jnp.ones((t_lat, 64), F32), jnp.zeros((t_lat, 32), F32)
    cos_l = jnp.concatenate([one, cos, cos, zero], 1)
    sin_l = jnp.concatenate([0 * one, sin, sin, zero], 1)
    cos_c = jnp.concatenate([jnp.ones((n_ctx, 96), F32), jnp.zeros((n_ctx, 32), F32)], 1)
    return jnp.concatenate([cos_l, cos_c], 0), jnp.concatenate([sin_l, jnp.zeros((n_ctx, 128), F32)], 0)


def kernel(x, c, ctx, c_ctx, w_mod, b_mod, w_in, mlstm_conv, mlstm_gate_bias, mlstm_norm_w, rwkv_mu, rwkv_w0, rwkv_w_up, rwkv_a0, rwkv_a_up, rwkv_g_up, rwkv_k_k, rwkv_k_a, rwkv_r_k, rwkv_ln_w, rwkv_ln_b, mla_q_norm, mla_kv_norm, mla_w_uq, mla_w_ukv, w_out, ln1_w, ln1_b, router_w, router_bias, exp_w_gate, exp_w_up, exp_w_down, sh_w_gate, sh_w_up, sh_w_down, ln2_w, ln2_b):
    b, t_lat, d = x.shape
    n_ctx = ctx.shape[1]
    ta = t_lat + n_ctx
    depth = w_in.shape[0]
    alpha = (2 * depth) ** 0.25
    tm = 256
    cl = 256
    nv_moe = TOP_K * ta // MOE_RB + N_EXPERTS
    assert b + 1 <= 8 and n_ctx % tm == 0 and t_lat % tm == 0 and (TOP_K * ta) % MOE_RB == 0

    cc = jnp.concatenate([c, c_ctx[None], jnp.zeros((8 - b - 1, d), F32)], 0)
    mods = _modulation(cc, w_mod, b_mod)
    cos, sin = _rope_tables(t_lat, n_ctx)
    hsum = jnp.kron(jnp.eye(RWKV_H, dtype=F32), jnp.ones((RWKV_N, RWKV_N), F32))
    hmean = hsum / RWKV_N
    lane = jnp.arange(128, dtype=jnp.int32) // (128 // (b * RWKV_H))
    pats = jnp.broadcast_to((jnp.arange(8, dtype=jnp.int32) * (b * RWKV_H))[:, None, None] + lane, (8, RWKV_N, 128))

    exp_w = [w.astype(BF16).reshape((depth * N_EXPERTS,) + w.shape[2:]) for w in (exp_w_gate, exp_w_up, exp_w_down)]
    xa = jnp.concatenate([x, ctx], axis=1)
    for l in range(depth):
        m_lat = mods[l, :b].reshape(b, 1, 6, d)
        m_ctx = jnp.broadcast_to(mods[l, b].reshape(1, 1, 6, d), (b, 1, 6, d))
        modv = jnp.concatenate([m_lat, m_ctx], axis=1)
        w_ext, wq, wqr, wk, wv = _layer_weights(l, w_in, mla_w_uq, mla_w_ukv)
        mu = rwkv_mu[l].reshape(1, N_RWKV_IN)
        mla = (cos, sin, mla_q_norm[l][None], mla_kv_norm[l][None], wq, wqr, wk, wv)
        zm, zg, zr, q, k, vt = _in_proj(xa, modv, w_ext, mlstm_conv[l], mu, mla, t_lat=t_lat, tm=tm)

        gates = zg[:, :, :4 * MLSTM_H].reshape(b, ta, 2, 8)
        g_col = gates.transpose(0, 2, 1, 3)
        g_row = gates.transpose(0, 2, 3, 1)
        gb = mlstm_gate_bias[l].reshape(2, 8)
        h_m = _mlstm(zm, g_col, g_row, gb.reshape(2, 1, 8), gb.reshape(2, 8, 1), t_lat=t_lat, cl=cl)

        zeros = jnp.zeros((DECAY_LORA, RWKV_W), F32)
        wup = jnp.concatenate([jnp.concatenate([rwkv_w_up[l, 0], zeros], 1),
                               jnp.concatenate([zeros, rwkv_w_up[l, 1]], 1)], 0).astype(BF16)
        aup = jnp.concatenate([jnp.concatenate([rwkv_a_up[l, 0], zeros], 1),
                               jnp.concatenate([zeros, rwkv_a_up[l, 1]], 1)], 0).astype(BF16)
        vec = jnp.concatenate([rwkv_w0[l], rwkv_a0[l], rwkv_k_k[l][None], rwkv_k_a[l][None], rwkv_r_k[l][None],
                               jnp.zeros((1, RWKV_W), F32)], 0)
        pd, v_tok, g_tok, bonus = _rwkv_prep(zr, wup, aup, rwkv_g_up[l].astype(BF16), vec, hsum, tm=128)
        rep = 128 // (b * RWKV_H)
        nvh = RWKV_N // rep
        vs = v_tok.reshape(b, ta, RWKV_H, nvh, rep).transpose(1, 3, 0, 2, 4).reshape(ta, nvh, 128)
        o_f, o_b = _rwkv_scan(pd, pats, vs, t_lat=t_lat, ts=32)
        o_s = jnp.stack([o_f, o_b], 0).reshape(2, ta, nvh, b, RWKV_H, rep)
        o_s = o_s.transpose(0, 3, 1, 4, 2, 5).reshape(2, b, ta, RWKV_W)

        att_l = _attention(q, k, vt, q_off=0, n_q=t_lat, k_off=0, n_keys=ta, tq=tm)
        att_c = _attention(q, k, vt, q_off=t_lat, n_q=n_ctx, k_off=t_lat, n_keys=n_ctx, tq=tm)

        vec_o = jnp.concatenate([mlstm_norm_w[l][None], rwkv_ln_w[l][None], rwkv_ln_b[l][None],
                                 jnp.zeros((5, RWKV_W), F32)], 0)
        ln1 = jnp.stack([ln1_w[l], ln1_b[l]], 0)
        xa = _out_proj(xa, h_m, zm, o_s, g_tok, bonus, att_l, att_c, modv, vec_o, hmean, w_out[l].astype(BF16), ln1,
                       t_lat=t_lat, tm=tm, alpha=alpha)

        ids, gts, h_rows = _router(xa, modv, router_w[l].T, router_bias[l].reshape(N_EXPERTS, 1), t_lat=t_lat, tm=tm)
        tok, dst, g_sorted, tables = _moe_tables(ids, gts, ta=ta, nv=nv_moe, nchunk=d // 128)
        routed = _moe_routed(h_rows, tok, dst, g_sorted, tables, *exp_w, nv=nv_moe, layer=l)
        ln2 = jnp.stack([ln2_w[l], ln2_b[l]], 0)
        xa = _moe_finish(xa, routed, modv, sh_w_gate[l].astype(BF16), sh_w_up[l].astype(BF16),
                         sh_w_down[l].astype(BF16), ln2, t_lat=t_lat, tm=tm, alpha=alpha,
                         n_rows=t_lat if l == depth - 1 else ta)
    return xa
```
